```python
import math
import jax, jax.numpy as jnp
from jax import lax
import numpy as np

D_MODEL = 1024
BATCH = 8
SEQ = 2048
DEPTH = 2
DEC_BATCH = 128
DEC_SEQ = 4
PAST_LEN = 16384
PAGE_SIZE = 128

D_MIX = 2 * D_MODEL
D_POOL = D_MIX // 4
POOL_WINDOWS = (2, 4, 8, 16)
POOL_GROUPS = len(POOL_WINDOWS)
POOL_GROUP = D_POOL // POOL_GROUPS
POOL_BUF = max(POOL_WINDOWS) - 1
D_SSM = D_MIX - D_POOL
HEAD_DIM = 64
N_HEADS = D_SSM // HEAD_DIM
N_GROUPS = 4
HEADS_PER_GROUP = N_HEADS // N_GROUPS
D_STATE = 128
CONV_W = 4
CHUNK = 128
D_XBC = D_SSM + 2 * N_GROUPS * D_STATE
D_IN_PROJ = D_POOL + D_SSM + D_XBC + N_HEADS
D_FF = 2816
N_EXPERTS = 8
TOP_K = 2
D_FF_EXPERT = 2816
N_DENSE = (DEPTH + 1) // 2
N_MOE = DEPTH // 2
ALPHA = (2.0 * DEPTH) ** 0.25
BETA = (8.0 * DEPTH) ** -0.25
LN_EPS = 1e-5
RMS_EPS = 1e-6

kernel_name = "pool_ssd_hybrid_decode_step"


def layer_norm(x, g, b):
    xf = x.astype(jnp.float32)
    mu = jnp.mean(xf, -1, keepdims=True)
    var = jnp.mean(jnp.square(xf - mu), -1, keepdims=True)
    return ((xf - mu) * lax.rsqrt(var + LN_EPS) * g + b).astype(x.dtype)


def pool_mixer(u, buf, start, w, scale):
    bsz, L, _ = u.shape
    up = jnp.concatenate([buf.astype(u.dtype), u], axis=1).astype(jnp.float32)
    cs = jnp.concatenate([jnp.zeros_like(up[:, :1]), jnp.cumsum(up, axis=1)], axis=1)
    hi = cs[:, POOL_BUF + 1:]
    pos = start + jnp.arange(L, dtype=jnp.int32)
    means = []
    for g, win in enumerate(POOL_WINDOWS):
        sl = slice(g * POOL_GROUP, (g + 1) * POOL_GROUP)
        lo = cs[:, POOL_BUF + 1 - win:POOL_BUF + 1 - win + L, sl]
        cnt = jnp.minimum(pos + 1, win).astype(jnp.float32)
        means.append((hi[:, :, sl] - lo) / cnt[None, :, None])
    mean = jnp.concatenate(means, axis=-1)
    diff = (mean - up[:, POOL_BUF:]).reshape(bsz, L, POOL_GROUPS, POOL_GROUP).astype(u.dtype)
    out = jnp.einsum('blgc,gcd->blgd', diff, w).reshape(bsz, L, D_POOL) * scale
    new_buf = up[:, -POOL_BUF:].astype(buf.dtype)
    return out.astype(u.dtype), new_buf


def ssd_scan(x, dt, A, Bm, Cm, h0):
    bsz, L = x.shape[:2]
    Q = min(CHUNK, L)
    pad = (-L) % Q
    if pad:
        def pw(a):
            return jnp.pad(a, [(0, 0), (0, pad)] + [(0, 0)] * (a.ndim - 2))
        x, dt, Bm, Cm = pw(x), pw(dt), pw(Bm), pw(Cm)
    nc = (L + pad) // Q
    G, R, P, N = N_GROUPS, HEADS_PER_GROUP, HEAD_DIM, D_STATE
    xc = x.reshape(bsz, nc, Q, G, R, P).astype(jnp.float32)
    dtc = dt.reshape(bsz, nc, Q, G, R)
    Bc = Bm.reshape(bsz, nc, Q, G, N).astype(jnp.float32)
    Cc = Cm.reshape(bsz, nc, Q, G, N).astype(jnp.float32)
    cum = jnp.cumsum(dtc * A.reshape(G, R), axis=2)
    seg = cum[:, :, :, None] - cum[:, :, None]
    causal = jnp.tril(jnp.ones((Q, Q), dtype=bool))[:, :, None, None]
    decay = jnp.exp(jnp.where(causal, seg, -jnp.inf))
    cb = jnp.einsum('bcign,bcjgn->bcijg', Cc, Bc)
    mix = cb[..., None] * decay * dtc[:, :, None]
    y_diag = jnp.einsum('bcijgr,bcjgrp->bcigrp', mix, xc)
    wx = (jnp.exp(cum[:, :, -1:] - cum) * dtc)[..., None] * xc
    chunk_states = jnp.einsum('bcjgn,bcjgrp->bcgrpn', Bc, wx)
    chunk_decay = jnp.exp(cum[:, :, -1])

    def step(h, inp):
        s, d = inp
        return h * d[..., None, None] + s, h

    hT, h_starts = lax.scan(step, h0.reshape(bsz, G, R, P, N).astype(jnp.float32),
                            (jnp.moveaxis(chunk_states, 1, 0), jnp.moveaxis(chunk_decay, 1, 0)))
    h_starts = jnp.moveaxis(h_starts, 0, 1)
    y_off = jnp.einsum('bcign,bcgrpn->bcigrp', Cc, h_starts) * jnp.exp(cum)[..., None]
    y = (y_diag + y_off).reshape(bsz, nc * Q, N_HEADS, P)[:, :L]
    return y, hT.reshape(bsz, N_HEADS, P, N)


def ssd_mixer(z, xbc, dt_raw, conv_buf, h0, conv_w, conv_b, dt_bias, A_log, D_skip, norm_w):
    bsz, L, _ = xbc.shape
    xpad = jnp.concatenate([conv_buf.astype(xbc.dtype), xbc], axis=1)
    acc = conv_b + xpad[:, 0:L] * conv_w[0]
    for k in range(1, CONV_W):
        acc = acc + xpad[:, k:k + L] * conv_w[k]
    xbc_act = jax.nn.silu(acc)
    xs, Bm, Cm = jnp.split(xbc_act, [D_SSM, D_SSM + N_GROUPS * D_STATE], axis=-1)
    xs = xs.reshape(bsz, L, N_HEADS, HEAD_DIM)
    dt = jax.nn.softplus(dt_raw.astype(jnp.float32) + dt_bias.astype(jnp.float32))
    A = -jnp.exp(A_log.astype(jnp.float32))
    y, hT = ssd_scan(xs, dt, A, Bm.reshape(bsz, L, N_GROUPS, D_STATE),
                     Cm.reshape(bsz, L, N_GROUPS, D_STATE), h0)
    y = y + xs.astype(jnp.float32) * D_skip.astype(jnp.float32)[:, None]
    g = y.reshape(bsz, L, D_SSM) * jax.nn.silu(z.astype(jnp.float32))
    g = g.reshape(bsz, L, N_GROUPS, D_SSM // N_GROUPS)
    g = g * lax.rsqrt(jnp.mean(jnp.square(g), -1, keepdims=True) + RMS_EPS)
    out = (g.reshape(bsz, L, D_SSM) * norm_w).astype(z.dtype)
    new_conv = xpad[:, -(CONV_W - 1):].astype(conv_buf.dtype)
    return out, new_conv, hT.astype(h0.dtype)


def swiglu(h, wg, wu, wd):
    return (jax.nn.silu(h @ wg) * (h @ wu)) @ wd


def moe_swiglu(h, rw, rb, wg, wu, wd):
    logits = (h @ rw).astype(jnp.float32) + rb
    top_v, top_i = lax.top_k(logits, TOP_K)
    gates = jax.nn.softmax(top_v, axis=-1)
    out = jnp.zeros_like(h)
    for e in range(N_EXPERTS):
        ge = jnp.sum(jnp.where(top_i == e, gates, 0.0), axis=-1)
        out = out + ge[..., None].astype(h.dtype) * swiglu(h, wg[e], wu[e], wd[e])
    return out


def trunk(x, start, pool_bufs, conv_bufs, ssm_states, params):
    (w_in, conv_w, conv_b, dt_bias, A_log, D_skip, ssm_norm_w, pool_w, pool_scale, w_out,
     ln1_g, ln1_b, ln2_g, ln2_b, ffn_w_gate, ffn_w_up, ffn_w_down,
     router_w, router_b, moe_w_gate, moe_w_up, moe_w_down) = params
    new_pool, new_conv, new_ssm = [], [], []
    for l in range(DEPTH):
        proj = x @ w_in[l]
        u, z, xbc, dt_raw = jnp.split(proj, [D_POOL, D_POOL + D_SSM, D_POOL + D_SSM + D_XBC], axis=-1)
        pool_out, pb = pool_mixer(u, pool_bufs[l], start, pool_w[l], pool_scale[l])
        ssd_out, cbuf, sh = ssd_mixer(z, xbc, dt_raw, conv_bufs[l], ssm_states[l], conv_w[l], conv_b[l],
                                      dt_bias[l], A_log[l], D_skip[l], ssm_norm_w[l])
        mixed = jnp.concatenate([pool_out, ssd_out], axis=-1) @ w_out[l]
        x = layer_norm(ALPHA * x + mixed, ln1_g[l], ln1_b[l])
        j = l // 2
        if l % 2 == 0:
            f = swiglu(x, ffn_w_gate[j], ffn_w_up[j], ffn_w_down[j])
        else:
            f = moe_swiglu(x, router_w[j], router_b[j], moe_w_gate[j], moe_w_up[j], moe_w_down[j])
        x = layer_norm(ALPHA * x + f, ln2_g[l], ln2_b[l])
        new_pool.append(pb)
        new_conv.append(cbuf)
        new_ssm.append(sh)
    return x, jnp.stack(new_pool), jnp.stack(new_conv), jnp.stack(new_ssm)


def setup_inputs(seed: int = 0) -> dict:
    key = jax.random.key(seed)
    ks = jax.random.split(key, 32)
    f32 = jnp.float32

    def nrm(k, shape, scale):
        return jax.random.normal(k, shape, f32) * scale

    dt0 = jnp.exp(jax.random.uniform(ks[8], (DEPTH, N_HEADS), f32, math.log(1e-3), math.log(1e-1)))
    return {
        "x_prompt": nrm(ks[0], (BATCH, SEQ, D_MODEL), 1.0),
        "x_sample": nrm(ks[1], (DEC_BATCH, DEC_SEQ, D_MODEL), 1.0),
        "state_pool": nrm(ks[2], (DEPTH, DEC_BATCH, POOL_BUF, D_POOL), 1.0),
        "state_conv": nrm(ks[3], (DEPTH, DEC_BATCH, CONV_W - 1, D_XBC), 1.0),
        "state_ssm": nrm(ks[4], (DEPTH, DEC_BATCH, N_HEADS, HEAD_DIM, D_STATE), 0.5),
        "w_in": nrm(ks[5], (DEPTH, D_MODEL, D_IN_PROJ), D_MODEL ** -0.5),
        "conv_w": nrm(ks[6], (DEPTH, CONV_W, D_XBC), CONV_W ** -0.5),
        "conv_b": nrm(ks[7], (DEPTH, D_XBC), 0.01),
        "dt_bias": dt0 + jnp.log(-jnp.expm1(-dt0)),
        "A_log": jnp.log(jax.random.uniform(ks[9], (DEPTH, N_HEADS), f32, 1.0, 16.0)),
        "D_skip": 1.0 + nrm(ks[10], (DEPTH, N_HEADS), 0.1),
        "ssm_norm_w": 1.0 + nrm(ks[11], (DEPTH, D_SSM), 0.02),
        "pool_w": nrm(ks[12], (DEPTH, POOL_GROUPS, POOL_GROUP, POOL_GROUP), POOL_GROUP ** -0.5),
        "pool_scale": 1.0 + nrm(ks[13], (DEPTH, D_POOL), 0.02),
        "w_out": nrm(ks[14], (DEPTH, D_MIX, D_MODEL), BETA * D_MIX ** -0.5),
        "ln1_g": 1.0 + nrm(ks[15], (DEPTH, D_MODEL), 0.02),
        "ln1_b": nrm(ks[16], (DEPTH, D_MODEL), 0.01),
        "ln2_g": 1.0 + nrm(ks[17], (DEPTH, D_MODEL), 0.02),
        "ln2_b": nrm(ks[18], (DEPTH, D_MODEL), 0.01),
        "ffn_w_gate": nrm(ks[19], (N_DENSE, D_MODEL, D_FF), D_MODEL ** -0.5),
        "ffn_w_up": nrm(ks[20], (N_DENSE, D_MODEL, D_FF), D_MODEL ** -0.5),
        "ffn_w_down": nrm(ks[21], (N_DENSE, D_FF, D_MODEL), BETA * D_FF ** -0.5),
        "router_w": nrm(ks[22], (N_MOE, D_MODEL, N_EXPERTS), D_MODEL ** -0.5),
        "router_b": nrm(ks[23], (N_MOE, N_EXPERTS), 0.01),
        "moe_w_gate": nrm(ks[24], (N_MOE, N_EXPERTS, D_MODEL, D_FF_EXPERT), D_MODEL ** -0.5),
        "moe_w_up": nrm(ks[25], (N_MOE, N_EXPERTS, D_MODEL, D_FF_EXPERT), D_MODEL ** -0.5),
        "moe_w_down": nrm(ks[26], (N_MOE, N_EXPERTS, D_FF_EXPERT, D_MODEL), BETA * D_FF_EXPERT ** -0.5),
    }


def reference(x_prompt, x_sample, state_pool, state_conv, state_ssm, w_in, conv_w, conv_b, dt_bias,
              A_log, D_skip, ssm_norm_w, pool_w, pool_scale, w_out, ln1_g, ln1_b, ln2_g, ln2_b,
              ffn_w_gate, ffn_w_up, ffn_w_down, router_w, router_b, moe_w_gate, moe_w_up, moe_w_down):
    params = (w_in, conv_w, conv_b, dt_bias, A_log, D_skip, ssm_norm_w, pool_w, pool_scale, w_out,
              ln1_g, ln1_b, ln2_g, ln2_b, ffn_w_gate, ffn_w_up, ffn_w_down,
              router_w, router_b, moe_w_gate, moe_w_up, moe_w_down)
    bp = x_prompt.shape[0]
    dtype = x_prompt.dtype
    pool0 = jnp.zeros((DEPTH, bp, POOL_BUF, D_POOL), dtype)
    conv0 = jnp.zeros((DEPTH, bp, CONV_W - 1, D_XBC), dtype)
    ssm0 = jnp.zeros((DEPTH, bp, N_HEADS, HEAD_DIM, D_STATE), dtype)
    y_prompt, pool_p, conv_p, ssm_p = trunk(x_prompt, 0, pool0, conv0, ssm0, params)
    y_sample, pool_s, conv_s, ssm_s = trunk(x_sample, PAST_LEN, state_pool, state_conv, state_ssm, params)
    return (y_prompt, y_sample, pool_p, conv_p, ssm_p, pool_s, conv_s, ssm_s)
```

```python
import functools

import jax
import jax.numpy as jnp
from jax import lax
from jax.experimental import pallas as pl
from jax.experimental.pallas import tpu as pltpu

F32 = jnp.float32
BF16 = jnp.bfloat16

PAST_LEN = 16384
POOL_WINDOWS = (2, 4, 8, 16)
POOL_GROUP = 128
POOL_BUF = max(POOL_WINDOWS) - 1
HEAD_DIM = 64
N_GROUPS = 4
D_STATE = 128
CONV_W = 4
CHUNK = 128
TOP_K = 2
LN_EPS = 1e-5
RMS_EPS = 1e-6

LANES = 128
SUBLANES = 8
V7X_VMEM_BYTES = 64 * 1024 * 1024
VMEM_CAP = V7X_VMEM_BYTES - 8 * 1024 * 1024

SAMPLE_ROWS = 16
SAMPLE_EXT_ROWS = 24
CONV_LEAD = 8


def _vmem_limit(nbytes):
    return int(min(VMEM_CAP, nbytes + 6 * 1024 * 1024))


def _dot(a, b):
    return jnp.dot(a, b, preferred_element_type=F32)


def _dot_nt(a, b):
    return lax.dot_general(a, b, (((1,), (1,)), ((), ())), preferred_element_type=F32)


def _split3(v):
    hi = v.astype(BF16)
    r = v - hi.astype(F32)
    mid = r.astype(BF16)
    lo = (r - mid.astype(F32)).astype(BF16)
    return hi, mid, lo


def _sel_right(parts, m):
    return _dot(parts[0], m) + _dot(parts[1], m) + _dot(parts[2], m)


def _sel_left(m, parts):
    return _dot(m, parts[0]) + _dot(m, parts[1]) + _dot(m, parts[2])


def _silu(x):
    return x / (1.0 + jnp.exp(-x))


def _layer_norm(h, g, b):
    mu = jnp.mean(h, axis=-1, keepdims=True)
    d = h - mu
    var = jnp.mean(d * d, axis=-1, keepdims=True)
    return d * lax.rsqrt(var + LN_EPS) * g + b


def _pick_tile(n, candidates):
    for c in candidates:
        if n % c == 0:
            return c
    raise ValueError(f"no tile in {candidates} divides {n}")


def _const_spec(shape):
    nd = len(shape)
    return pl.BlockSpec(shape, lambda *_: (0,) * nd)


def _inproj_kernel(x_ref, w_ref, u_ref, z_ref, xbc_ref, dt_ref, *, splits):
    xb = x_ref[...].astype(BF16)
    for ref, (lo, hi) in zip((u_ref, z_ref, xbc_ref, dt_ref), splits):
        ref[...] = _dot(xb, w_ref[:, lo:hi])


def _in_proj(x, w_pad, d_pool, d_ssm, d_xbc):
    t, d = x.shape
    n = w_pad.shape[1]
    tm = _pick_tile(t, (512, 256, 128, 64))
    splits = ((0, d_pool), (d_pool, d_pool + d_ssm), (d_pool + d_ssm, d_pool + d_ssm + d_xbc),
              (d_pool + d_ssm + d_xbc, n))
    widths = [hi - lo for lo, hi in splits]
    vmem = 2 * tm * d * 4 + 2 * d * n * 2 + 2 * tm * n * 4 + tm * n * 4
    return pl.pallas_call(
        functools.partial(_inproj_kernel, splits=splits),
        grid=(t // tm,),
        in_specs=[pl.BlockSpec((tm, d), lambda i: (i, 0)), _const_spec((d, n))],
        out_specs=[pl.BlockSpec((tm, w), lambda i: (i, 0)) for w in widths],
        out_shape=[jax.ShapeDtypeStruct((t, w), F32) for w in widths],
        compiler_params=pltpu.CompilerParams(dimension_semantics=("arbitrary",),
                                             vmem_limit_bytes=_vmem_limit(vmem)),
        name="in_proj",
    )(x, w_pad)


def _pool_prompt_kernel(u_ref, w_ref, scale_ref, o_ref, ext_ref, *, tl):
    j = pl.program_id(1)

    @pl.when(j == 0)
    def _():
        ext_ref[0:16, :] = jnp.zeros((16, ext_ref.shape[1]), F32)

    @pl.when(j > 0)
    def _():
        ext_ref[0:16, :] = ext_ref[tl:tl + 16, :]

    ext_ref[16:16 + tl, :] = u_ref[...]
    pos = j * tl + lax.broadcasted_iota(jnp.int32, (tl, POOL_GROUP), 0)
    for g, win in enumerate(POOL_WINDOWS):
        lanes = slice(g * POOL_GROUP, (g + 1) * POOL_GROUP)
        cur = ext_ref[16:16 + tl, lanes]
        acc = cur
        for k in range(1, win):
            acc = acc + ext_ref[16 - k:16 - k + tl, lanes]
        cnt = jnp.minimum(pos + 1, win).astype(F32)
        diff = (acc / cnt - cur).astype(BF16)
        out = _dot(diff, w_ref[g]) * scale_ref[:, lanes]
        o_ref[:, lanes] = out.astype(o_ref.dtype)


def _pool_prompt(u, n_rows, seq_len, pool_w, pool_scale):
    d_pool = u.shape[1]
    tl = _pick_tile(seq_len, (512, 256, 128))
    nj = seq_len // tl
    nb = n_rows // seq_len
    return pl.pallas_call(
        functools.partial(_pool_prompt_kernel, tl=tl),
        grid=(nb, nj),
        in_specs=[pl.BlockSpec((tl, d_pool), lambda b, j: (b * nj + j, 0)),
                  _const_spec(pool_w.shape), _const_spec(pool_scale.shape)],
        out_specs=pl.BlockSpec((tl, d_pool), lambda b, j: (b * nj + j, 0)),
        out_shape=jax.ShapeDtypeStruct((n_rows, d_pool), BF16),
        scratch_shapes=[pltpu.VMEM((tl + 16, d_pool), F32)],
        compiler_params=pltpu.CompilerParams(dimension_semantics=("arbitrary", "arbitrary")),
        name="pool_prompt",
    )(u, pool_w, pool_scale)


def _pool_sample_kernel(ext_ref, w_ref, scale_ref, o_ref, *, n_new, start):
    for t in range(n_new):
        for g, win in enumerate(POOL_WINDOWS):
            lanes = slice(g * POOL_GROUP, (g + 1) * POOL_GROUP)
            cur = ext_ref[POOL_BUF + t, :, lanes]
            acc = cur
            for k in range(1, win):
                acc = acc + ext_ref[POOL_BUF + t - k, :, lanes]
            cnt = float(min(start + t + 1, win))
            diff = (acc / cnt - cur).astype(BF16)
            out = _dot(diff, w_ref[g]) * scale_ref[:, lanes]
            o_ref[t, :, lanes] = out.astype(o_ref.dtype)


def _pool_sample(ext_t, pool_w, pool_scale, n_new, start):
    rows, bs, d_pool = ext_t.shape
    return pl.pallas_call(
        functools.partial(_pool_sample_kernel, n_new=n_new, start=start),
        grid=(1,),
        in_specs=[_const_spec(ext_t.shape), _const_spec(pool_w.shape), _const_spec(pool_scale.shape)],
        out_specs=_const_spec((n_new, bs, d_pool)),
        out_shape=jax.ShapeDtypeStruct((n_new, bs, d_pool), BF16),
        compiler_params=pltpu.CompilerParams(dimension_semantics=("arbitrary",)),
        name="pool_sample",
    )(ext_t, pool_w, pool_scale)


def _conv_silu(window, w_ref, b_ref):
    acc = b_ref[...] + window(0) * w_ref[0:1, :]
    for k in range(1, CONV_W):
        acc = acc + window(k) * w_ref[k:k + 1, :]
    return _silu(acc)


def _softplus(x):
    return jnp.maximum(x, 0.0) + jnp.log1p(jnp.exp(-jnp.abs(x)))


def _ssd_chunk(xact, z, dt, prm, seq_rows, read_state, write_state):
    (tri_ref, tris_ref, sel64_ref, sel128_ref, alog_ref, dexp_ref, normw_ref) = prm
    q = xact.shape[0]
    d_ssm = z.shape[1]
    n_heads = d_ssm // HEAD_DIM
    gw = d_ssm // N_GROUPS
    n_seq = q // seq_rows
    xs = xact[:, :d_ssm]
    bm = xact[:, d_ssm:d_ssm + N_GROUPS * D_STATE]
    cm = xact[:, d_ssm + N_GROUPS * D_STATE:]

    a = dt * (-jnp.exp(alog_ref[...]))
    a3 = _split3(a)
    cum = _sel_left(tri_ref[...], a3)
    rcum = _sel_left(tris_ref[...], a3)
    w = jnp.exp(rcum) * dt
    cum3 = _split3(cum)
    cum_t = cum.T
    cum_e128 = _sel_right(cum3, sel128_ref[...])
    cum_e64 = _sel_right(cum3, sel64_ref[...])
    dt_e = _sel_right(_split3(dt), sel64_ref[...])
    w_e = _sel_right(_split3(w), sel64_ref[...])
    xdt = xs * dt_e
    wx = xs * w_e

    ii = lax.broadcasted_iota(jnp.int32, (q, q), 0)
    jj = lax.broadcasted_iota(jnp.int32, (q, q), 1)
    mask = ii >= jj
    if n_seq > 1:
        mask = jnp.logical_and(mask, (ii // seq_rows) == (jj // seq_rows))
    lo_half = lax.broadcasted_iota(jnp.int32, (q, LANES), 1) < HEAD_DIM
    col = lax.broadcasted_iota(jnp.int32, (gw, q), 1)

    ydiag, yoff = [], []
    for g in range(N_GROUPS):
        bg = bm[:, g * D_STATE:(g + 1) * D_STATE].astype(BF16)
        cg = cm[:, g * D_STATE:(g + 1) * D_STATE].astype(BF16)
        cb = _dot_nt(cg, bg)
        for pr in range(gw // LANES):
            blk = g * (gw // LANES) + pr
            xp = xdt[:, blk * LANES:(blk + 1) * LANES].astype(BF16)
            ys = []
            for half in range(2):
                h = 2 * blk + half
                seg = cum_e128[:, h * LANES:(h + 1) * LANES] - cum_t[h:h + 1, :]
                dec = jnp.exp(jnp.where(mask, seg, -jnp.inf))
                ys.append(_dot((cb * dec).astype(BF16), xp))
            ydiag.append(jnp.where(lo_half, ys[0], ys[1]))
        wx_t = wx[:, g * gw:(g + 1) * gw].T
        yoff_rows = []
        for s in range(n_seq):
            r0 = s * seq_rows
            st = read_state(s, g)
            yoff_rows.append(_dot_nt(cg[r0:r0 + seq_rows, :], st.astype(BF16)))
            last = r0 + seq_rows - 1
            scale = jnp.concatenate(
                [jnp.broadcast_to(jnp.exp(cum_e128[last:last + 1, h * LANES:(h + 1) * LANES]), (HEAD_DIM, D_STATE))
                 for h in range(g * (gw // HEAD_DIM), (g + 1) * (gw // HEAD_DIM))], axis=0)
            wsel = wx_t
            if n_seq > 1:
                wsel = jnp.where(jnp.logical_and(col >= r0, col < r0 + seq_rows), wx_t, 0.0)
            write_state(s, g, st * scale + _dot(wsel.astype(BF16), bg))
        yoff.append(yoff_rows[0] if n_seq == 1 else jnp.concatenate(yoff_rows, axis=0))
    y = (jnp.concatenate(ydiag, axis=1) + jnp.concatenate(yoff, axis=1) * jnp.exp(cum_e64)
         + xs * dexp_ref[...])
    gz = y * _silu(z)
    outs = []
    for g in range(N_GROUPS):
        gg = gz[:, g * gw:(g + 1) * gw]
        ms = jnp.sum(gg * gg, axis=-1, keepdims=True) * (1.0 / gw)
        outs.append(gg * lax.rsqrt(ms + RMS_EPS) * normw_ref[:, g * gw:(g + 1) * gw])
    return jnp.concatenate(outs, axis=1)


def _ssd_prompt_kernel(z_ref, xbc_ref, dtr_ref, convw_ref, convb_ref, dtb_ref,
                       tri_ref, tris_ref, sel64_ref, sel128_ref, alog_ref, dexp_ref, normw_ref,
                       y_ref, hout_ref, ext_ref, h_ref, *, gw):
    c = pl.program_id(1)
    nc = pl.num_programs(1)
    q = z_ref.shape[0]

    @pl.when(c == 0)
    def _():
        ext_ref[0:CONV_LEAD, :] = jnp.zeros((CONV_LEAD, ext_ref.shape[1]), F32)
        h_ref[...] = jnp.zeros(h_ref.shape, F32)

    @pl.when(c > 0)
    def _():
        ext_ref[0:CONV_LEAD, :] = ext_ref[q:q + CONV_LEAD, :]

    ext_ref[CONV_LEAD:CONV_LEAD + q, :] = xbc_ref[...]
    first = CONV_LEAD - (CONV_W - 1)
    xact = _conv_silu(lambda k: ext_ref[first + k:first + k + q, :], convw_ref, convb_ref)
    dt = _softplus(dtr_ref[...] + dtb_ref[...])

    def read_state(s, g):
        return h_ref[g * gw:(g + 1) * gw, :]

    def write_state(s, g, v):
        h_ref[g * gw:(g + 1) * gw, :] = v

    prm = (tri_ref, tris_ref, sel64_ref, sel128_ref, alog_ref, dexp_ref, normw_ref)
    y_ref[...] = _ssd_chunk(xact, z_ref[...], dt, prm, q, read_state, write_state).astype(y_ref.dtype)

    @pl.when(c == nc - 1)
    def _():
        hout_ref[0] = h_ref[...]


def _ssd_sample_kernel(z_ref, ext_ref, dtr_ref, hin_ref, convw_ref, convb_ref, dtb_ref,
                       tri_ref, tris_ref, sel64_ref, sel128_ref, alog_ref, dexp_ref, normw_ref,
                       y_ref, hout_ref, *, gw, n_new):
    q = z_ref.shape[0]
    n_seq = q // SAMPLE_ROWS
    first = CONV_LEAD - (CONV_W - 1)

    def window(k):
        return jnp.concatenate(
            [ext_ref[s * SAMPLE_EXT_ROWS + first + k:s * SAMPLE_EXT_ROWS + first + k + SAMPLE_ROWS, :]
             for s in range(n_seq)], axis=0)

    xact = _conv_silu(window, convw_ref, convb_ref)
    row = lax.broadcasted_iota(jnp.int32, (q, LANES), 0)
    dt = jnp.where((row % SAMPLE_ROWS) < n_new, _softplus(dtr_ref[...] + dtb_ref[...]), 0.0)

    def read_state(s, g):
        return hin_ref[s, g * gw:(g + 1) * gw, :]

    def write_state(s, g, v):
        hout_ref[s, g * gw:(g + 1) * gw, :] = v

    prm = (tri_ref, tris_ref, sel64_ref, sel128_ref, alog_ref, dexp_ref, normw_ref)
    y_ref[...] = _ssd_chunk(xact, z_ref[...], dt, prm, SAMPLE_ROWS, read_state, write_state).astype(y_ref.dtype)


def _ssd_consts(seq_rows, n_heads):
    q = CHUNK
    i = jnp.arange(q)[:, None]
    j = jnp.arange(q)[None, :]
    same = (i // seq_rows) == (j // seq_rows)
    tri = jnp.logical_and(same, j <= i).astype(BF16)
    tris = jnp.logical_and(same, j > i).astype(BF16)
    hrow = jnp.arange(LANES)[:, None]
    sel64 = (hrow == (jnp.arange(n_heads * HEAD_DIM)[None, :] // HEAD_DIM)).astype(BF16)
    sel128 = (hrow == (jnp.arange(n_heads * LANES)[None, :] // LANES)).astype(BF16)
    return tri, tris, sel64, sel128


def _ssd_param_specs(prm_arrays):
    return [_const_spec(a.shape) for a in prm_arrays]


def _ssd_prompt(z, xbc, dtr, n_seq, seq_len, lp):
    d_ssm = z.shape[1]
    d_xbc = xbc.shape[1]
    q = CHUNK
    nc = seq_len // q
    gw = d_ssm // N_GROUPS
    prm = (lp["conv_w"], lp["conv_b"], lp["dt_bias"], *lp["ssd_consts_prompt"], lp["a_log"], lp["d_exp"],
           lp["norm_w"])
    rows = lambda b, c: (b * nc + c, 0)
    return pl.pallas_call(
        functools.partial(_ssd_prompt_kernel, gw=gw),
        grid=(n_seq, nc),
        in_specs=[pl.BlockSpec((q, d_ssm), rows), pl.BlockSpec((q, d_xbc), rows),
                  pl.BlockSpec((q, LANES), rows)] + _ssd_param_specs(prm),
        out_specs=[pl.BlockSpec((q, d_ssm), rows),
                   pl.BlockSpec((1, d_ssm, D_STATE), lambda b, c: (b, 0, 0))],
        out_shape=[jax.ShapeDtypeStruct((n_seq * seq_len, d_ssm), BF16),
                   jax.ShapeDtypeStruct((n_seq, d_ssm, D_STATE), F32)],
        scratch_shapes=[pltpu.VMEM((q + CONV_LEAD, d_xbc), F32), pltpu.VMEM((d_ssm, D_STATE), F32)],
        compiler_params=pltpu.CompilerParams(dimension_semantics=("arbitrary", "arbitrary"),
                                             vmem_limit_bytes=_vmem_limit(40 * 1024 * 1024)),
        name="ssd_prompt",
    )(z, xbc, dtr, *prm)


def _ssd_sample(z16, ext, dtr16, h_in, lp, n_new):
    d_ssm = z16.shape[1]
    d_xbc = ext.shape[1]
    q = CHUNK
    spc = q // SAMPLE_ROWS
    bs = h_in.shape[0]
    gw = d_ssm // N_GROUPS
    prm = (lp["conv_w"], lp["conv_b"], lp["dt_bias"], *lp["ssd_consts_sample"], lp["a_log"], lp["d_exp"],
           lp["norm_w"])
    rows = lambda i: (i, 0)
    state_bytes = spc * d_ssm * D_STATE * 4
    return pl.pallas_call(
        functools.partial(_ssd_sample_kernel, gw=gw, n_new=n_new),
        grid=(bs // spc,),
        in_specs=[pl.BlockSpec((q, d_ssm), rows), pl.BlockSpec((spc * SAMPLE_EXT_ROWS, d_xbc), rows),
                  pl.BlockSpec((q, LANES), rows),
                  pl.BlockSpec((spc, d_ssm, D_STATE), lambda i: (i, 0, 0))] + _ssd_param_specs(prm),
        out_specs=[pl.BlockSpec((q, d_ssm), rows),
                   pl.BlockSpec((spc, d_ssm, D_STATE), lambda i: (i, 0, 0))],
        out_shape=[jax.ShapeDtypeStruct((bs * SAMPLE_ROWS, d_ssm), BF16),
                   jax.ShapeDtypeStruct((bs, d_ssm, D_STATE), F32)],
        compiler_params=pltpu.CompilerParams(dimension_semantics=("arbitrary",),
                                             vmem_limit_bytes=_vmem_limit(4 * state_bytes + 20 * 1024 * 1024)),
        name="ssd_sample",
    )(z16, ext, dtr16, h_in, *prm)


def _outproj_ln_kernel(x_ref, p_ref, s_ref, wp_ref, ws_ref, g_ref, b_ref, o_ref, *, alpha):
    mixed = _dot(p_ref[...], wp_ref[...]) + _dot(s_ref[...], ws_ref[...])
    o_ref[...] = _layer_norm(alpha * x_ref[...] + mixed, g_ref[...], b_ref[...])


def _outproj_ln(x, pool_out, ssd_out, w_pool, w_ssd, g, b, alpha):
    t, d = x.shape
    tm = _pick_tile(t, (512, 256, 128, 64))
    row = lambda w: pl.BlockSpec((tm, w), lambda i: (i, 0))
    return pl.pallas_call(
        functools.partial(_outproj_ln_kernel, alpha=alpha),
        grid=(t // tm,),
        in_specs=[row(d), row(pool_out.shape[1]), row(ssd_out.shape[1]), _const_spec(w_pool.shape),
                  _const_spec(w_ssd.shape), _const_spec(g.shape), _const_spec(b.shape)],
        out_specs=row(d),
        out_shape=jax.ShapeDtypeStruct((t, d), F32),
        compiler_params=pltpu.CompilerParams(dimension_semantics=("arbitrary",),
                                             vmem_limit_bytes=_vmem_limit(32 * 1024 * 1024)),
        name="outproj_ln",
    )(x, pool_out, ssd_out, w_pool, w_ssd, g, b)


def _swiglu(xb, wg_ref, wu_ref, wd_ref):
    act = (_silu(_dot(xb, wg_ref[0])) * _dot(xb, wu_ref[0])).astype(BF16)
    return _dot(act, wd_ref[0])


def _ffn_ln_kernel(te_ref, na_ref, x_ref, wg_ref, wu_ref, wd_ref, g_ref, b_ref, o_ref, *, alpha):
    x = x_ref[...]
    f = _swiglu(x.astype(BF16), wg_ref, wu_ref, wd_ref)
    o_ref[...] = _layer_norm(alpha * x + f, g_ref[...], b_ref[...])


def _moe_ffn_kernel(te_ref, na_ref, x_ref, gate_ref, wg_ref, wu_ref, wd_ref, o_ref):
    active = pl.program_id(0) < na_ref[0]

    @pl.when(active)
    def _():
        f = _swiglu(x_ref[...].astype(BF16), wg_ref, wu_ref, wd_ref)
        o_ref[...] = gate_ref[...] * f

    @pl.when(jnp.logical_not(active))
    def _():
        o_ref[...] = jnp.zeros(o_ref.shape, o_ref.dtype)


def _ffn_call(kernel_fn, x, extra_rows, wg, wu, wd, extra_consts, tile_expert, n_active, tm):
    rows, d = x.shape
    ff = wg.shape[2]
    n_tiles = rows // tm
    row_idx = lambda i, te, na: (jnp.minimum(i, na[0] - 1), 0)
    w_idx = lambda i, te, na: (te[i], 0, 0)
    in_specs = [pl.BlockSpec((tm, d), row_idx)]
    in_specs += [pl.BlockSpec((tm, a.shape[1]), row_idx) for a in extra_rows]
    in_specs += [pl.BlockSpec((1, d, ff), w_idx), pl.BlockSpec((1, d, ff), w_idx), pl.BlockSpec((1, ff, d), w_idx)]
    in_specs += [pl.BlockSpec(a.shape, lambda i, te, na: (0, 0)) for a in extra_consts]
    vmem = 2 * 3 * d * ff * 2 + 4 * tm * d * 4 + tm * ff * 12
    return pl.pallas_call(
        kernel_fn,
        grid_spec=pltpu.PrefetchScalarGridSpec(
            num_scalar_prefetch=2, grid=(n_tiles,), in_specs=in_specs,
            out_specs=pl.BlockSpec((tm, d), lambda i, te, na: (i, 0))),
        out_shape=jax.ShapeDtypeStruct((rows, d), F32),
        compiler_params=pltpu.CompilerParams(dimension_semantics=("arbitrary",),
                                             vmem_limit_bytes=_vmem_limit(vmem)),
        name="swiglu_ffn",
    )(tile_expert, n_active, x, *extra_rows, wg, wu, wd, *extra_consts)


def _router_kernel(x_ref, w_ref, b_ref, idx_ref, gate_ref, *, n_experts):
    x = x_ref[...]
    w = w_ref[...]
    xh = x.astype(BF16)
    xl = (x - xh.astype(F32)).astype(BF16)
    wh = w.astype(BF16)
    wl = (w - wh.astype(F32)).astype(BF16)
    logits = _dot(xh, wh) + (_dot(xh, wl) + _dot(xl, wh)) + b_ref[...]
    lane = lax.broadcasted_iota(jnp.int32, logits.shape, 1)
    logits = jnp.where(lane < n_experts, logits, -jnp.inf)
    lane_f = lane.astype(F32)
    m1 = jnp.max(logits, axis=-1, keepdims=True)
    i1 = jnp.min(jnp.where(logits == m1, lane_f, float(LANES)), axis=-1, keepdims=True)
    rest = jnp.where(lane_f == i1, -jnp.inf, logits)
    m2 = jnp.max(rest, axis=-1, keepdims=True)
    i2 = jnp.min(jnp.where(rest == m2, lane_f, float(LANES)), axis=-1, keepdims=True)
    e2 = jnp.exp(m2 - m1)
    den = 1.0 + e2
    idx_ref[...] = jnp.where(lane == 0, i1, jnp.where(lane == 1, i2, 0.0)).astype(jnp.int32)
    gate_ref[...] = jnp.where(lane == 0, 1.0 / den, jnp.where(lane == 1, e2 / den, 0.0))


def _router(x, w_pad, b_pad, n_experts):
    t, d = x.shape
    tm = _pick_tile(t, (512, 256, 128, 64))
    return pl.pallas_call(
        functools.partial(_router_kernel, n_experts=n_experts),
        grid=(t // tm,),
        in_specs=[pl.BlockSpec((tm, d), lambda i: (i, 0)), _const_spec(w_pad.shape), _const_spec(b_pad.shape)],
        out_specs=[pl.BlockSpec((tm, LANES), lambda i: (i, 0))] * 2,
        out_shape=[jax.ShapeDtypeStruct((t, LANES), jnp.int32), jax.ShapeDtypeStruct((t, LANES), F32)],
        compiler_params=pltpu.CompilerParams(dimension_semantics=("arbitrary",)),
        name="router",
    )(x, w_pad, b_pad)


def _gather_kernel(idx_ref, src_ref, out_ref, sem, *, rows_per_step):
    base = pl.program_id(0) * rows_per_step

    def issue(r, carry):
        pltpu.make_async_copy(src_ref.at[pl.ds(idx_ref[0, 0, r], 1)], out_ref.at[pl.ds(base + r, 1)], sem).start()
        return carry

    lax.fori_loop(0, rows_per_step, issue, 0)
    pltpu.make_async_copy(src_ref.at[pl.ds(0, rows_per_step)], out_ref.at[pl.ds(base, rows_per_step)], sem).wait()


def _gather_rows(src, idx, rows_per_step):
    n = idx.shape[0]
    steps = n // rows_per_step
    return pl.pallas_call(
        functools.partial(_gather_kernel, rows_per_step=rows_per_step),
        grid=(steps,),
        in_specs=[pl.BlockSpec((1, 1, rows_per_step), lambda i: (i, 0, 0), memory_space=pltpu.SMEM),
                  pl.BlockSpec(memory_space=pl.ANY)],
        out_specs=pl.BlockSpec(memory_space=pl.ANY),
        out_shape=jax.ShapeDtypeStruct((n, src.shape[1]), src.dtype),
        scratch_shapes=[pltpu.SemaphoreType.DMA(())],
        compiler_params=pltpu.CompilerParams(dimension_semantics=("arbitrary",)),
        name="gather_rows",
    )(idx.reshape(steps, 1, rows_per_step), src)


def _combine_ln_kernel(x_ref, ya_ref, yb_ref, g_ref, b_ref, o_ref, *, alpha):
    o_ref[...] = _layer_norm(alpha * x_ref[...] + (ya_ref[...] + yb_ref[...]), g_ref[...], b_ref[...])


def _combine_ln(x, ya, yb, g, b, alpha):
    t, d = x.shape
    tm = _pick_tile(t, (512, 256, 128, 64))
    row = pl.BlockSpec((tm, d), lambda i: (i, 0))
    return pl.pallas_call(
        functools.partial(_combine_ln_kernel, alpha=alpha),
        grid=(t // tm,),
        in_specs=[row, row, row, _const_spec(g.shape), _const_spec(b.shape)],
        out_specs=row,
        out_shape=jax.ShapeDtypeStruct((t, d), F32),
        compiler_params=pltpu.CompilerParams(dimension_semantics=("arbitrary",)),
        name="combine_ln",
    )(x, ya, yb, g, b)


def _moe_layer(x, router_w, router_b, wg, wu, wd, g, b, alpha):
    t, d = x.shape
    n_experts = router_w.shape[1]
    tm = 256
    gather_rows = 512
    w_pad = jnp.zeros((d, LANES), F32).at[:, :n_experts].set(router_w)
    b_pad = jnp.zeros((1, LANES), F32).at[0, :n_experts].set(router_b)
    idx, gate = _router(x, w_pad, b_pad, n_experts)
    experts = idx[:, :TOP_K].reshape(-1)
    gates = gate[:, :TOP_K].reshape(-1)
    onehot = (experts[:, None] == jnp.arange(n_experts)[None, :]).astype(jnp.int32)
    rank = jnp.sum((jnp.cumsum(onehot, axis=0) - onehot) * onehot, axis=1)
    counts = jnp.sum(onehot, axis=0)
    padded = ((counts + tm - 1) // tm) * tm
    ends = jnp.cumsum(padded)
    slot = (ends - padded)[experts] + rank
    n_pairs = TOP_K * t
    n_slots = -(-(n_pairs + n_experts * (tm - 1)) // gather_rows) * gather_rows
    token_of_slot = jnp.zeros((n_slots,), jnp.int32).at[slot].set(jnp.arange(n_pairs, dtype=jnp.int32) // TOP_K)
    gate_of_slot = jnp.zeros((n_slots, 1), F32).at[slot, 0].set(gates)
    n_tiles = n_slots // tm
    n_active = (ends[-1:] // tm).astype(jnp.int32)
    tile_start = jnp.minimum(jnp.arange(n_tiles), n_active[0] - 1) * tm
    tile_expert = jnp.searchsorted(ends, tile_start, side="right").astype(jnp.int32)
    x_sorted = _gather_rows(x, token_of_slot, gather_rows)
    y_sorted = _ffn_call(_moe_ffn_kernel, x_sorted, [gate_of_slot], wg, wu, wd, [], tile_expert, n_active, tm)
    slot2 = slot.reshape(t, TOP_K).astype(jnp.int32)
    tg = _pick_tile(t, (512, 256, 128, 64))
    ya = _gather_rows(y_sorted, slot2[:, 0], tg)
    yb = _gather_rows(y_sorted, slot2[:, 1], tg)
    return _combine_ln(x, ya, yb, g, b, alpha)


def _dense_layer(x, wg, wu, wd, g, b, alpha):
    t, d = x.shape
    tm = _pick_tile(t, (256, 128, 64))
    n_tiles = t // tm
    tile_expert = jnp.zeros((n_tiles,), jnp.int32)
    n_active = jnp.full((1,), n_tiles, jnp.int32)
    return _ffn_call(functools.partial(_ffn_ln_kernel, alpha=alpha), x, [], wg, wu, wd, [g, b],
                     tile_expert, n_active, tm)


def kernel(x_prompt, x_sample, state_pool, state_conv, state_ssm, w_in, conv_w, conv_b, dt_bias, A_log, D_skip,
           ssm_norm_w, pool_w, pool_scale, w_out, ln1_g, ln1_b, ln2_g, ln2_b, ffn_w_gate, ffn_w_up, ffn_w_down,
           router_w, router_b, moe_w_gate, moe_w_up, moe_w_down):
    bp, seq, d = x_prompt.shape
    bs, n_new, _ = x_sample.shape
    depth = w_in.shape[0]
    d_pool = pool_scale.shape[1]
    d_ssm = ssm_norm_w.shape[1]
    d_xbc = conv_w.shape[2]
    n_heads = dt_bias.shape[1]
    tp, ts = bp * seq, bs * n_new
    alpha = (2.0 * depth) ** 0.25
    assert d_pool == POOL_GROUP * len(POOL_WINDOWS) and d_ssm == n_heads * HEAD_DIM
    assert d_xbc == d_ssm + 2 * N_GROUPS * D_STATE and seq % CHUNK == 0 and n_new <= CONV_W
    assert n_heads <= LANES and bs % (CHUNK // SAMPLE_ROWS) == 0

    consts_prompt = _ssd_consts(CHUNK, n_heads)
    consts_sample = _ssd_consts(SAMPLE_ROWS, n_heads)
    x = jnp.concatenate([x_prompt.reshape(tp, d), x_sample.reshape(ts, d)], axis=0)
    new_pool_p, new_conv_p, new_ssm_p, new_pool_s, new_conv_s, new_ssm_s = [], [], [], [], [], []
    for l in range(depth):
        n_in = w_in.shape[2]
        w_in_pad = jnp.zeros((d, d_pool + d_ssm + d_xbc + LANES), BF16).at[:, :n_in].set(w_in[l].astype(BF16))
        lp = dict(
            conv_w=conv_w[l], conv_b=conv_b[l][None, :],
            dt_bias=jnp.zeros((1, LANES), F32).at[0, :n_heads].set(dt_bias[l]),
            a_log=jnp.zeros((1, LANES), F32).at[0, :n_heads].set(A_log[l]),
            d_exp=jnp.repeat(D_skip[l], HEAD_DIM)[None, :], norm_w=ssm_norm_w[l][None, :],
            ssd_consts_prompt=consts_prompt, ssd_consts_sample=consts_sample)
        pw = pool_w[l].astype(BF16)
        ps = pool_scale[l][None, :]

        u, z, xbc, dtr = _in_proj(x, w_in_pad, d_pool, d_ssm, d_xbc)

        pool_p = _pool_prompt(u, tp, seq, pw, ps)
        y_p, h_p = _ssd_prompt(z, xbc, dtr, bp, seq, lp)
        u_p = u[:tp].reshape(bp, seq, d_pool)
        xbc_p = xbc[:tp].reshape(bp, seq, d_xbc)
        new_pool_p.append(u_p[:, seq - POOL_BUF:])
        new_conv_p.append(xbc_p[:, seq - (CONV_W - 1):])
        new_ssm_p.append(h_p.reshape(bp, n_heads, HEAD_DIM, D_STATE))

        u_s = u[tp:].reshape(bs, n_new, d_pool)
        xbc_s = xbc[tp:].reshape(bs, n_new, d_xbc)
        pool_ext = jnp.concatenate([state_pool[l], u_s], axis=1)
        pool_s = _pool_sample(jnp.swapaxes(pool_ext, 0, 1), pw, ps, n_new, PAST_LEN)
        pool_s = jnp.swapaxes(pool_s, 0, 1).reshape(ts, d_pool)
        conv_ext = jnp.concatenate([state_conv[l], xbc_s], axis=1)
        lead = CONV_LEAD - (CONV_W - 1)
        ext = jnp.pad(conv_ext, ((0, 0), (lead, SAMPLE_EXT_ROWS - lead - conv_ext.shape[1]), (0, 0)))
        pad_rows = lambda a: jnp.pad(a.reshape(bs, n_new, -1), ((0, 0), (0, SAMPLE_ROWS - n_new), (0, 0))
                                     ).reshape(bs * SAMPLE_ROWS, -1)
        y_s16, h_s = _ssd_sample(pad_rows(z[tp:]), ext.reshape(bs * SAMPLE_EXT_ROWS, d_xbc), pad_rows(dtr[tp:]),
                                 state_ssm[l].reshape(bs, d_ssm, D_STATE), lp, n_new)
        y_s = y_s16.reshape(bs, SAMPLE_ROWS, d_ssm)[:, :n_new].reshape(ts, d_ssm)
        new_pool_s.append(pool_ext[:, n_new:])
        new_conv_s.append(conv_ext[:, n_new:])
        new_ssm_s.append(h_s.reshape(bs, n_heads, HEAD_DIM, D_STATE))

        pool_out = jnp.concatenate([pool_p, pool_s], axis=0)
        ssd_out = jnp.concatenate([y_p, y_s], axis=0)
        w_o = w_out[l].astype(BF16)
        x = _outproj_ln(x, pool_out, ssd_out, w_o[:d_pool], w_o[d_pool:], ln1_g[l][None, :], ln1_b[l][None, :], alpha)

        j = l // 2
        g2, b2 = ln2_g[l][None, :], ln2_b[l][None, :]
        if l % 2 == 0:
            x = _dense_layer(x, ffn_w_gate[j:j + 1].astype(BF16), ffn_w_up[j:j + 1].astype(BF16),
                             ffn_w_down[j:j + 1].astype(BF16), g2, b2, alpha)
        else:
            x = _moe_layer(x, router_w[j], router_b[j], moe_w_gate[j].astype(BF16), moe_w_up[j].astype(BF16),
                           moe_w_down[j].astype(BF16), g2, b2, alpha)

    return (x[:tp].reshape(bp, seq, d), x[tp:].reshape(bs, n_new, d),
            jnp.stack(new_pool_p), jnp.stack(new_conv_p), jnp.stack(new_ssm_p),
            jnp.stack(new_pool_s), jnp.stack(new_conv_s), jnp.stack(new_ssm_s))
```

```python
import functools

import jax
import jax.numpy as jnp
from jax import lax
from jax.experimental import pallas as pl
from jax.experimental.pallas import tpu as pltpu

F32 = jnp.float32
BF16 = jnp.bfloat16

PAST_LEN = 16384
POOL_WINDOWS = (2, 4, 8, 16)
POOL_GROUP = 128
POOL_BUF = max(POOL_WINDOWS) - 1
HEAD_DIM = 64
N_GROUPS = 4
D_STATE = 128
CONV_W = 4
CHUNK = 128
TOP_K = 2
LN_EPS = 1e-5
RMS_EPS = 1e-6

LANES = 128
SUBLANES = 8
V7X_VMEM_BYTES = 64 * 1024 * 1024
VMEM_CAP = V7X_VMEM_BYTES - 8 * 1024 * 1024

SAMPLE_ROWS = 16
SAMPLE_EXT_ROWS = 24
CONV_LEAD = 8
MOE_TILE = 256


def _vmem_limit(nbytes):
    return int(min(VMEM_CAP, nbytes + 6 * 1024 * 1024))


def _dot(a, b):
    return jnp.dot(a, b, preferred_element_type=F32)


def _dot_nt(a, b):
    return lax.dot_general(a, b, (((1,), (1,)), ((), ())), preferred_element_type=F32)


def _split3(v):
    hi = v.astype(BF16)
    r = v - hi.astype(F32)
    mid = r.astype(BF16)
    lo = (r - mid.astype(F32)).astype(BF16)
    return hi, mid, lo


def _sel_right(parts, m):
    return _dot(parts[0], m) + _dot(parts[1], m) + _dot(parts[2], m)


def _sel_left(m, parts):
    return _dot(m, parts[0]) + _dot(m, parts[1]) + _dot(m, parts[2])


def _silu(x):
    return x / (1.0 + jnp.exp(-x))


def _layer_norm(h, g, b):
    mu = jnp.mean(h, axis=-1, keepdims=True)
    d = h - mu
    var = jnp.mean(d * d, axis=-1, keepdims=True)
    return d * lax.rsqrt(var + LN_EPS) * g + b


def _pick_tile(n, candidates):
    for c in candidates:
        if n % c == 0:
            return c
    raise ValueError(f"no tile in {candidates} divides {n}")


def _const_spec(shape):
    nd = len(shape)
    return pl.BlockSpec(shape, lambda *_: (0,) * nd)


def _seg_tiles(segs, tm, rows_per_token=1):
    return [a.shape[0] // (tm * rows_per_token) for a in segs]


def _seg_offsets(tiles):
    offs, off = [], 0
    for n in tiles:
        offs.append(off)
        off += n
    return offs


def _seg_specs(segs, tm, rows_per_token=1):
    tiles = _seg_tiles(segs, tm, rows_per_token)
    return [pl.BlockSpec((tm * rows_per_token, a.shape[1]),
                         lambda i, *_, o=o, n=n: (jnp.clip(i - o, 0, n - 1), 0))
            for a, o, n in zip(segs, _seg_offsets(tiles), tiles)]


def _seg_read(refs, offs):
    v = refs[0][...]
    for r, o in zip(refs[1:], offs[1:]):
        v = jnp.where(pl.program_id(0) >= o, r[...], v)
    return v


def _seg_write(refs, offs, tiles, v):
    i = pl.program_id(0)
    for r, o, n in zip(refs, offs, tiles):
        @pl.when(jnp.logical_and(i >= o, i < o + n))
        def _(r=r):
            r[...] = v.astype(r.dtype)


def _common_tile(row_counts, candidates):
    for c in candidates:
        if all(n % c == 0 for n in row_counts):
            return c
    raise ValueError(f"no tile in {candidates} divides all of {row_counts}")


def _inproj_kernel(*refs, n_x, offs, splits):
    x_refs, w_ref, out_refs = refs[:n_x], refs[n_x], refs[n_x + 1:]
    xb = _seg_read(x_refs, offs).astype(BF16)
    for ref, (lo, hi) in zip(out_refs, splits):
        ref[...] = _dot(xb, w_ref[:, lo:hi])


def _in_proj(x_segs, w_pad, d_pool, d_ssm, d_xbc):
    d, n = w_pad.shape
    tm = _common_tile([a.shape[0] for a in x_segs], (512, 256, 128, 64))
    tiles = _seg_tiles(x_segs, tm)
    t = sum(tiles) * tm
    splits = ((0, d_pool), (d_pool, d_pool + d_ssm), (d_pool + d_ssm, d_pool + d_ssm + d_xbc),
              (d_pool + d_ssm + d_xbc, n))
    widths = [hi - lo for lo, hi in splits]
    vmem = 2 * len(x_segs) * tm * d * 4 + 2 * d * n * 2 + 2 * tm * n * 4 + tm * n * 4
    return pl.pallas_call(
        functools.partial(_inproj_kernel, n_x=len(x_segs), offs=_seg_offsets(tiles), splits=splits),
        grid=(t // tm,),
        in_specs=_seg_specs(x_segs, tm) + [_const_spec((d, n))],
        out_specs=[pl.BlockSpec((tm, w), lambda i: (i, 0)) for w in widths],
        out_shape=[jax.ShapeDtypeStruct((t, w), F32) for w in widths],
        compiler_params=pltpu.CompilerParams(dimension_semantics=("arbitrary",),
                                             vmem_limit_bytes=_vmem_limit(vmem)),
        name="in_proj",
    )(*x_segs, w_pad)


def _pool_prompt_kernel(u_ref, w_ref, scale_ref, o_ref, buf_ref, ext_ref, *, tl):
    j = pl.program_id(1)

    @pl.when(j == pl.num_programs(1) - 1)
    def _():
        buf_ref[0] = u_ref[tl - POOL_BUF:tl, :]

    @pl.when(j == 0)
    def _():
        ext_ref[0:16, :] = jnp.zeros((16, ext_ref.shape[1]), F32)

    @pl.when(j > 0)
    def _():
        ext_ref[0:16, :] = ext_ref[tl:tl + 16, :]

    ext_ref[16:16 + tl, :] = u_ref[...]
    pos = j * tl + lax.broadcasted_iota(jnp.int32, (tl, POOL_GROUP), 0)
    for g, win in enumerate(POOL_WINDOWS):
        lanes = slice(g * POOL_GROUP, (g + 1) * POOL_GROUP)
        cur = ext_ref[16:16 + tl, lanes]
        acc = cur
        for k in range(1, win):
            acc = acc + ext_ref[16 - k:16 - k + tl, lanes]
        cnt = jnp.minimum(pos + 1, win).astype(F32)
        diff = (acc / cnt - cur).astype(BF16)
        out = _dot(diff, w_ref[g]) * scale_ref[:, lanes]
        o_ref[:, lanes] = out.astype(o_ref.dtype)


def _pool_prompt(u, n_rows, seq_len, pool_w, pool_scale):
    d_pool = u.shape[1]
    tl = _pick_tile(seq_len, (512, 256, 128))
    nj = seq_len // tl
    nb = n_rows // seq_len
    return pl.pallas_call(
        functools.partial(_pool_prompt_kernel, tl=tl),
        grid=(nb, nj),
        in_specs=[pl.BlockSpec((tl, d_pool), lambda b, j: (b * nj + j, 0)),
                  _const_spec(pool_w.shape), _const_spec(pool_scale.shape)],
        out_specs=[pl.BlockSpec((tl, d_pool), lambda b, j: (b * nj + j, 0)),
                   pl.BlockSpec((1, POOL_BUF, d_pool), lambda b, j: (b, 0, 0))],
        out_shape=[jax.ShapeDtypeStruct((n_rows, d_pool), BF16),
                   jax.ShapeDtypeStruct((nb, POOL_BUF, d_pool), F32)],
        scratch_shapes=[pltpu.VMEM((tl + 16, d_pool), F32)],
        compiler_params=pltpu.CompilerParams(dimension_semantics=("arbitrary", "arbitrary")),
        name="pool_prompt",
    )(u, pool_w, pool_scale)


def _pool_sample_kernel(ext_ref, w_ref, scale_ref, o_ref, *, n_new, start):
    for t in range(n_new):
        for g, win in enumerate(POOL_WINDOWS):
            lanes = slice(g * POOL_GROUP, (g + 1) * POOL_GROUP)
            cur = ext_ref[POOL_BUF + t, :, lanes]
            acc = cur
            for k in range(1, win):
                acc = acc + ext_ref[POOL_BUF + t - k, :, lanes]
            cnt = float(min(start + t + 1, win))
            diff = (acc / cnt - cur).astype(BF16)
            out = _dot(diff, w_ref[g]) * scale_ref[:, lanes]
            o_ref[t, :, lanes] = out.astype(o_ref.dtype)


def _pool_sample(ext_t, pool_w, pool_scale, n_new, start):
    rows, bs, d_pool = ext_t.shape
    return pl.pallas_call(
        functools.partial(_pool_sample_kernel, n_new=n_new, start=start),
        grid=(1,),
        in_specs=[_const_spec(ext_t.shape), _const_spec(pool_w.shape), _const_spec(pool_scale.shape)],
        out_specs=_const_spec((n_new, bs, d_pool)),
        out_shape=jax.ShapeDtypeStruct((n_new, bs, d_pool), BF16),
        compiler_params=pltpu.CompilerParams(dimension_semantics=("arbitrary",)),
        name="pool_sample",
    )(ext_t, pool_w, pool_scale)


def _conv_silu(window, w_ref, b_ref):
    acc = b_ref[...] + window(0) * w_ref[0:1, :]
    for k in range(1, CONV_W):
        acc = acc + window(k) * w_ref[k:k + 1, :]
    return _silu(acc)


def _softplus(x):
    return jnp.maximum(x, 0.0) + jnp.log1p(jnp.exp(-jnp.abs(x)))


def _ssd_chunk(xact, z, dt, prm, seq_rows, read_state, write_state):
    (tri_ref, tris_ref, sel64_ref, sel128_ref, alog_ref, dexp_ref, normw_ref) = prm
    q = xact.shape[0]
    d_ssm = z.shape[1]
    gw = d_ssm // N_GROUPS
    n_seq = q // seq_rows
    xs = xact[:, :d_ssm]
    bm = xact[:, d_ssm:d_ssm + N_GROUPS * D_STATE]
    cm = xact[:, d_ssm + N_GROUPS * D_STATE:]

    a = dt * (-jnp.exp(alog_ref[...]))
    a3 = _split3(a)
    cum = _sel_left(tri_ref[...], a3)
    rcum = _sel_left(tris_ref[...], a3)
    w = jnp.exp(rcum) * dt
    cum3 = _split3(cum)
    cum_t = cum.T
    cum_e128 = _sel_right(cum3, sel128_ref[...])
    cum_e64 = _sel_right(cum3, sel64_ref[...])
    dt_e = _sel_right(_split3(dt), sel64_ref[...])
    w_e = _sel_right(_split3(w), sel64_ref[...])
    xdt = xs * dt_e
    wx = xs * w_e

    ii = lax.broadcasted_iota(jnp.int32, (q, q), 0)
    jj = lax.broadcasted_iota(jnp.int32, (q, q), 1)
    mask = ii >= jj
    if n_seq > 1:
        mask = jnp.logical_and(mask, (ii // seq_rows) == (jj // seq_rows))
    lo_half = lax.broadcasted_iota(jnp.int32, (q, LANES), 1) < HEAD_DIM
    col = lax.broadcasted_iota(jnp.int32, (gw, q), 1)

    ydiag, yoff = [], []
    for g in range(N_GROUPS):
        bg = bm[:, g * D_STATE:(g + 1) * D_STATE].astype(BF16)
        cg = cm[:, g * D_STATE:(g + 1) * D_STATE].astype(BF16)
        cb = _dot_nt(cg, bg)
        for pr in range(gw // LANES):
            blk = g * (gw // LANES) + pr
            xp = xdt[:, blk * LANES:(blk + 1) * LANES].astype(BF16)
            ys = []
            for half in range(2):
                h = 2 * blk + half
                seg = cum_e128[:, h * LANES:(h + 1) * LANES] - cum_t[h:h + 1, :]
                dec = jnp.exp(jnp.where(mask, seg, -jnp.inf))
                ys.append(_dot((cb * dec).astype(BF16), xp))
            ydiag.append(jnp.where(lo_half, ys[0], ys[1]))
        wx_t = wx[:, g * gw:(g + 1) * gw].T
        yoff_rows = []
        for s in range(n_seq):
            r0 = s * seq_rows
            st = read_state(s, g)
            yoff_rows.append(_dot_nt(cg[r0:r0 + seq_rows, :], st.astype(BF16)))
            last = r0 + seq_rows - 1
            scale = jnp.concatenate(
                [jnp.broadcast_to(jnp.exp(cum_e128[last:last + 1, h * LANES:(h + 1) * LANES]), (HEAD_DIM, D_STATE))
                 for h in range(g * (gw // HEAD_DIM), (g + 1) * (gw // HEAD_DIM))], axis=0)
            wsel = wx_t
            if n_seq > 1:
                wsel = jnp.where(jnp.logical_and(col >= r0, col < r0 + seq_rows), wx_t, 0.0)
            write_state(s, g, st * scale + _dot(wsel.astype(BF16), bg))
        yoff.append(yoff_rows[0] if n_seq == 1 else jnp.concatenate(yoff_rows, axis=0))
    y = (jnp.concatenate(ydiag, axis=1) + jnp.concatenate(yoff, axis=1) * jnp.exp(cum_e64)
         + xs * dexp_ref[...])
    gz = y * _silu(z)
    outs = []
    for g in range(N_GROUPS):
        gg = gz[:, g * gw:(g + 1) * gw]
        ms = jnp.sum(gg * gg, axis=-1, keepdims=True) * (1.0 / gw)
        outs.append(gg * lax.rsqrt(ms + RMS_EPS) * normw_ref[:, g * gw:(g + 1) * gw])
    return jnp.concatenate(outs, axis=1)


def _ssd_prompt_kernel(z_ref, xbc_ref, dtr_ref, convw_ref, convb_ref, dtb_ref,
                       tri_ref, tris_ref, sel64_ref, sel128_ref, alog_ref, dexp_ref, normw_ref,
                       y_ref, hout_ref, cout_ref, ext_ref, h_ref, *, gw):
    c = pl.program_id(1)
    nc = pl.num_programs(1)
    q = z_ref.shape[0]

    @pl.when(c == 0)
    def _():
        ext_ref[0:CONV_LEAD, :] = jnp.zeros((CONV_LEAD, ext_ref.shape[1]), F32)
        h_ref[...] = jnp.zeros(h_ref.shape, F32)

    @pl.when(c > 0)
    def _():
        ext_ref[0:CONV_LEAD, :] = ext_ref[q:q + CONV_LEAD, :]

    ext_ref[CONV_LEAD:CONV_LEAD + q, :] = xbc_ref[...]
    first = CONV_LEAD - (CONV_W - 1)
    xact = _conv_silu(lambda k: ext_ref[first + k:first + k + q, :], convw_ref, convb_ref)
    dt = _softplus(dtr_ref[...] + dtb_ref[...])

    def read_state(s, g):
        return h_ref[g * gw:(g + 1) * gw, :]

    def write_state(s, g, v):
        h_ref[g * gw:(g + 1) * gw, :] = v

    prm = (tri_ref, tris_ref, sel64_ref, sel128_ref, alog_ref, dexp_ref, normw_ref)
    y_ref[...] = _ssd_chunk(xact, z_ref[...], dt, prm, q, read_state, write_state).astype(y_ref.dtype)

    @pl.when(c == nc - 1)
    def _():
        hout_ref[0] = h_ref[...]
        cout_ref[0] = xbc_ref[q - (CONV_W - 1):q, :]


def _ssd_sample_kernel(z_ref, ext_ref, dtr_ref, hin_ref, convw_ref, convb_ref, dtb_ref,
                       tri_ref, tris_ref, sel64_ref, sel128_ref, alog_ref, dexp_ref, normw_ref,
                       y_ref, hout_ref, *, gw, n_new):
    q = z_ref.shape[0]
    n_seq = q // SAMPLE_ROWS
    first = CONV_LEAD - (CONV_W - 1)

    def window(k):
        return jnp.concatenate(
            [ext_ref[s * SAMPLE_EXT_ROWS + first + k:s * SAMPLE_EXT_ROWS + first + k + SAMPLE_ROWS, :]
             for s in range(n_seq)], axis=0)

    xact = _conv_silu(window, convw_ref, convb_ref)
    row = lax.broadcasted_iota(jnp.int32, (q, LANES), 0)
    dt = jnp.where((row % SAMPLE_ROWS) < n_new, _softplus(dtr_ref[...] + dtb_ref[...]), 0.0)

    def read_state(s, g):
        return hin_ref[s, g * gw:(g + 1) * gw, :]

    def write_state(s, g, v):
        hout_ref[s, g * gw:(g + 1) * gw, :] = v

    prm = (tri_ref, tris_ref, sel64_ref, sel128_ref, alog_ref, dexp_ref, normw_ref)
    y_ref[...] = _ssd_chunk(xact, z_ref[...], dt, prm, SAMPLE_ROWS, read_state, write_state).astype(y_ref.dtype)


def _ssd_consts(seq_rows, n_heads):
    q = CHUNK
    i = jnp.arange(q)[:, None]
    j = jnp.arange(q)[None, :]
    same = (i // seq_rows) == (j // seq_rows)
    tri = jnp.logical_and(same, j <= i).astype(BF16)
    tris = jnp.logical_and(same, j > i).astype(BF16)
    hrow = jnp.arange(LANES)[:, None]
    sel64 = (hrow == (jnp.arange(n_heads * HEAD_DIM)[None, :] // HEAD_DIM)).astype(BF16)
    sel128 = (hrow == (jnp.arange(n_heads * LANES)[None, :] // LANES)).astype(BF16)
    return tri, tris, sel64, sel128


def _ssd_param_specs(prm_arrays):
    return [_const_spec(a.shape) for a in prm_arrays]


def _ssd_prompt(z, xbc, dtr, n_seq, seq_len, lp):
    d_ssm = z.shape[1]
    d_xbc = xbc.shape[1]
    q = CHUNK
    nc = seq_len // q
    gw = d_ssm // N_GROUPS
    prm = (lp["conv_w"], lp["conv_b"], lp["dt_bias"], *lp["ssd_consts_prompt"], lp["a_log"], lp["d_exp"],
           lp["norm_w"])
    rows = lambda b, c: (b * nc + c, 0)
    return pl.pallas_call(
        functools.partial(_ssd_prompt_kernel, gw=gw),
        grid=(n_seq, nc),
        in_specs=[pl.BlockSpec((q, d_ssm), rows), pl.BlockSpec((q, d_xbc), rows),
                  pl.BlockSpec((q, LANES), rows)] + _ssd_param_specs(prm),
        out_specs=[pl.BlockSpec((q, d_ssm), rows),
                   pl.BlockSpec((1, d_ssm, D_STATE), lambda b, c: (b, 0, 0)),
                   pl.BlockSpec((1, CONV_W - 1, d_xbc), lambda b, c: (b, 0, 0))],
        out_shape=[jax.ShapeDtypeStruct((n_seq * seq_len, d_ssm), BF16),
                   jax.ShapeDtypeStruct((n_seq, d_ssm, D_STATE), F32),
                   jax.ShapeDtypeStruct((n_seq, CONV_W - 1, d_xbc), F32)],
        scratch_shapes=[pltpu.VMEM((q + CONV_LEAD, d_xbc), F32), pltpu.VMEM((d_ssm, D_STATE), F32)],
        compiler_params=pltpu.CompilerParams(dimension_semantics=("arbitrary", "arbitrary"),
                                             vmem_limit_bytes=_vmem_limit(40 * 1024 * 1024)),
        name="ssd_prompt",
    )(z, xbc, dtr, *prm)


def _ssd_sample(z16, ext, dtr16, h_in, lp, n_new):
    d_ssm = z16.shape[1]
    d_xbc = ext.shape[1]
    q = CHUNK
    spc = q // SAMPLE_ROWS
    bs = h_in.shape[0]
    gw = d_ssm // N_GROUPS
    prm = (lp["conv_w"], lp["conv_b"], lp["dt_bias"], *lp["ssd_consts_sample"], lp["a_log"], lp["d_exp"],
           lp["norm_w"])
    rows = lambda i: (i, 0)
    state_bytes = spc * d_ssm * D_STATE * 4
    return pl.pallas_call(
        functools.partial(_ssd_sample_kernel, gw=gw, n_new=n_new),
        grid=(bs // spc,),
        in_specs=[pl.BlockSpec((q, d_ssm), rows), pl.BlockSpec((spc * SAMPLE_EXT_ROWS, d_xbc), rows),
                  pl.BlockSpec((q, LANES), rows),
                  pl.BlockSpec((spc, d_ssm, D_STATE), lambda i: (i, 0, 0))] + _ssd_param_specs(prm),
        out_specs=[pl.BlockSpec((q, d_ssm), rows),
                   pl.BlockSpec((spc, d_ssm, D_STATE), lambda i: (i, 0, 0))],
        out_shape=[jax.ShapeDtypeStruct((bs * SAMPLE_ROWS, d_ssm), BF16),
                   jax.ShapeDtypeStruct((bs, d_ssm, D_STATE), F32)],
        compiler_params=pltpu.CompilerParams(dimension_semantics=("arbitrary",),
                                             vmem_limit_bytes=_vmem_limit(4 * state_bytes + 20 * 1024 * 1024)),
        name="ssd_sample",
    )(z16, ext, dtr16, h_in, *prm)


def _route(h, rw_ref, rb_ref, tril_ref, cnt_ref, n_experts):
    xh = h.astype(BF16)
    xl = (h - xh.astype(F32)).astype(BF16)
    w = rw_ref[...]
    wh = w.astype(BF16)
    wl = (w - wh.astype(F32)).astype(BF16)
    logits = _dot(xh, wh) + (_dot(xh, wl) + _dot(xl, wh)) + rb_ref[...]
    lane = lax.broadcasted_iota(jnp.int32, logits.shape, 1)
    lane_f = lane.astype(F32)
    logits = jnp.where(lane < n_experts, logits, -jnp.inf)
    m1 = jnp.max(logits, axis=-1, keepdims=True)
    i1 = jnp.min(jnp.where(logits == m1, lane_f, float(LANES)), axis=-1, keepdims=True)
    rest = jnp.where(lane_f == i1, -jnp.inf, logits)
    m2 = jnp.max(rest, axis=-1, keepdims=True)
    i2 = jnp.min(jnp.where(rest == m2, lane_f, float(LANES)), axis=-1, keepdims=True)
    e2 = jnp.exp(m2 - m1)
    den = 1.0 + e2
    oh1 = jnp.where(lane_f == i1, 1.0, 0.0)
    oh2 = jnp.where(lane_f == i2, 1.0, 0.0)
    before1 = _dot(tril_ref[...], oh1.astype(BF16))
    before2 = _dot(tril_ref[...], oh2.astype(BF16))
    c1 = jnp.sum(oh1, axis=0, keepdims=True)
    c2 = jnp.sum(oh2, axis=0, keepdims=True)
    base = cnt_ref[...]
    r1 = jnp.sum(oh1 * (before1 + base), axis=-1, keepdims=True)
    r2 = jnp.sum(oh2 * (before2 + (base + c1)), axis=-1, keepdims=True)
    cnt_ref[...] = base + (c1 + c2)
    idx = jnp.where(lane == 0, i1, jnp.where(lane == 1, i2, 0.0)).astype(jnp.int32)
    gate = jnp.where(lane == 0, 1.0 / den, jnp.where(lane == 1, e2 / den, 0.0))
    rank = jnp.where(lane == 0, r1, jnp.where(lane == 1, r2, 0.0)).astype(jnp.int32)
    return idx, gate, rank


def _outproj_ln_kernel(*refs, alpha, n_seg, offs, route, n_experts):
    it = iter(refs)
    take = lambda n: [next(it) for _ in range(n)]
    x_refs, p_refs, s_refs = take(n_seg[0]), take(n_seg[1]), take(n_seg[2])
    wp_ref, ws_ref, g_ref, b_ref = take(4)
    x = _seg_read(x_refs, offs[0])
    mixed = _dot(_seg_read(p_refs, offs[1]), wp_ref[...]) + _dot(_seg_read(s_refs, offs[2]), ws_ref[...])
    h = _layer_norm(alpha * x + mixed, g_ref[...], b_ref[...])
    if not route:
        (o_ref,) = take(1)
        o_ref[...] = h
        return
    rw_ref, rb_ref, tril_ref = take(3)
    xt_ref, idx_ref, gate_ref, rank_ref, cnt_ref = take(5)

    @pl.when(pl.program_id(0) == 0)
    def _():
        cnt_ref[...] = jnp.zeros(cnt_ref.shape, F32)

    tm, d = h.shape
    rpt = d // LANES
    for k in range(rpt):
        xt_ref[pl.ds(k, tm, stride=rpt), :] = h[:, k * LANES:(k + 1) * LANES]
    idx, gate, rank = _route(h, rw_ref, rb_ref, tril_ref, cnt_ref, n_experts)
    idx_ref[...] = idx
    gate_ref[...] = gate
    rank_ref[...] = rank


def _outproj_ln(x_segs, pool_segs, ssd_segs, w_pool, w_ssd, g, b, alpha, router=None):
    d = x_segs[0].shape[1]
    rows = [a.shape[0] for a in x_segs + pool_segs + ssd_segs]
    tm = _common_tile(rows, (512, 256, 128, 64))
    groups = (x_segs, pool_segs, ssd_segs)
    tiles = [_seg_tiles(s, tm) for s in groups]
    t = sum(tiles[0]) * tm
    assert all(sum(ts) * tm == t for ts in tiles)
    in_specs = [sp for s in groups for sp in _seg_specs(s, tm)]
    consts = [w_pool, w_ssd, g, b]
    row = lambda w, dt: (pl.BlockSpec((tm, w), lambda i: (i, 0)), jax.ShapeDtypeStruct((t, w), dt))
    if router is None:
        outs = [row(d, F32)]
        n_experts = 0
    else:
        rw_pad, rb_pad, n_experts = router
        tril = (jnp.arange(tm)[:, None] > jnp.arange(tm)[None, :]).astype(BF16)
        consts += [rw_pad, rb_pad, tril]
        rpt = d // LANES
        outs = [(pl.BlockSpec((tm * rpt, LANES), lambda i: (i, 0)), jax.ShapeDtypeStruct((t * rpt, LANES), F32)),
                row(LANES, jnp.int32), row(LANES, F32), row(LANES, jnp.int32),
                (_const_spec((1, LANES)), jax.ShapeDtypeStruct((1, LANES), F32))]
    in_specs += [_const_spec(c.shape) for c in consts]
    res = pl.pallas_call(
        functools.partial(_outproj_ln_kernel, alpha=alpha, n_seg=[len(s) for s in groups],
                          offs=[_seg_offsets(ts) for ts in tiles], route=router is not None, n_experts=n_experts),
        grid=(t // tm,),
        in_specs=in_specs,
        out_specs=[o[0] for o in outs],
        out_shape=[o[1] for o in outs],
        compiler_params=pltpu.CompilerParams(dimension_semantics=("arbitrary",),
                                             vmem_limit_bytes=_vmem_limit(40 * 1024 * 1024)),
        name="outproj_ln",
    )(*x_segs, *pool_segs, *ssd_segs, *consts)
    return res[0] if router is None else res


def _swiglu(xb, wg_ref, wu_ref, wd_ref):
    act = (_silu(_dot(xb, wg_ref[0])) * _dot(xb, wu_ref[0])).astype(BF16)
    return _dot(act, wd_ref[0])


def _dense_ffn_ln_kernel(x_ref, wg_ref, wu_ref, wd_ref, g_ref, b_ref, *out_refs, alpha, offs, tiles):
    x = x_ref[...]
    f = _swiglu(x.astype(BF16), wg_ref, wu_ref, wd_ref)
    _seg_write(out_refs, offs, tiles, _layer_norm(alpha * x + f, g_ref[...], b_ref[...]))


def _dense_ffn_ln(x, wg, wu, wd, g, b, alpha, out_rows):
    t, d = x.shape
    ff = wg.shape[2]
    tm = _common_tile(out_rows, (256, 128, 64))
    tiles = [n // tm for n in out_rows]
    offs = _seg_offsets(tiles)
    vmem = 2 * 3 * d * ff * 2 + 4 * tm * d * 4 + tm * ff * 12
    res = pl.pallas_call(
        functools.partial(_dense_ffn_ln_kernel, alpha=alpha, offs=offs, tiles=tiles),
        grid=(t // tm,),
        in_specs=[pl.BlockSpec((tm, d), lambda i: (i, 0)), _const_spec(wg.shape), _const_spec(wu.shape),
                  _const_spec(wd.shape), _const_spec(g.shape), _const_spec(b.shape)],
        out_specs=[pl.BlockSpec((tm, d), lambda i, o=o, n=n: (jnp.clip(i - o, 0, n - 1), 0))
                   for o, n in zip(offs, tiles)],
        out_shape=[jax.ShapeDtypeStruct((n, d), F32) for n in out_rows],
        compiler_params=pltpu.CompilerParams(dimension_semantics=("arbitrary",),
                                             vmem_limit_bytes=_vmem_limit(vmem)),
        name="dense_ffn_ln",
    )(x, wg, wu, wd, g, b)
    return list(res)


def _dispatch_kernel(ends_ref, slot_ref, x_hbm, out_hbm, zeros_ref, sem, zsem, *, tokens, rpt, tile, n_experts):
    i = pl.program_id(0)

    def tail_copy(e):
        start = pl.multiple_of((ends_ref[e] - tile) * rpt, tile * rpt)
        return pltpu.make_async_copy(zeros_ref, out_hbm.at[pl.ds(start, tile * rpt)], zsem)

    def nonempty(e):
        return ends_ref[e] > (ends_ref[e - 1] if e > 0 else 0)

    def unused_copy(j):
        start = pl.multiple_of((ends_ref[n_experts - 1] + j * tile) * rpt, tile * rpt)
        return pltpu.make_async_copy(zeros_ref, out_hbm.at[pl.ds(start, tile * rpt)], zsem)

    def unused(j):
        return ends_ref[n_experts - 1] + (j + 1) * tile <= out_hbm.shape[0] // rpt

    @pl.when(i == 0)
    def _():
        zeros_ref[...] = jnp.zeros(zeros_ref.shape, zeros_ref.dtype)
        for e in range(n_experts):
            @pl.when(nonempty(e))
            def _(e=e):
                tail_copy(e).start()

            @pl.when(unused(e))
            def _(e=e):
                unused_copy(e).start()
        for e in range(n_experts):
            @pl.when(nonempty(e))
            def _(e=e):
                tail_copy(e).wait()

            @pl.when(unused(e))
            def _(e=e):
                unused_copy(e).wait()

    base = i * tokens

    def issue(r, carry):
        src = x_hbm.at[pl.ds(pl.multiple_of((base + r) * rpt, rpt), rpt)]
        for k in range(TOP_K):
            dst = out_hbm.at[pl.ds(pl.multiple_of(slot_ref[0, 0, k * tokens + r] * rpt, rpt), rpt)]
            pltpu.make_async_copy(src, dst, sem).start()
        return carry

    lax.fori_loop(0, tokens, issue, 0)
    n = TOP_K * tokens * rpt
    pltpu.make_async_copy(x_hbm.at[pl.ds(0, n)], out_hbm.at[pl.ds(0, n)], sem).wait()


def _dispatch(xt, slots, ends, n_slots, tokens, rpt, n_experts):
    steps = slots.shape[0]
    return pl.pallas_call(
        functools.partial(_dispatch_kernel, tokens=tokens, rpt=rpt, tile=MOE_TILE, n_experts=n_experts),
        grid_spec=pltpu.PrefetchScalarGridSpec(
            num_scalar_prefetch=1, grid=(steps,),
            in_specs=[pl.BlockSpec((1, 1, TOP_K * tokens), lambda i, e: (i, 0, 0), memory_space=pltpu.SMEM),
                      pl.BlockSpec(memory_space=pl.ANY)],
            out_specs=pl.BlockSpec(memory_space=pl.ANY),
            scratch_shapes=[pltpu.VMEM((MOE_TILE * rpt, LANES), xt.dtype),
                            pltpu.SemaphoreType.DMA(()), pltpu.SemaphoreType.DMA(())]),
        out_shape=jax.ShapeDtypeStruct((n_slots * rpt, LANES), xt.dtype),
        compiler_params=pltpu.CompilerParams(dimension_semantics=("arbitrary",)),
        name="moe_dispatch",
    )(ends, slots, xt)


def _moe_ffn_kernel(te_ref, na_ref, x_ref, wg_ref, wu_ref, wd_ref, o_ref, *, tm, rpt):
    active = pl.program_id(0) < na_ref[0]

    @pl.when(active)
    def _():
        xb = jnp.concatenate([x_ref[pl.ds(k, tm, stride=rpt), :] for k in range(rpt)], axis=1).astype(BF16)
        f = _swiglu(xb, wg_ref, wu_ref, wd_ref)
        for k in range(rpt):
            o_ref[pl.ds(k, tm, stride=rpt), :] = f[:, k * LANES:(k + 1) * LANES]

    @pl.when(jnp.logical_not(active))
    def _():
        o_ref[...] = jnp.zeros(o_ref.shape, o_ref.dtype)


def _moe_ffn(xs, wg, wu, wd, tile_expert, n_active, rpt):
    tm = MOE_TILE
    d, ff = wg.shape[1], wg.shape[2]
    n_tiles = xs.shape[0] // (tm * rpt)
    w_idx = lambda i, te, na: (te[i], 0, 0)
    vmem = 2 * 3 * d * ff * 2 + 4 * tm * d * 4 + tm * ff * 12 + 2 * tm * d * 4
    return pl.pallas_call(
        functools.partial(_moe_ffn_kernel, tm=tm, rpt=rpt),
        grid_spec=pltpu.PrefetchScalarGridSpec(
            num_scalar_prefetch=2, grid=(n_tiles,),
            in_specs=[pl.BlockSpec((tm * rpt, LANES), lambda i, te, na: (jnp.minimum(i, na[0] - 1), 0)),
                      pl.BlockSpec((1, d, ff), w_idx), pl.BlockSpec((1, d, ff), w_idx),
                      pl.BlockSpec((1, ff, d), w_idx)],
            out_specs=pl.BlockSpec((tm * rpt, LANES), lambda i, te, na: (i, 0))),
        out_shape=jax.ShapeDtypeStruct(xs.shape, F32),
        compiler_params=pltpu.CompilerParams(dimension_semantics=("arbitrary",),
                                             vmem_limit_bytes=_vmem_limit(vmem)),
        name="moe_ffn",
    )(tile_expert, n_active, xs, wg, wu, wd)


def _combine_ln_kernel(slot_ref, next_ref, x_ref, gate_ref, y_hbm, g_ref, b_ref, *rest, alpha, tm, rpt, offs, tiles):
    out_refs, (buf, sem) = rest[:len(tiles)], rest[len(tiles):]
    i = pl.program_id(0)
    n_rows = TOP_K * tm * rpt

    def issue(s_ref, slot):
        def body(r, carry):
            src = y_hbm.at[pl.ds(pl.multiple_of(s_ref[0, 0, r] * rpt, rpt), rpt)]
            pltpu.make_async_copy(src, buf.at[slot, pl.ds(pl.multiple_of(r * rpt, rpt), rpt)], sem.at[slot]).start()
            return carry
        lax.fori_loop(0, TOP_K * tm, body, 0)

    @pl.when(i == 0)
    def _():
        issue(slot_ref, 0)

    @pl.when(i + 1 < pl.num_programs(0))
    def _():
        issue(next_ref, (i + 1) % 2)

    cur = i % 2
    pltpu.make_async_copy(y_hbm.at[pl.ds(0, n_rows)], buf.at[cur], sem.at[cur]).wait()
    g1 = gate_ref[:, 0:1]
    g2 = gate_ref[:, 1:2]
    cols = []
    for k in range(rpt):
        xk = x_ref[pl.ds(k, tm, stride=rpt), :]
        ya = buf[cur, pl.ds(k, tm, stride=rpt), :]
        yb = buf[cur, pl.ds(tm * rpt + k, tm, stride=rpt), :]
        cols.append(alpha * xk + (g1 * ya + g2 * yb))
    h = jnp.concatenate(cols, axis=1)
    _seg_write(out_refs, offs, tiles, _layer_norm(h, g_ref[...], b_ref[...]))


def _combine_ln(xt, gate, ys, slots, g, b, alpha, tm, rpt, out_rows):
    d = g.shape[1]
    tiles = [n // tm for n in out_rows]
    offs = _seg_offsets(tiles)
    steps = slots.shape[0]
    slot_spec = lambda f: pl.BlockSpec((1, 1, TOP_K * tm), f, memory_space=pltpu.SMEM)
    res = pl.pallas_call(
        functools.partial(_combine_ln_kernel, alpha=alpha, tm=tm, rpt=rpt, offs=offs, tiles=tiles),
        grid=(steps,),
        in_specs=[slot_spec(lambda i: (i, 0, 0)), slot_spec(lambda i: (jnp.minimum(i + 1, steps - 1), 0, 0)),
                  pl.BlockSpec((tm * rpt, LANES), lambda i: (i, 0)), pl.BlockSpec((tm, LANES), lambda i: (i, 0)),
                  pl.BlockSpec(memory_space=pl.ANY), _const_spec(g.shape), _const_spec(b.shape)],
        out_specs=[pl.BlockSpec((tm, d), lambda i, o=o, n=n: (jnp.clip(i - o, 0, n - 1), 0))
                   for o, n in zip(offs, tiles)],
        out_shape=[jax.ShapeDtypeStruct((n, d), F32) for n in out_rows],
        scratch_shapes=[pltpu.VMEM((2, TOP_K * tm * rpt, LANES), F32), pltpu.SemaphoreType.DMA((2,))],
        compiler_params=pltpu.CompilerParams(dimension_semantics=("arbitrary",),
                                             vmem_limit_bytes=_vmem_limit(32 * 1024 * 1024)),
        name="moe_combine_ln",
    )(slots, slots, xt, gate, ys, g, b)
    return list(res)


def _moe_ffn_ln(xt, idx, gate, rank, counts, wg, wu, wd, g, b, alpha, tm, out_rows):
    n_experts = wg.shape[0]
    d = wg.shape[1]
    rpt = d // LANES
    t = idx.shape[0]
    tile = MOE_TILE
    cnt = counts[0, :n_experts].astype(jnp.int32)
    padded = ((cnt + tile - 1) // tile) * tile
    ends = jnp.cumsum(padded).astype(jnp.int32)
    starts = ends - padded
    slot = jnp.take(starts, idx[:, :TOP_K]) + rank[:, :TOP_K]
    n_slots = -(-(TOP_K * t + n_experts * (tile - 1)) // tile) * tile
    n_tiles = n_slots // tile
    n_active = ends[-1:] // tile
    tile_start = jnp.minimum(jnp.arange(n_tiles, dtype=jnp.int32), n_active[0] - 1) * tile
    tile_expert = jnp.sum((tile_start[:, None] >= ends[None, :]).astype(jnp.int32), axis=1)
    slots = jnp.swapaxes(slot.reshape(t // tm, tm, TOP_K), 1, 2).reshape(t // tm, 1, TOP_K * tm)
    xs = _dispatch(xt, slots, ends, n_slots, tm, rpt, n_experts)
    ys = _moe_ffn(xs, wg, wu, wd, tile_expert, n_active, rpt)
    return _combine_ln(xt, gate, ys, slots, g, b, alpha, tm, rpt, out_rows)


def kernel(x_prompt, x_sample, state_pool, state_conv, state_ssm, w_in, conv_w, conv_b, dt_bias, A_log, D_skip,
           ssm_norm_w, pool_w, pool_scale, w_out, ln1_g, ln1_b, ln2_g, ln2_b, ffn_w_gate, ffn_w_up, ffn_w_down,
           router_w, router_b, moe_w_gate, moe_w_up, moe_w_down):
    bp, seq, d = x_prompt.shape
    bs, n_new, _ = x_sample.shape
    depth = w_in.shape[0]
    d_pool = pool_scale.shape[1]
    d_ssm = ssm_norm_w.shape[1]
    d_xbc = conv_w.shape[2]
    n_heads = dt_bias.shape[1]
    tp, ts = bp * seq, bs * n_new
    alpha = (2.0 * depth) ** 0.25
    assert d_pool == POOL_GROUP * len(POOL_WINDOWS) and d_ssm == n_heads * HEAD_DIM
    assert d_xbc == d_ssm + 2 * N_GROUPS * D_STATE and seq % CHUNK == 0 and n_new <= CONV_W
    assert n_heads <= LANES and bs % (CHUNK // SAMPLE_ROWS) == 0 and d % LANES == 0

    consts_prompt = _ssd_consts(CHUNK, n_heads)
    consts_sample = _ssd_consts(SAMPLE_ROWS, n_heads)
    x_segs = [x_prompt.reshape(tp, d), x_sample.reshape(ts, d)]
    new_pool_p, new_conv_p, new_ssm_p, new_pool_s, new_conv_s, new_ssm_s = [], [], [], [], [], []
    for l in range(depth):
        last = l == depth - 1
        out_rows = [tp, ts] if last else [tp + ts]
        n_in = w_in.shape[2]
        w_in_pad = jnp.zeros((d, d_pool + d_ssm + d_xbc + LANES), BF16).at[:, :n_in].set(w_in[l].astype(BF16))
        lp = dict(
            conv_w=conv_w[l], conv_b=conv_b[l][None, :],
            dt_bias=jnp.zeros((1, LANES), F32).at[0, :n_heads].set(dt_bias[l]),
            a_log=jnp.zeros((1, LANES), F32).at[0, :n_heads].set(A_log[l]),
            d_exp=jnp.repeat(D_skip[l], HEAD_DIM)[None, :], norm_w=ssm_norm_w[l][None, :],
            ssd_consts_prompt=consts_prompt, ssd_consts_sample=consts_sample)
        pw = pool_w[l].astype(BF16)
        ps = pool_scale[l][None, :]

        u, z, xbc, dtr = _in_proj(x_segs, w_in_pad, d_pool, d_ssm, d_xbc)

        pool_p, buf_p = _pool_prompt(u, tp, seq, pw, ps)
        y_p, h_p, c_p = _ssd_prompt(z, xbc, dtr, bp, seq, lp)
        new_pool_p.append(buf_p)
        new_conv_p.append(c_p)
        new_ssm_p.append(h_p.reshape(bp, n_heads, HEAD_DIM, D_STATE))

        u_s = u[tp:].reshape(bs, n_new, d_pool)
        xbc_s = xbc[tp:].reshape(bs, n_new, d_xbc)
        pool_ext = jnp.concatenate([state_pool[l], u_s], axis=1)
        pool_s = _pool_sample(jnp.swapaxes(pool_ext, 0, 1), pw, ps, n_new, PAST_LEN)
        pool_s = jnp.swapaxes(pool_s, 0, 1).reshape(ts, d_pool)
        conv_ext = jnp.concatenate([state_conv[l], xbc_s], axis=1)
        lead = CONV_LEAD - (CONV_W - 1)
        ext = jnp.pad(conv_ext, ((0, 0), (lead, SAMPLE_EXT_ROWS - lead - conv_ext.shape[1]), (0, 0)))
        pad_rows = lambda a: jnp.pad(a.reshape(bs, n_new, -1), ((0, 0), (0, SAMPLE_ROWS - n_new), (0, 0))
                                     ).reshape(bs * SAMPLE_ROWS, -1)
        y_s16, h_s = _ssd_sample(pad_rows(z[tp:]), ext.reshape(bs * SAMPLE_EXT_ROWS, d_xbc), pad_rows(dtr[tp:]),
                                 state_ssm[l].reshape(bs, d_ssm, D_STATE), lp, n_new)
        y_s = y_s16.reshape(bs, SAMPLE_ROWS, d_ssm)[:, :n_new].reshape(ts, d_ssm)
        new_pool_s.append(pool_ext[:, n_new:])
        new_conv_s.append(conv_ext[:, n_new:])
        new_ssm_s.append(h_s.reshape(bs, n_heads, HEAD_DIM, D_STATE))

        w_o = w_out[l].astype(BF16)
        ln1 = (ln1_g[l][None, :], ln1_b[l][None, :])
        g2, b2 = ln2_g[l][None, :], ln2_b[l][None, :]
        j = l // 2
        if l % 2 == 0:
            x1 = _outproj_ln(x_segs, [pool_p, pool_s], [y_p, y_s], w_o[:d_pool], w_o[d_pool:], *ln1, alpha)
            x_segs = _dense_ffn_ln(x1, ffn_w_gate[j:j + 1].astype(BF16), ffn_w_up[j:j + 1].astype(BF16),
                                   ffn_w_down[j:j + 1].astype(BF16), g2, b2, alpha, out_rows)
        else:
            n_experts = router_w.shape[2]
            rw_pad = jnp.zeros((d, LANES), F32).at[:, :n_experts].set(router_w[j])
            rb_pad = jnp.zeros((1, LANES), F32).at[0, :n_experts].set(router_b[j])
            xt, idx, gate, rank, counts = _outproj_ln(x_segs, [pool_p, pool_s], [y_p, y_s], w_o[:d_pool],
                                                      w_o[d_pool:], *ln1, alpha, router=(rw_pad, rb_pad, n_experts))
            tm = _common_tile([tp, ts], (512, 256, 128, 64))
            x_segs = _moe_ffn_ln(xt, idx, gate, rank, counts, moe_w_gate[j].astype(BF16), moe_w_up[j].astype(BF16),
                                 moe_w_down[j].astype(BF16), g2, b2, alpha, tm, out_rows)

    y_prompt, y_sample = x_segs
    return (y_prompt.reshape(bp, seq, d), y_sample.reshape(bs, n_new, d),
            jnp.stack(new_pool_p), jnp.stack(new_conv_p), jnp.stack(new_ssm_p),
            jnp.stack(new_pool_s), jnp.stack(new_conv_s), jnp.stack(new_ssm_s))
```

```python
import functools

import jax
import jax.numpy as jnp
from jax import lax
from jax.experimental import pallas as pl
from jax.experimental.pallas import tpu as pltpu

F32 = jnp.float32
BF16 = jnp.bfloat16

PAST_LEN = 16384
POOL_WINDOWS = (2, 4, 8, 16)
POOL_GROUP = 128
POOL_BUF = max(POOL_WINDOWS) - 1
HEAD_DIM = 64
N_GROUPS = 4
D_STATE = 128
CONV_W = 4
CHUNK = 128
TOP_K = 2
LN_EPS = 1e-5
RMS_EPS = 1e-6

LANES = 128
SUBLANES = 8
V7X_VMEM_BYTES = 64 * 1024 * 1024
VMEM_CAP = V7X_VMEM_BYTES - 8 * 1024 * 1024

SAMPLE_ROWS = 16
SAMPLE_EXT_ROWS = 24
CONV_LEAD = 8
MOE_TILE = 256


def _vmem_limit(nbytes):
    return int(min(VMEM_CAP, nbytes + 6 * 1024 * 1024))


def _dot(a, b):
    return jnp.dot(a, b, preferred_element_type=F32)


def _dot_nt(a, b):
    return lax.dot_general(a, b, (((1,), (1,)), ((), ())), preferred_element_type=F32)


def _split3(v):
    hi = v.astype(BF16)
    r = v - hi.astype(F32)
    mid = r.astype(BF16)
    lo = (r - mid.astype(F32)).astype(BF16)
    return hi, mid, lo


def _sel_right(parts, m):
    return _dot(parts[0], m) + _dot(parts[1], m) + _dot(parts[2], m)


def _sel_left(m, parts):
    return _dot(m, parts[0]) + _dot(m, parts[1]) + _dot(m, parts[2])


def _silu(x):
    return x / (1.0 + jnp.exp(-x))


def _layer_norm(h, g, b):
    mu = jnp.mean(h, axis=-1, keepdims=True)
    d = h - mu
    var = jnp.mean(d * d, axis=-1, keepdims=True)
    return d * lax.rsqrt(var + LN_EPS) * g + b


def _pick_tile(n, candidates):
    for c in candidates:
        if n % c == 0:
            return c
    raise ValueError(f"no tile in {candidates} divides {n}")


def _const_spec(shape):
    nd = len(shape)
    return pl.BlockSpec(shape, lambda *_: (0,) * nd)


def _seg_tiles(segs, tm, rows_per_token=1):
    return [a.shape[0] // (tm * rows_per_token) for a in segs]


def _seg_offsets(tiles):
    offs, off = [], 0
    for n in tiles:
        offs.append(off)
        off += n
    return offs


def _seg_specs(segs, tm, rows_per_token=1):
    tiles = _seg_tiles(segs, tm, rows_per_token)
    return [pl.BlockSpec((tm * rows_per_token, a.shape[1]),
                         lambda i, *_, o=o, n=n: (jnp.clip(i - o, 0, n - 1), 0))
            for a, o, n in zip(segs, _seg_offsets(tiles), tiles)]


def _seg_read(refs, offs):
    v = refs[0][...]
    for r, o in zip(refs[1:], offs[1:]):
        v = jnp.where(pl.program_id(0) >= o, r[...], v)
    return v


def _seg_write(refs, offs, tiles, v):
    i = pl.program_id(0)
    for r, o, n in zip(refs, offs, tiles):
        @pl.when(jnp.logical_and(i >= o, i < o + n))
        def _(r=r):
            r[...] = v.astype(r.dtype)


def _common_tile(row_counts, candidates):
    for c in candidates:
        if all(n % c == 0 for n in row_counts):
            return c
    raise ValueError(f"no tile in {candidates} divides all of {row_counts}")


def _inproj_kernel(*refs, n_x, offs, splits):
    x_refs, w_ref, out_refs = refs[:n_x], refs[n_x], refs[n_x + 1:]
    xb = _seg_read(x_refs, offs).astype(BF16)
    for ref, (lo, hi) in zip(out_refs, splits):
        ref[...] = _dot(xb, w_ref[:, lo:hi])


def _in_proj(x_segs, w_pad, d_pool, d_ssm, d_xbc):
    d, n = w_pad.shape
    tm = _common_tile([a.shape[0] for a in x_segs], (512, 256, 128, 64))
    tiles = _seg_tiles(x_segs, tm)
    t = sum(tiles) * tm
    splits = ((0, d_pool), (d_pool, d_pool + d_ssm), (d_pool + d_ssm, d_pool + d_ssm + d_xbc),
              (d_pool + d_ssm + d_xbc, n))
    widths = [hi - lo for lo, hi in splits]
    vmem = 2 * len(x_segs) * tm * d * 4 + 2 * d * n * 2 + 2 * tm * n * 4 + tm * n * 4
    return pl.pallas_call(
        functools.partial(_inproj_kernel, n_x=len(x_segs), offs=_seg_offsets(tiles), splits=splits),
        grid=(t // tm,),
        in_specs=_seg_specs(x_segs, tm) + [_const_spec((d, n))],
        out_specs=[pl.BlockSpec((tm, w), lambda i: (i, 0)) for w in widths],
        out_shape=[jax.ShapeDtypeStruct((t, w), F32) for w in widths],
        compiler_params=pltpu.CompilerParams(dimension_semantics=("arbitrary",),
                                             vmem_limit_bytes=_vmem_limit(vmem)),
        name="in_proj",
    )(*x_segs, w_pad)


def _pool_prompt_kernel(u_ref, w_ref, scale_ref, o_ref, buf_ref, ext_ref, *, tl):
    j = pl.program_id(1)

    @pl.when(j == pl.num_programs(1) - 1)
    def _():
        buf_ref[0] = u_ref[tl - POOL_BUF:tl, :]

    @pl.when(j == 0)
    def _():
        ext_ref[0:16, :] = jnp.zeros((16, ext_ref.shape[1]), F32)

    @pl.when(j > 0)
    def _():
        ext_ref[0:16, :] = ext_ref[tl:tl + 16, :]

    ext_ref[16:16 + tl, :] = u_ref[...]
    pos = j * tl + lax.broadcasted_iota(jnp.int32, (tl, POOL_GROUP), 0)
    for g, win in enumerate(POOL_WINDOWS):
        lanes = slice(g * POOL_GROUP, (g + 1) * POOL_GROUP)
        cur = ext_ref[16:16 + tl, lanes]
        acc = cur
        for k in range(1, win):
            acc = acc + ext_ref[16 - k:16 - k + tl, lanes]
        cnt = jnp.minimum(pos + 1, win).astype(F32)
        diff = (acc / cnt - cur).astype(BF16)
        out = _dot(diff, w_ref[g]) * scale_ref[:, lanes]
        o_ref[:, lanes] = out.astype(o_ref.dtype)


def _pool_prompt(u, n_rows, seq_len, pool_w, pool_scale):
    d_pool = u.shape[1]
    tl = _pick_tile(seq_len, (512, 256, 128))
    nj = seq_len // tl
    nb = n_rows // seq_len
    return pl.pallas_call(
        functools.partial(_pool_prompt_kernel, tl=tl),
        grid=(nb, nj),
        in_specs=[pl.BlockSpec((tl, d_pool), lambda b, j: (b * nj + j, 0)),
                  _const_spec(pool_w.shape), _const_spec(pool_scale.shape)],
        out_specs=[pl.BlockSpec((tl, d_pool), lambda b, j: (b * nj + j, 0)),
                   pl.BlockSpec((1, POOL_BUF, d_pool), lambda b, j: (b, 0, 0))],
        out_shape=[jax.ShapeDtypeStruct((n_rows, d_pool), BF16),
                   jax.ShapeDtypeStruct((nb, POOL_BUF, d_pool), F32)],
        scratch_shapes=[pltpu.VMEM((tl + 16, d_pool), F32)],
        compiler_params=pltpu.CompilerParams(dimension_semantics=("arbitrary", "arbitrary")),
        name="pool_prompt",
    )(u, pool_w, pool_scale)


def _pool_sample_kernel(ext_ref, w_ref, scale_ref, o_ref, *, n_new, start):
    for t in range(n_new):
        for g, win in enumerate(POOL_WINDOWS):
            lanes = slice(g * POOL_GROUP, (g + 1) * POOL_GROUP)
            cur = ext_ref[POOL_BUF + t, :, lanes]
            acc = cur
            for k in range(1, win):
                acc = acc + ext_ref[POOL_BUF + t - k, :, lanes]
            cnt = float(min(start + t + 1, win))
            diff = (acc / cnt - cur).astype(BF16)
            out = _dot(diff, w_ref[g]) * scale_ref[:, lanes]
            o_ref[t, :, lanes] = out.astype(o_ref.dtype)


def _pool_sample(ext_t, pool_w, pool_scale, n_new, start):
    rows, bs, d_pool = ext_t.shape
    return pl.pallas_call(
        functools.partial(_pool_sample_kernel, n_new=n_new, start=start),
        grid=(1,),
        in_specs=[_const_spec(ext_t.shape), _const_spec(pool_w.shape), _const_spec(pool_scale.shape)],
        out_specs=_const_spec((n_new, bs, d_pool)),
        out_shape=jax.ShapeDtypeStruct((n_new, bs, d_pool), BF16),
        compiler_params=pltpu.CompilerParams(dimension_semantics=("arbitrary",)),
        name="pool_sample",
    )(ext_t, pool_w, pool_scale)


def _conv_silu(window, w_ref, b_ref):
    acc = b_ref[...] + window(0) * w_ref[0:1, :]
    for k in range(1, CONV_W):
        acc = acc + window(k) * w_ref[k:k + 1, :]
    return _silu(acc)


def _softplus(x):
    return jnp.maximum(x, 0.0) + jnp.log1p(jnp.exp(-jnp.abs(x)))


def _ssd_chunk(xact, z, dt, prm, seq_rows, read_state, write_state):
    (tri_ref, tris_ref, sel64_ref, sel128_ref, alog_ref, dexp_ref, normw_ref) = prm
    q = xact.shape[0]
    d_ssm = z.shape[1]
    gw = d_ssm // N_GROUPS
    n_seq = q // seq_rows
    xs = xact[:, :d_ssm]
    bm = xact[:, d_ssm:d_ssm + N_GROUPS * D_STATE]
    cm = xact[:, d_ssm + N_GROUPS * D_STATE:]

    a = dt * (-jnp.exp(alog_ref[...]))
    a3 = _split3(a)
    cum = _sel_left(tri_ref[...], a3)
    rcum = _sel_left(tris_ref[...], a3)
    w = jnp.exp(rcum) * dt
    cum3 = _split3(cum)
    cum_t = cum.T
    cum_e128 = _sel_right(cum3, sel128_ref[...])
    cum_e64 = _sel_right(cum3, sel64_ref[...])
    dt_e = _sel_right(_split3(dt), sel64_ref[...])
    w_e = _sel_right(_split3(w), sel64_ref[...])
    xdt = xs * dt_e
    wx = xs * w_e

    ii = lax.broadcasted_iota(jnp.int32, (q, q), 0)
    jj = lax.broadcasted_iota(jnp.int32, (q, q), 1)
    mask = ii >= jj
    if n_seq > 1:
        mask = jnp.logical_and(mask, (ii // seq_rows) == (jj // seq_rows))
    lo_half = lax.broadcasted_iota(jnp.int32, (q, LANES), 1) < HEAD_DIM
    col = lax.broadcasted_iota(jnp.int32, (gw, q), 1)

    ydiag, yoff = [], []
    for g in range(N_GROUPS):
        bg = bm[:, g * D_STATE:(g + 1) * D_STATE].astype(BF16)
        cg = cm[:, g * D_STATE:(g + 1) * D_STATE].astype(BF16)
        cb = _dot_nt(cg, bg)
        for pr in range(gw // LANES):
            blk = g * (gw // LANES) + pr
            xp = xdt[:, blk * LANES:(blk + 1) * LANES].astype(BF16)
            ys = []
            for half in range(2):
                h = 2 * blk + half
                seg = cum_e128[:, h * LANES:(h + 1) * LANES] - cum_t[h:h + 1, :]
                dec = jnp.exp(jnp.where(mask, seg, -jnp.inf))
                ys.append(_dot((cb * dec).astype(BF16), xp))
            ydiag.append(jnp.where(lo_half, ys[0], ys[1]))
        wx_t = wx[:, g * gw:(g + 1) * gw].T
        yoff_rows = []
        for s in range(n_seq):
            r0 = s * seq_rows
            st = read_state(s, g)
            yoff_rows.append(_dot_nt(cg[r0:r0 + seq_rows, :], st.astype(BF16)))
            last = r0 + seq_rows - 1
            scale = jnp.concatenate(
                [jnp.broadcast_to(jnp.exp(cum_e128[last:last + 1, h * LANES:(h + 1) * LANES]), (HEAD_DIM, D_STATE))
                 for h in range(g * (gw // HEAD_DIM), (g + 1) * (gw // HEAD_DIM))], axis=0)
            wsel = wx_t
            if n_seq > 1:
                wsel = jnp.where(jnp.logical_and(col >= r0, col < r0 + seq_rows), wx_t, 0.0)
            write_state(s, g, st * scale + _dot(wsel.astype(BF16), bg))
        yoff.append(yoff_rows[0] if n_seq == 1 else jnp.concatenate(yoff_rows, axis=0))
    y = (jnp.concatenate(ydiag, axis=1) + jnp.concatenate(yoff, axis=1) * jnp.exp(cum_e64)
         + xs * dexp_ref[...])
    gz = y * _silu(z)
    outs = []
    for g in range(N_GROUPS):
        gg = gz[:, g * gw:(g + 1) * gw]
        ms = jnp.sum(gg * gg, axis=-1, keepdims=True) * (1.0 / gw)
        outs.append(gg * lax.rsqrt(ms + RMS_EPS) * normw_ref[:, g * gw:(g + 1) * gw])
    return jnp.concatenate(outs, axis=1)


def _ssd_prompt_kernel(z_ref, xbc_ref, dtr_ref, convw_ref, convb_ref, dtb_ref,
                       tri_ref, tris_ref, sel64_ref, sel128_ref, alog_ref, dexp_ref, normw_ref,
                       y_ref, hout_ref, cout_ref, ext_ref, h_ref, *, gw):
    c = pl.program_id(1)
    nc = pl.num_programs(1)
    q = z_ref.shape[0]

    @pl.when(c == 0)
    def _():
        ext_ref[0:CONV_LEAD, :] = jnp.zeros((CONV_LEAD, ext_ref.shape[1]), F32)
        h_ref[...] = jnp.zeros(h_ref.shape, F32)

    @pl.when(c > 0)
    def _():
        ext_ref[0:CONV_LEAD, :] = ext_ref[q:q + CONV_LEAD, :]

    ext_ref[CONV_LEAD:CONV_LEAD + q, :] = xbc_ref[...]
    first = CONV_LEAD - (CONV_W - 1)
    xact = _conv_silu(lambda k: ext_ref[first + k:first + k + q, :], convw_ref, convb_ref)
    dt = _softplus(dtr_ref[...] + dtb_ref[...])

    def read_state(s, g):
        return h_ref[g * gw:(g + 1) * gw, :]

    def write_state(s, g, v):
        h_ref[g * gw:(g + 1) * gw, :] = v

    prm = (tri_ref, tris_ref, sel64_ref, sel128_ref, alog_ref, dexp_ref, normw_ref)
    y_ref[...] = _ssd_chunk(xact, z_ref[...], dt, prm, q, read_state, write_state).astype(y_ref.dtype)

    @pl.when(c == nc - 1)
    def _():
        hout_ref[0] = h_ref[...]
        cout_ref[0] = xbc_ref[q - (CONV_W - 1):q, :]


def _ssd_sample_kernel(z_ref, ext_ref, dtr_ref, hin_ref, convw_ref, convb_ref, dtb_ref,
                       tri_ref, tris_ref, sel64_ref, sel128_ref, alog_ref, dexp_ref, normw_ref,
                       *rest, gw, n_new):
    y_ref, hout_ref = rest[-2:]
    q = z_ref.shape[0]
    n_seq = q // SAMPLE_ROWS
    first = CONV_LEAD - (CONV_W - 1)

    def window(k):
        return jnp.concatenate(
            [ext_ref[s * SAMPLE_EXT_ROWS + first + k:s * SAMPLE_EXT_ROWS + first + k + SAMPLE_ROWS, :]
             for s in range(n_seq)], axis=0)

    xact = _conv_silu(window, convw_ref, convb_ref)
    row = lax.broadcasted_iota(jnp.int32, (q, LANES), 0)
    dt = jnp.where((row % SAMPLE_ROWS) < n_new, _softplus(dtr_ref[...] + dtb_ref[...]), 0.0)

    def read_state(s, g):
        return hin_ref[s, g * gw:(g + 1) * gw, :]

    def write_state(s, g, v):
        hout_ref[s, g * gw:(g + 1) * gw, :] = v

    prm = (tri_ref, tris_ref, sel64_ref, sel128_ref, alog_ref, dexp_ref, normw_ref)
    y_ref[...] = _ssd_chunk(xact, z_ref[...], dt, prm, SAMPLE_ROWS, read_state, write_state).astype(y_ref.dtype)


def _ssd_consts(seq_rows, n_heads):
    q = CHUNK
    i = jnp.arange(q)[:, None]
    j = jnp.arange(q)[None, :]
    same = (i // seq_rows) == (j // seq_rows)
    tri = jnp.logical_and(same, j <= i).astype(BF16)
    tris = jnp.logical_and(same, j > i).astype(BF16)
    hrow = jnp.arange(LANES)[:, None]
    sel64 = (hrow == (jnp.arange(n_heads * HEAD_DIM)[None, :] // HEAD_DIM)).astype(BF16)
    sel128 = (hrow == (jnp.arange(n_heads * LANES)[None, :] // LANES)).astype(BF16)
    return tri, tris, sel64, sel128


def _ssd_param_specs(prm_arrays):
    return [_const_spec(a.shape) for a in prm_arrays]


def _ssd_prompt(z, xbc, dtr, n_seq, seq_len, lp):
    d_ssm = z.shape[1]
    d_xbc = xbc.shape[1]
    q = CHUNK
    nc = seq_len // q
    gw = d_ssm // N_GROUPS
    prm = (lp["conv_w"], lp["conv_b"], lp["dt_bias"], *lp["ssd_consts_prompt"], lp["a_log"], lp["d_exp"],
           lp["norm_w"])
    rows = lambda b, c: (b * nc + c, 0)
    return pl.pallas_call(
        functools.partial(_ssd_prompt_kernel, gw=gw),
        grid=(n_seq, nc),
        in_specs=[pl.BlockSpec((q, d_ssm), rows), pl.BlockSpec((q, d_xbc), rows),
                  pl.BlockSpec((q, LANES), rows)] + _ssd_param_specs(prm),
        out_specs=[pl.BlockSpec((q, d_ssm), rows),
                   pl.BlockSpec((1, d_ssm, D_STATE), lambda b, c: (b, 0, 0)),
                   pl.BlockSpec((1, CONV_W - 1, d_xbc), lambda b, c: (b, 0, 0))],
        out_shape=[jax.ShapeDtypeStruct((n_seq * seq_len, d_ssm), BF16),
                   jax.ShapeDtypeStruct((n_seq, d_ssm, D_STATE), F32),
                   jax.ShapeDtypeStruct((n_seq, CONV_W - 1, d_xbc), F32)],
        scratch_shapes=[pltpu.VMEM((q + CONV_LEAD, d_xbc), F32), pltpu.VMEM((d_ssm, D_STATE), F32)],
        compiler_params=pltpu.CompilerParams(dimension_semantics=("arbitrary", "arbitrary"),
                                             vmem_limit_bytes=_vmem_limit(40 * 1024 * 1024)),
        name="ssd_prompt",
    )(z, xbc, dtr, *prm)


def _ssd_sample(z16, ext, dtr16, h_all, h_new, layer, lp, n_new):
    d_ssm = z16.shape[1]
    d_xbc = ext.shape[1]
    q = CHUNK
    spc = q // SAMPLE_ROWS
    bs = h_all.shape[1]
    gw = d_ssm // N_GROUPS
    prm = (lp["conv_w"], lp["conv_b"], lp["dt_bias"], *lp["ssd_consts_sample"], lp["a_log"], lp["d_exp"],
           lp["norm_w"])
    rows = lambda i: (i, 0)
    slab = pl.BlockSpec((None, spc, d_ssm, D_STATE), lambda i: (layer, i, 0, 0))
    state_bytes = spc * d_ssm * D_STATE * 4
    prev = [] if h_new is None else [h_new]
    n_in = 4 + len(prm)
    return pl.pallas_call(
        functools.partial(_ssd_sample_kernel, gw=gw, n_new=n_new),
        grid=(bs // spc,),
        in_specs=[pl.BlockSpec((q, d_ssm), rows), pl.BlockSpec((spc * SAMPLE_EXT_ROWS, d_xbc), rows),
                  pl.BlockSpec((q, LANES), rows), slab] + _ssd_param_specs(prm)
                 + [pl.BlockSpec(memory_space=pl.ANY) for _ in prev],
        out_specs=[pl.BlockSpec((q, d_ssm), rows), slab],
        out_shape=[jax.ShapeDtypeStruct((bs * SAMPLE_ROWS, d_ssm), BF16),
                   jax.ShapeDtypeStruct(h_all.shape, F32)],
        input_output_aliases={n_in: 1} if prev else {},
        compiler_params=pltpu.CompilerParams(dimension_semantics=("arbitrary",),
                                             vmem_limit_bytes=_vmem_limit(4 * state_bytes + 20 * 1024 * 1024)),
        name="ssd_sample",
    )(z16, ext, dtr16, h_all, *prm, *prev)


def _route(h, rw_ref, rb_ref, tril_ref, cnt_ref, n_experts):
    xh = h.astype(BF16)
    xl = (h - xh.astype(F32)).astype(BF16)
    w = rw_ref[...]
    wh = w.astype(BF16)
    wl = (w - wh.astype(F32)).astype(BF16)
    logits = _dot(xh, wh) + (_dot(xh, wl) + _dot(xl, wh)) + rb_ref[...]
    lane = lax.broadcasted_iota(jnp.int32, logits.shape, 1)
    lane_f = lane.astype(F32)
    logits = jnp.where(lane < n_experts, logits, -jnp.inf)
    m1 = jnp.max(logits, axis=-1, keepdims=True)
    i1 = jnp.min(jnp.where(logits == m1, lane_f, float(LANES)), axis=-1, keepdims=True)
    rest = jnp.where(lane_f == i1, -jnp.inf, logits)
    m2 = jnp.max(rest, axis=-1, keepdims=True)
    i2 = jnp.min(jnp.where(rest == m2, lane_f, float(LANES)), axis=-1, keepdims=True)
    e2 = jnp.exp(m2 - m1)
    den = 1.0 + e2
    oh1 = jnp.where(lane_f == i1, 1.0, 0.0)
    oh2 = jnp.where(lane_f == i2, 1.0, 0.0)
    before1 = _dot(tril_ref[...], oh1.astype(BF16))
    before2 = _dot(tril_ref[...], oh2.astype(BF16))
    c1 = jnp.sum(oh1, axis=0, keepdims=True)
    c2 = jnp.sum(oh2, axis=0, keepdims=True)
    base = cnt_ref[...]
    r1 = jnp.sum(oh1 * (before1 + base), axis=-1, keepdims=True)
    r2 = jnp.sum(oh2 * (before2 + (base + c1)), axis=-1, keepdims=True)
    cnt_ref[...] = base + (c1 + c2)
    idx = jnp.where(lane == 0, i1, jnp.where(lane == 1, i2, 0.0)).astype(jnp.int32)
    gate = jnp.where(lane == 0, 1.0 / den, jnp.where(lane == 1, e2 / den, 0.0))
    rank = jnp.where(lane == 0, r1, jnp.where(lane == 1, r2, 0.0)).astype(jnp.int32)
    return idx, gate, rank


def _outproj_ln_kernel(*refs, alpha, n_seg, offs, route, n_experts):
    it = iter(refs)
    take = lambda n: [next(it) for _ in range(n)]
    x_refs, p_refs, s_refs = take(n_seg[0]), take(n_seg[1]), take(n_seg[2])
    wp_ref, ws_ref, g_ref, b_ref = take(4)
    x = _seg_read(x_refs, offs[0])
    mixed = _dot(_seg_read(p_refs, offs[1]), wp_ref[...]) + _dot(_seg_read(s_refs, offs[2]), ws_ref[...])
    h = _layer_norm(alpha * x + mixed, g_ref[...], b_ref[...])
    if not route:
        (o_ref,) = take(1)
        o_ref[...] = h
        return
    rw_ref, rb_ref, tril_ref = take(3)
    xt_ref, idx_ref, gate_ref, rank_ref, cnt_ref = take(5)

    @pl.when(pl.program_id(0) == 0)
    def _():
        cnt_ref[...] = jnp.zeros(cnt_ref.shape, F32)

    tm, d = h.shape
    rpt = d // LANES
    for k in range(rpt):
        xt_ref[pl.ds(k, tm, stride=rpt), :] = h[:, k * LANES:(k + 1) * LANES]
    idx, gate, rank = _route(h, rw_ref, rb_ref, tril_ref, cnt_ref, n_experts)
    idx_ref[...] = idx
    gate_ref[...] = gate
    rank_ref[...] = rank


def _outproj_ln(x_segs, pool_segs, ssd_segs, w_pool, w_ssd, g, b, alpha, router=None):
    d = x_segs[0].shape[1]
    rows = [a.shape[0] for a in x_segs + pool_segs + ssd_segs]
    tm = _common_tile(rows, (512, 256, 128, 64))
    groups = (x_segs, pool_segs, ssd_segs)
    tiles = [_seg_tiles(s, tm) for s in groups]
    t = sum(tiles[0]) * tm
    assert all(sum(ts) * tm == t for ts in tiles)
    in_specs = [sp for s in groups for sp in _seg_specs(s, tm)]
    consts = [w_pool, w_ssd, g, b]
    row = lambda w, dt: (pl.BlockSpec((tm, w), lambda i: (i, 0)), jax.ShapeDtypeStruct((t, w), dt))
    if router is None:
        outs = [row(d, F32)]
        n_experts = 0
    else:
        rw_pad, rb_pad, n_experts = router
        tril = (jnp.arange(tm)[:, None] > jnp.arange(tm)[None, :]).astype(BF16)
        consts += [rw_pad, rb_pad, tril]
        rpt = d // LANES
        outs = [(pl.BlockSpec((tm * rpt, LANES), lambda i: (i, 0)), jax.ShapeDtypeStruct((t * rpt, LANES), F32)),
                row(LANES, jnp.int32), row(LANES, F32), row(LANES, jnp.int32),
                (_const_spec((1, LANES)), jax.ShapeDtypeStruct((1, LANES), F32))]
    in_specs += [_const_spec(c.shape) for c in consts]
    res = pl.pallas_call(
        functools.partial(_outproj_ln_kernel, alpha=alpha, n_seg=[len(s) for s in groups],
                          offs=[_seg_offsets(ts) for ts in tiles], route=router is not None, n_experts=n_experts),
        grid=(t // tm,),
        in_specs=in_specs,
        out_specs=[o[0] for o in outs],
        out_shape=[o[1] for o in outs],
        compiler_params=pltpu.CompilerParams(dimension_semantics=("arbitrary",),
                                             vmem_limit_bytes=_vmem_limit(40 * 1024 * 1024)),
        name="outproj_ln",
    )(*x_segs, *pool_segs, *ssd_segs, *consts)
    return res[0] if router is None else res


def _swiglu(xb, wg_ref, wu_ref, wd_ref):
    act = (_silu(_dot(xb, wg_ref[0])) * _dot(xb, wu_ref[0])).astype(BF16)
    return _dot(act, wd_ref[0])


def _dense_ffn_ln_kernel(x_ref, wg_ref, wu_ref, wd_ref, g_ref, b_ref, *out_refs, alpha, offs, tiles):
    x = x_ref[...]
    f = _swiglu(x.astype(BF16), wg_ref, wu_ref, wd_ref)
    _seg_write(out_refs, offs, tiles, _layer_norm(alpha * x + f, g_ref[...], b_ref[...]))


def _dense_ffn_ln(x, wg, wu, wd, g, b, alpha, out_rows):
    t, d = x.shape
    ff = wg.shape[2]
    tm = _common_tile(out_rows, (256, 128, 64))
    tiles = [n // tm for n in out_rows]
    offs = _seg_offsets(tiles)
    vmem = 2 * 3 * d * ff * 2 + 4 * tm * d * 4 + tm * ff * 12
    res = pl.pallas_call(
        functools.partial(_dense_ffn_ln_kernel, alpha=alpha, offs=offs, tiles=tiles),
        grid=(t // tm,),
        in_specs=[pl.BlockSpec((tm, d), lambda i: (i, 0)), _const_spec(wg.shape), _const_spec(wu.shape),
                  _const_spec(wd.shape), _const_spec(g.shape), _const_spec(b.shape)],
        out_specs=[pl.BlockSpec((tm, d), lambda i, o=o, n=n: (jnp.clip(i - o, 0, n - 1), 0))
                   for o, n in zip(offs, tiles)],
        out_shape=[jax.ShapeDtypeStruct((n, d), F32) for n in out_rows],
        compiler_params=pltpu.CompilerParams(dimension_semantics=("arbitrary",),
                                             vmem_limit_bytes=_vmem_limit(vmem)),
        name="dense_ffn_ln",
    )(x, wg, wu, wd, g, b)
    return list(res)


def _dispatch_kernel(ends_ref, slot_ref, x_ref, out_hbm, zeros_ref, sem, zsem, *, tokens, rpt, tile, n_experts):
    i = pl.program_id(0)

    def tail_copy(e):
        start = pl.multiple_of((ends_ref[e] - tile) * rpt, tile * rpt)
        return pltpu.make_async_copy(zeros_ref, out_hbm.at[pl.ds(start, tile * rpt)], zsem)

    def nonempty(e):
        return ends_ref[e] > (ends_ref[e - 1] if e > 0 else 0)

    def unused_copy(j):
        start = pl.multiple_of((ends_ref[n_experts - 1] + j * tile) * rpt, tile * rpt)
        return pltpu.make_async_copy(zeros_ref, out_hbm.at[pl.ds(start, tile * rpt)], zsem)

    def unused(j):
        return ends_ref[n_experts - 1] + (j + 1) * tile <= out_hbm.shape[0] // rpt

    @pl.when(i == 0)
    def _():
        zeros_ref[...] = jnp.zeros(zeros_ref.shape, zeros_ref.dtype)
        for e in range(n_experts):
            @pl.when(nonempty(e))
            def _(e=e):
                tail_copy(e).start()

            @pl.when(unused(e))
            def _(e=e):
                unused_copy(e).start()
        for e in range(n_experts):
            @pl.when(nonempty(e))
            def _(e=e):
                tail_copy(e).wait()

            @pl.when(unused(e))
            def _(e=e):
                unused_copy(e).wait()

    def issue(r, carry):
        src = x_ref.at[pl.ds(pl.multiple_of(r * rpt, rpt), rpt)]
        for k in range(TOP_K):
            dst = out_hbm.at[pl.ds(pl.multiple_of(slot_ref[0, 0, k * tokens + r] * rpt, rpt), rpt)]
            pltpu.make_async_copy(src, dst, sem).start()
        return carry

    lax.fori_loop(0, tokens, issue, 0)
    for k in range(TOP_K):
        pltpu.make_async_copy(x_ref, out_hbm.at[pl.ds(0, tokens * rpt)], sem).wait()


def _dispatch(xt, slots, ends, n_slots, tokens, rpt, n_experts):
    steps = slots.shape[0]
    return pl.pallas_call(
        functools.partial(_dispatch_kernel, tokens=tokens, rpt=rpt, tile=MOE_TILE, n_experts=n_experts),
        grid_spec=pltpu.PrefetchScalarGridSpec(
            num_scalar_prefetch=1, grid=(steps,),
            in_specs=[pl.BlockSpec((1, 1, TOP_K * tokens), lambda i, e: (i, 0, 0), memory_space=pltpu.SMEM),
                      pl.BlockSpec((tokens * rpt, LANES), lambda i, e: (i, 0))],
            out_specs=pl.BlockSpec(memory_space=pl.ANY),
            scratch_shapes=[pltpu.VMEM((MOE_TILE * rpt, LANES), xt.dtype),
                            pltpu.SemaphoreType.DMA(()), pltpu.SemaphoreType.DMA(())]),
        out_shape=jax.ShapeDtypeStruct((n_slots * rpt, LANES), xt.dtype),
        compiler_params=pltpu.CompilerParams(dimension_semantics=("arbitrary",)),
        name="moe_dispatch",
    )(ends, slots, xt)


def _moe_ffn_kernel(te_ref, na_ref, x_ref, wg_ref, wu_ref, wd_ref, o_ref, *, tm, rpt):
    active = pl.program_id(0) < na_ref[0]

    @pl.when(active)
    def _():
        xb = jnp.concatenate([x_ref[pl.ds(k, tm, stride=rpt), :] for k in range(rpt)], axis=1).astype(BF16)
        f = _swiglu(xb, wg_ref, wu_ref, wd_ref)
        for k in range(rpt):
            o_ref[pl.ds(k, tm, stride=rpt), :] = f[:, k * LANES:(k + 1) * LANES]

    @pl.when(jnp.logical_not(active))
    def _():
        o_ref[...] = jnp.zeros(o_ref.shape, o_ref.dtype)


def _moe_ffn(xs, wg, wu, wd, tile_expert, n_active, rpt):
    tm = MOE_TILE
    d, ff = wg.shape[1], wg.shape[2]
    n_tiles = xs.shape[0] // (tm * rpt)
    w_idx = lambda i, te, na: (te[i], 0, 0)
    vmem = 2 * 3 * d * ff * 2 + 4 * tm * d * 4 + tm * ff * 12 + 2 * tm * d * 4
    return pl.pallas_call(
        functools.partial(_moe_ffn_kernel, tm=tm, rpt=rpt),
        grid_spec=pltpu.PrefetchScalarGridSpec(
            num_scalar_prefetch=2, grid=(n_tiles,),
            in_specs=[pl.BlockSpec((tm * rpt, LANES), lambda i, te, na: (jnp.minimum(i, na[0] - 1), 0)),
                      pl.BlockSpec((1, d, ff), w_idx), pl.BlockSpec((1, d, ff), w_idx),
                      pl.BlockSpec((1, ff, d), w_idx)],
            out_specs=pl.BlockSpec((tm * rpt, LANES), lambda i, te, na: (i, 0))),
        out_shape=jax.ShapeDtypeStruct(xs.shape, F32),
        compiler_params=pltpu.CompilerParams(dimension_semantics=("arbitrary",),
                                             vmem_limit_bytes=_vmem_limit(vmem)),
        name="moe_ffn",
    )(tile_expert, n_active, xs, wg, wu, wd)


def _combine_ln_kernel(slot_ref, next_ref, x_ref, gate_ref, y_hbm, g_ref, b_ref, *rest, alpha, tm, rpt, offs, tiles):
    out_refs, (buf, sem) = rest[:len(tiles)], rest[len(tiles):]
    i = pl.program_id(0)
    n_rows = TOP_K * tm * rpt

    def issue(s_ref, slot):
        def body(r, carry):
            src = y_hbm.at[pl.ds(pl.multiple_of(s_ref[0, 0, r] * rpt, rpt), rpt)]
            pltpu.make_async_copy(src, buf.at[slot, pl.ds(pl.multiple_of(r * rpt, rpt), rpt)], sem.at[slot]).start()
            return carry
        lax.fori_loop(0, TOP_K * tm, body, 0)

    @pl.when(i == 0)
    def _():
        issue(slot_ref, 0)

    @pl.when(i + 1 < pl.num_programs(0))
    def _():
        issue(next_ref, (i + 1) % 2)

    cur = i % 2
    pltpu.make_async_copy(y_hbm.at[pl.ds(0, n_rows)], buf.at[cur], sem.at[cur]).wait()
    g1 = gate_ref[:, 0:1]
    g2 = gate_ref[:, 1:2]
    cols = []
    for k in range(rpt):
        xk = x_ref[pl.ds(k, tm, stride=rpt), :]
        ya = buf[cur, pl.ds(k, tm, stride=rpt), :]
        yb = buf[cur, pl.ds(tm * rpt + k, tm, stride=rpt), :]
        cols.append(alpha * xk + (g1 * ya + g2 * yb))
    h = jnp.concatenate(cols, axis=1)
    _seg_write(out_refs, offs, tiles, _layer_norm(h, g_ref[...], b_ref[...]))


def _combine_ln(xt, gate, ys, slots, g, b, alpha, tm, rpt, out_rows):
    d = g.shape[1]
    tiles = [n // tm for n in out_rows]
    offs = _seg_offsets(tiles)
    steps = slots.shape[0]
    slot_spec = lambda f: pl.BlockSpec((1, 1, TOP_K * tm), f, memory_space=pltpu.SMEM)
    res = pl.pallas_call(
        functools.partial(_combine_ln_kernel, alpha=alpha, tm=tm, rpt=rpt, offs=offs, tiles=tiles),
        grid=(steps,),
        in_specs=[slot_spec(lambda i: (i, 0, 0)), slot_spec(lambda i: (jnp.minimum(i + 1, steps - 1), 0, 0)),
                  pl.BlockSpec((tm * rpt, LANES), lambda i: (i, 0)), pl.BlockSpec((tm, LANES), lambda i: (i, 0)),
                  pl.BlockSpec(memory_space=pl.ANY), _const_spec(g.shape), _const_spec(b.shape)],
        out_specs=[pl.BlockSpec((tm, d), lambda i, o=o, n=n: (jnp.clip(i - o, 0, n - 1), 0))
                   for o, n in zip(offs, tiles)],
        out_shape=[jax.ShapeDtypeStruct((n, d), F32) for n in out_rows],
        scratch_shapes=[pltpu.VMEM((2, TOP_K * tm * rpt, LANES), F32), pltpu.SemaphoreType.DMA((2,))],
        compiler_params=pltpu.CompilerParams(dimension_semantics=("arbitrary",),
                                             vmem_limit_bytes=_vmem_limit(32 * 1024 * 1024)),
        name="moe_combine_ln",
    )(slots, slots, xt, gate, ys, g, b)
    return list(res)


def _moe_ffn_ln(xt, idx, gate, rank, counts, wg, wu, wd, g, b, alpha, tm, out_rows):
    n_experts = wg.shape[0]
    d = wg.shape[1]
    rpt = d // LANES
    t = idx.shape[0]
    tile = MOE_TILE
    cnt = counts[0, :n_experts].astype(jnp.int32)
    padded = ((cnt + tile - 1) // tile) * tile
    ends = jnp.cumsum(padded).astype(jnp.int32)
    starts = ends - padded
    slot = jnp.take(starts, idx[:, :TOP_K]) + rank[:, :TOP_K]
    n_slots = -(-(TOP_K * t + n_experts * (tile - 1)) // tile) * tile
    n_tiles = n_slots // tile
    n_active = ends[-1:] // tile
    tile_start = jnp.minimum(jnp.arange(n_tiles, dtype=jnp.int32), n_active[0] - 1) * tile
    tile_expert = jnp.sum((tile_start[:, None] >= ends[None, :]).astype(jnp.int32), axis=1)
    slots = jnp.swapaxes(slot.reshape(t // tm, tm, TOP_K), 1, 2).reshape(t // tm, 1, TOP_K * tm)
    xs = _dispatch(xt, slots, ends, n_slots, tm, rpt, n_experts)
    ys = _moe_ffn(xs, wg, wu, wd, tile_expert, n_active, rpt)
    return _combine_ln(xt, gate, ys, slots, g, b, alpha, tm, rpt, out_rows)


def kernel(x_prompt, x_sample, state_pool, state_conv, state_ssm, w_in, conv_w, conv_b, dt_bias, A_log, D_skip,
           ssm_norm_w, pool_w, pool_scale, w_out, ln1_g, ln1_b, ln2_g, ln2_b, ffn_w_gate, ffn_w_up, ffn_w_down,
           router_w, router_b, moe_w_gate, moe_w_up, moe_w_down):
    bp, seq, d = x_prompt.shape
    bs, n_new, _ = x_sample.shape
    depth = w_in.shape[0]
    d_pool = pool_scale.shape[1]
    d_ssm = ssm_norm_w.shape[1]
    d_xbc = conv_w.shape[2]
    n_heads = dt_bias.shape[1]
    tp, ts = bp * seq, bs * n_new
    alpha = (2.0 * depth) ** 0.25
    assert d_pool == POOL_GROUP * len(POOL_WINDOWS) and d_ssm == n_heads * HEAD_DIM
    assert d_xbc == d_ssm + 2 * N_GROUPS * D_STATE and seq % CHUNK == 0 and n_new <= CONV_W
    assert n_heads <= LANES and bs % (CHUNK // SAMPLE_ROWS) == 0 and d % LANES == 0

    consts_prompt = _ssd_consts(CHUNK, n_heads)
    consts_sample = _ssd_consts(SAMPLE_ROWS, n_heads)
    x_segs = [x_prompt.reshape(tp, d), x_sample.reshape(ts, d)]
    new_pool_p, new_conv_p, new_ssm_p, new_pool_s, new_conv_s = [], [], [], [], []
    ssm_s = None
    for l in range(depth):
        last = l == depth - 1
        out_rows = [tp, ts] if last else [tp + ts]
        n_in = w_in.shape[2]
        w_in_pad = jnp.zeros((d, d_pool + d_ssm + d_xbc + LANES), BF16).at[:, :n_in].set(w_in[l].astype(BF16))
        lp = dict(
            conv_w=conv_w[l], conv_b=conv_b[l][None, :],
            dt_bias=jnp.zeros((1, LANES), F32).at[0, :n_heads].set(dt_bias[l]),
            a_log=jnp.zeros((1, LANES), F32).at[0, :n_heads].set(A_log[l]),
            d_exp=jnp.repeat(D_skip[l], HEAD_DIM)[None, :], norm_w=ssm_norm_w[l][None, :],
            ssd_consts_prompt=consts_prompt, ssd_consts_sample=consts_sample)
        pw = pool_w[l].astype(BF16)
        ps = pool_scale[l][None, :]

        u, z, xbc, dtr = _in_proj(x_segs, w_in_pad, d_pool, d_ssm, d_xbc)

        pool_p, buf_p = _pool_prompt(u, tp, seq, pw, ps)
        y_p, h_p, c_p = _ssd_prompt(z, xbc, dtr, bp, seq, lp)
        new_pool_p.append(buf_p)
        new_conv_p.append(c_p)
        new_ssm_p.append(h_p.reshape(bp, n_heads, HEAD_DIM, D_STATE))

        u_s = u[tp:].reshape(bs, n_new, d_pool)
        xbc_s = xbc[tp:].reshape(bs, n_new, d_xbc)
        pool_ext = jnp.concatenate([state_pool[l], u_s], axis=1)
        pool_s = _pool_sample(jnp.swapaxes(pool_ext, 0, 1), pw, ps, n_new, PAST_LEN)
        pool_s = jnp.swapaxes(pool_s, 0, 1).reshape(ts, d_pool)
        conv_ext = jnp.concatenate([state_conv[l], xbc_s], axis=1)
        lead = CONV_LEAD - (CONV_W - 1)
        ext = jnp.pad(conv_ext, ((0, 0), (lead, SAMPLE_EXT_ROWS - lead - conv_ext.shape[1]), (0, 0)))
        pad_rows = lambda a: jnp.pad(a.reshape(bs, n_new, -1), ((0, 0), (0, SAMPLE_ROWS - n_new), (0, 0))
                                     ).reshape(bs * SAMPLE_ROWS, -1)
        y_s16, ssm_s = _ssd_sample(pad_rows(z[tp:]), ext.reshape(bs * SAMPLE_EXT_ROWS, d_xbc), pad_rows(dtr[tp:]),
                                   state_ssm.reshape(depth, bs, d_ssm, D_STATE), ssm_s, l, lp, n_new)
        y_s = y_s16.reshape(bs, SAMPLE_ROWS, d_ssm)[:, :n_new].reshape(ts, d_ssm)
        new_pool_s.append(pool_ext[:, n_new:])
        new_conv_s.append(conv_ext[:, n_new:])

        w_o = w_out[l].astype(BF16)
        ln1 = (ln1_g[l][None, :], ln1_b[l][None, :])
        g2, b2 = ln2_g[l][None, :], ln2_b[l][None, :]
        j = l // 2
        if l % 2 == 0:
            x1 = _outproj_ln(x_segs, [pool_p, pool_s], [y_p, y_s], w_o[:d_pool], w_o[d_pool:], *ln1, alpha)
            x_segs = _dense_ffn_ln(x1, ffn_w_gate[j:j + 1].astype(BF16), ffn_w_up[j:j + 1].astype(BF16),
                                   ffn_w_down[j:j + 1].astype(BF16), g2, b2, alpha, out_rows)
        else:
            n_experts = router_w.shape[2]
            rw_pad = jnp.zeros((d, LANES), F32).at[:, :n_experts].set(router_w[j])
            rb_pad = jnp.zeros((1, LANES), F32).at[0, :n_experts].set(router_b[j])
            xt, idx, gate, rank, counts = _outproj_ln(x_segs, [pool_p, pool_s], [y_p, y_s], w_o[:d_pool],
                                                      w_o[d_pool:], *ln1, alpha, router=(rw_pad, rb_pad, n_experts))
            tm = _common_tile([tp, ts], (512, 256, 128, 64))
            x_segs = _moe_ffn_ln(xt, idx, gate, rank, counts, moe_w_gate[j].astype(BF16), moe_w_up[j].astype(BF16),
                                 moe_w_down[j].astype(BF16), g2, b2, alpha, tm, out_rows)

    y_prompt, y_sample = x_segs
    return (y_prompt.reshape(bp, seq, d), y_sample.reshape(bs, n_new, d),
            jnp.stack(new_pool_p), jnp.stack(new_conv_p), jnp.stack(new_ssm_p),
            jnp.stack(new_pool_s), jnp.stack(new_conv_s),
            ssm_s.reshape(depth, bs, n_heads, HEAD_DIM, D_STATE))
```

```python
import functools

import jax
import jax.numpy as jnp
from jax import lax
from jax.experimental import pallas as pl
from jax.experimental.pallas import tpu as pltpu

F32 = jnp.float32
BF16 = jnp.bfloat16

PAST_LEN = 16384
POOL_WINDOWS = (2, 4, 8, 16)
POOL_GROUP = 128
POOL_BUF = max(POOL_WINDOWS) - 1
HEAD_DIM = 64
N_GROUPS = 4
D_STATE = 128
CONV_W = 4
CHUNK = 128
TOP_K = 2
LN_EPS = 1e-5
RMS_EPS = 1e-6
LOG2_E = 1.4426950408889634

LANES = 128
SUBLANES = 8
V7X_VMEM_BYTES = 64 * 1024 * 1024
VMEM_CAP = V7X_VMEM_BYTES - 8 * 1024 * 1024

SAMPLE_ROWS = 16
SAMPLE_EXT_ROWS = 24
CONV_LEAD = 8
MOE_TILE = 256


def _vmem_limit(nbytes):
    return int(min(VMEM_CAP, nbytes + 6 * 1024 * 1024))


def _dot(a, b):
    return jnp.dot(a, b, preferred_element_type=F32)


def _dot_nt(a, b):
    return lax.dot_general(a, b, (((1,), (1,)), ((), ())), preferred_element_type=F32)


def _split3(v):
    hi = v.astype(BF16)
    r = v - hi.astype(F32)
    mid = r.astype(BF16)
    lo = (r - mid.astype(F32)).astype(BF16)
    return hi, mid, lo


def _sel_right(parts, m):
    out = _dot(parts[0], m)
    for p in parts[1:]:
        out = out + _dot(p, m)
    return out


def _sel_left(m, parts):
    out = _dot(m, parts[0])
    for p in parts[1:]:
        out = out + _dot(m, p)
    return out


def _silu(x):
    return x / (1.0 + jnp.exp2(x * (-LOG2_E)))


def _layer_norm(h, g, b):
    mu = jnp.mean(h, axis=-1, keepdims=True)
    d = h - mu
    var = jnp.mean(d * d, axis=-1, keepdims=True)
    return d * lax.rsqrt(var + LN_EPS) * g + b


def _pick_tile(n, candidates):
    for c in candidates:
        if n % c == 0:
            return c
    raise ValueError(f"no tile in {candidates} divides {n}")


def _const_spec(shape):
    nd = len(shape)
    return pl.BlockSpec(shape, lambda *_: (0,) * nd)


def _seg_tiles(segs, tm, rows_per_token=1):
    return [a.shape[0] // (tm * rows_per_token) for a in segs]


def _seg_offsets(tiles):
    offs, off = [], 0
    for n in tiles:
        offs.append(off)
        off += n
    return offs


def _seg_specs(segs, tm, rows_per_token=1):
    tiles = _seg_tiles(segs, tm, rows_per_token)
    return [pl.BlockSpec((tm * rows_per_token, a.shape[1]),
                         lambda i, *_, o=o, n=n: (jnp.clip(i - o, 0, n - 1), 0))
            for a, o, n in zip(segs, _seg_offsets(tiles), tiles)]


def _seg_read(refs, offs):
    v = refs[0][...]
    for r, o in zip(refs[1:], offs[1:]):
        v = jnp.where(pl.program_id(0) >= o, r[...], v)
    return v


def _seg_write(refs, offs, tiles, v):
    i = pl.program_id(0)
    for r, o, n in zip(refs, offs, tiles):
        @pl.when(jnp.logical_and(i >= o, i < o + n))
        def _(r=r):
            r[...] = v.astype(r.dtype)


def _common_tile(row_counts, candidates):
    for c in candidates:
        if all(n % c == 0 for n in row_counts):
            return c
    raise ValueError(f"no tile in {candidates} divides all of {row_counts}")


INPROJ_CHUNK = 512


def _inproj_kernel(x_ref, w_ref, convw_ref, convb_ref, dtb_ref, u_ref, sz_ref, xo_ref, dt_ref, *rest,
                   splits, tiles_per_seq):
    (u0, u1), (z0, z1), (c0, c1), (d0, d1) = splits
    tm = x_ref.shape[0]
    xb = x_ref[...].astype(BF16)
    u_ref[...] = _dot(xb, w_ref[:, u0:u1])
    dt_ref[...] = _softplus(_dot(xb, w_ref[:, d0:d1]) + dtb_ref[...])
    for lo in range(z0, z1, INPROJ_CHUNK):
        sz_ref[:, lo - z0:lo - z0 + INPROJ_CHUNK] = _silu(_dot(xb, w_ref[:, lo:lo + INPROJ_CHUNK]))
    if tiles_per_seq is None:
        for lo in range(c0, c1, INPROJ_CHUNK):
            xo_ref[:, lo - c0:lo - c0 + INPROJ_CHUNK] = _dot(xb, w_ref[:, lo:lo + INPROJ_CHUNK])
        return

    cstate_ref, ext_ref = rest
    pos = pl.program_id(0) % tiles_per_seq

    @pl.when(pos == 0)
    def _():
        ext_ref[0:CONV_LEAD, :] = jnp.zeros((CONV_LEAD, ext_ref.shape[1]), F32)

    @pl.when(pos > 0)
    def _():
        ext_ref[0:CONV_LEAD, :] = ext_ref[tm:tm + CONV_LEAD, :]

    first = CONV_LEAD - (CONV_W - 1)
    for lo in range(c0, c1, INPROJ_CHUNK):
        cols = slice(lo - c0, lo - c0 + INPROJ_CHUNK)
        ext_ref[CONV_LEAD:CONV_LEAD + tm, cols] = _dot(xb, w_ref[:, lo:lo + INPROJ_CHUNK])
        rows = ext_ref[:, cols]
        acc = convb_ref[:, cols] + (pltpu.roll(rows, CONV_W - 1, axis=0)[CONV_LEAD:] * convw_ref[0:1, cols])
        for k in range(1, CONV_W - 1):
            acc = acc + pltpu.roll(rows, CONV_W - 1 - k, axis=0)[CONV_LEAD:] * convw_ref[k:k + 1, cols]
        acc = acc + rows[CONV_LEAD:] * convw_ref[CONV_W - 1:CONV_W, cols]
        xo_ref[:, cols] = _silu(acc)

    @pl.when(pos == tiles_per_seq - 1)
    def _():
        cstate_ref[0] = ext_ref[CONV_LEAD + tm - (CONV_W - 1):CONV_LEAD + tm, :]


def _in_proj(x, w_pad, conv_w, conv_b, dt_bias, d_pool, d_ssm, d_xbc, seq_len=None):
    t, d = x.shape
    n = w_pad.shape[1]
    tm = _pick_tile(t if seq_len is None else seq_len, (512, 256, 128, 64))
    splits = ((0, d_pool), (d_pool, d_pool + d_ssm), (d_pool + d_ssm, d_pool + d_ssm + d_xbc),
              (d_pool + d_ssm + d_xbc, n))
    assert d_ssm % INPROJ_CHUNK == 0 and d_xbc % INPROJ_CHUNK == 0
    widths = [hi - lo for lo, hi in splits]
    out_specs = [pl.BlockSpec((tm, w), lambda i: (i, 0)) for w in widths]
    out_shape = [jax.ShapeDtypeStruct((t, w), F32) for w in widths]
    scratch = []
    tiles_per_seq = None
    if seq_len is not None:
        tiles_per_seq = seq_len // tm
        out_specs.append(pl.BlockSpec((1, CONV_W - 1, d_xbc), lambda i: (i // tiles_per_seq, 0, 0)))
        out_shape.append(jax.ShapeDtypeStruct((t // seq_len, CONV_W - 1, d_xbc), F32))
        scratch.append(pltpu.VMEM((tm + CONV_LEAD, d_xbc), F32))
    vmem = 2 * tm * d * 4 + 2 * d * n * 2 + 2 * tm * n * 4 + 2 * tm * d_xbc * 4 + 4 * tm * INPROJ_CHUNK * 4
    return pl.pallas_call(
        functools.partial(_inproj_kernel, splits=splits, tiles_per_seq=tiles_per_seq),
        grid=(t // tm,),
        in_specs=[pl.BlockSpec((tm, d), lambda i: (i, 0)), _const_spec((d, n)), _const_spec(conv_w.shape),
                  _const_spec(conv_b.shape), _const_spec(dt_bias.shape)],
        out_specs=out_specs,
        out_shape=out_shape,
        scratch_shapes=scratch,
        compiler_params=pltpu.CompilerParams(dimension_semantics=("arbitrary",),
                                             vmem_limit_bytes=_vmem_limit(vmem)),
        name="in_proj",
    )(x, w_pad, conv_w, conv_b, dt_bias)


def _pool_prompt_kernel(u_ref, w_ref, scale_ref, o_ref, buf_ref, ext_ref, *, tl):
    j = pl.program_id(1)

    @pl.when(j == pl.num_programs(1) - 1)
    def _():
        buf_ref[0] = u_ref[tl - POOL_BUF:tl, :]

    @pl.when(j == 0)
    def _():
        ext_ref[0:16, :] = jnp.zeros((16, ext_ref.shape[1]), F32)

    @pl.when(j > 0)
    def _():
        ext_ref[0:16, :] = ext_ref[tl:tl + 16, :]

    ext_ref[16:16 + tl, :] = u_ref[...]
    pos = j * tl + lax.broadcasted_iota(jnp.int32, (tl, POOL_GROUP), 0)
    for g, win in enumerate(POOL_WINDOWS):
        lanes = slice(g * POOL_GROUP, (g + 1) * POOL_GROUP)
        cur = ext_ref[16:16 + tl, lanes]
        acc = cur
        for k in range(1, win):
            acc = acc + ext_ref[16 - k:16 - k + tl, lanes]
        cnt = jnp.minimum(pos + 1, win).astype(F32)
        diff = (acc / cnt - cur).astype(BF16)
        out = _dot(diff, w_ref[g]) * scale_ref[:, lanes]
        o_ref[:, lanes] = out.astype(o_ref.dtype)


def _pool_prompt(u, n_rows, seq_len, pool_w, pool_scale):
    d_pool = u.shape[1]
    tl = _pick_tile(seq_len, (512, 256, 128))
    nj = seq_len // tl
    nb = n_rows // seq_len
    return pl.pallas_call(
        functools.partial(_pool_prompt_kernel, tl=tl),
        grid=(nb, nj),
        in_specs=[pl.BlockSpec((tl, d_pool), lambda b, j: (b * nj + j, 0)),
                  _const_spec(pool_w.shape), _const_spec(pool_scale.shape)],
        out_specs=[pl.BlockSpec((tl, d_pool), lambda b, j: (b * nj + j, 0)),
                   pl.BlockSpec((1, POOL_BUF, d_pool), lambda b, j: (b, 0, 0))],
        out_shape=[jax.ShapeDtypeStruct((n_rows, d_pool), BF16),
                   jax.ShapeDtypeStruct((nb, POOL_BUF, d_pool), F32)],
        scratch_shapes=[pltpu.VMEM((tl + 16, d_pool), F32)],
        compiler_params=pltpu.CompilerParams(dimension_semantics=("arbitrary", "arbitrary")),
        name="pool_prompt",
    )(u, pool_w, pool_scale)


def _pool_sample_kernel(ext_ref, w_ref, scale_ref, o_ref, *, n_new, start):
    for t in range(n_new):
        for g, win in enumerate(POOL_WINDOWS):
            lanes = slice(g * POOL_GROUP, (g + 1) * POOL_GROUP)
            cur = ext_ref[POOL_BUF + t, :, lanes]
            acc = cur
            for k in range(1, win):
                acc = acc + ext_ref[POOL_BUF + t - k, :, lanes]
            cnt = float(min(start + t + 1, win))
            diff = (acc / cnt - cur).astype(BF16)
            out = _dot(diff, w_ref[g]) * scale_ref[:, lanes]
            o_ref[t, :, lanes] = out.astype(o_ref.dtype)


def _pool_sample(ext_t, pool_w, pool_scale, n_new, start):
    rows, bs, d_pool = ext_t.shape
    return pl.pallas_call(
        functools.partial(_pool_sample_kernel, n_new=n_new, start=start),
        grid=(1,),
        in_specs=[_const_spec(ext_t.shape), _const_spec(pool_w.shape), _const_spec(pool_scale.shape)],
        out_specs=_const_spec((n_new, bs, d_pool)),
        out_shape=jax.ShapeDtypeStruct((n_new, bs, d_pool), BF16),
        compiler_params=pltpu.CompilerParams(dimension_semantics=("arbitrary",)),
        name="pool_sample",
    )(ext_t, pool_w, pool_scale)


def _conv_silu(window, w_ref, b_ref):
    acc = b_ref[...] + window(0) * w_ref[0:1, :]
    for k in range(1, CONV_W):
        acc = acc + window(k) * w_ref[k:k + 1, :]
    return _silu(acc)


def _softplus(x):
    return jnp.maximum(x, 0.0) + jnp.log1p(jnp.exp(-jnp.abs(x)))


def _ssd_chunk(xact, sz, dt, prm, seq_rows, read_state, write_state):
    (tri_ref, tris_ref, sel64_ref, alog_ref, dexp_ref, normw_ref) = prm
    q = xact.shape[0]
    d_ssm = sz.shape[1]
    gw = d_ssm // N_GROUPS
    n_seq = q // seq_rows
    xs = xact[:, :d_ssm]
    bm = xact[:, d_ssm:d_ssm + N_GROUPS * D_STATE]
    cm = xact[:, d_ssm + N_GROUPS * D_STATE:]

    a = dt * (-jnp.exp(alog_ref[...]) * LOG2_E)
    a3 = _split3(a)
    cum = _sel_left(tri_ref[...], a3)
    rcum = _sel_left(tris_ref[...], a3)
    w = jnp.exp2(rcum) * dt
    cum_t = cum.T
    dt_t = dt.T
    ecum_e = _sel_right(_split3(jnp.exp2(cum))[:2], sel64_ref[...])
    w_e = _sel_right(_split3(w)[:2], sel64_ref[...])
    wx = xs * w_e

    ii = lax.broadcasted_iota(jnp.int32, (q, q), 0)
    jj = lax.broadcasted_iota(jnp.int32, (q, q), 1)
    mask = ii >= jj
    if n_seq > 1:
        mask = jnp.logical_and(mask, (ii // seq_rows) == (jj // seq_rows))
    lo_half = lax.broadcasted_iota(jnp.int32, (q, LANES), 1) < HEAD_DIM
    col = lax.broadcasted_iota(jnp.int32, (gw, q), 1)

    ydiag, yoff = [], []
    for g in range(N_GROUPS):
        bg = bm[:, g * D_STATE:(g + 1) * D_STATE].astype(BF16)
        cg = cm[:, g * D_STATE:(g + 1) * D_STATE].astype(BF16)
        cb = _dot_nt(cg, bg)
        colb = {}
        for pr in range(gw // LANES):
            blk = g * (gw // LANES) + pr
            xp = xs[:, blk * LANES:(blk + 1) * LANES].astype(BF16)
            ys = []
            for half in range(2):
                h = 2 * blk + half
                colb[h] = jnp.broadcast_to(cum[:, h:h + 1], (q, q))
                dec = jnp.exp2(jnp.where(mask, colb[h] - cum_t[h:h + 1, :], -jnp.inf))
                ys.append(_dot((cb * dec * dt_t[h:h + 1, :]).astype(BF16), xp))
            ydiag.append(jnp.where(lo_half, ys[0], ys[1]))
        wx_t = wx[:, g * gw:(g + 1) * gw].T
        yoff_rows = []
        for s in range(n_seq):
            r0 = s * seq_rows
            st = read_state(s, g)
            yoff_rows.append(_dot_nt(cg[r0:r0 + seq_rows, :], st.astype(BF16)))
            last = r0 + seq_rows - 1
            scale = jnp.concatenate(
                [jnp.broadcast_to(jnp.exp2(colb[h][last:last + 1, :]), (HEAD_DIM, D_STATE))
                 for h in range(g * (gw // HEAD_DIM), (g + 1) * (gw // HEAD_DIM))], axis=0)
            wsel = wx_t
            if n_seq > 1:
                wsel = jnp.where(jnp.logical_and(col >= r0, col < r0 + seq_rows), wx_t, 0.0)
            write_state(s, g, st * scale + _dot(wsel.astype(BF16), bg))
        yoff.append(yoff_rows[0] if n_seq == 1 else jnp.concatenate(yoff_rows, axis=0))
    y = jnp.concatenate(ydiag, axis=1) + jnp.concatenate(yoff, axis=1) * ecum_e + xs * dexp_ref[...]
    gz = y * sz
    outs = []
    for g in range(N_GROUPS):
        gg = gz[:, g * gw:(g + 1) * gw]
        ms = jnp.sum(gg * gg, axis=-1, keepdims=True) * (1.0 / gw)
        outs.append(gg * lax.rsqrt(ms + RMS_EPS) * normw_ref[:, g * gw:(g + 1) * gw])
    return jnp.concatenate(outs, axis=1)


def _ssd_prompt_kernel(sz_ref, xact_ref, dt_ref, tri_ref, tris_ref, sel64_ref, alog_ref, dexp_ref, normw_ref,
                       y_ref, hout_ref, h_ref, *, gw):
    c = pl.program_id(1)
    q = sz_ref.shape[0]

    @pl.when(c == 0)
    def _():
        h_ref[...] = jnp.zeros(h_ref.shape, F32)

    def read_state(s, g):
        return h_ref[g * gw:(g + 1) * gw, :]

    def write_state(s, g, v):
        h_ref[g * gw:(g + 1) * gw, :] = v

    prm = (tri_ref, tris_ref, sel64_ref, alog_ref, dexp_ref, normw_ref)
    y_ref[...] = _ssd_chunk(xact_ref[...], sz_ref[...], dt_ref[...], prm, q, read_state, write_state
                            ).astype(y_ref.dtype)

    @pl.when(c == pl.num_programs(1) - 1)
    def _():
        hout_ref[0] = h_ref[...]


def _ssd_sample_kernel(sz_ref, ext_ref, dt_ref, hin_ref, convw_ref, convb_ref,
                       tri_ref, tris_ref, sel64_ref, alog_ref, dexp_ref, normw_ref, *rest, gw, n_new):
    y_ref, hout_ref = rest[-2:]
    q = sz_ref.shape[0]
    n_seq = q // SAMPLE_ROWS
    first = CONV_LEAD - (CONV_W - 1)

    def window(k):
        return jnp.concatenate(
            [ext_ref[s * SAMPLE_EXT_ROWS + first + k:s * SAMPLE_EXT_ROWS + first + k + SAMPLE_ROWS, :]
             for s in range(n_seq)], axis=0)

    xact = _conv_silu(window, convw_ref, convb_ref)
    row = lax.broadcasted_iota(jnp.int32, (q, LANES), 0)
    dt = jnp.where((row % SAMPLE_ROWS) < n_new, dt_ref[...], 0.0)

    def read_state(s, g):
        return hin_ref[s, g * gw:(g + 1) * gw, :]

    def write_state(s, g, v):
        hout_ref[s, g * gw:(g + 1) * gw, :] = v

    prm = (tri_ref, tris_ref, sel64_ref, alog_ref, dexp_ref, normw_ref)
    y_ref[...] = _ssd_chunk(xact, sz_ref[...], dt, prm, SAMPLE_ROWS, read_state, write_state).astype(y_ref.dtype)


def _ssd_consts(seq_rows, n_heads):
    q = CHUNK
    i = jnp.arange(q)[:, None]
    j = jnp.arange(q)[None, :]
    same = (i // seq_rows) == (j // seq_rows)
    tri = jnp.logical_and(same, j <= i).astype(BF16)
    tris = jnp.logical_and(same, j > i).astype(BF16)
    hrow = jnp.arange(LANES)[:, None]
    sel64 = (hrow == (jnp.arange(n_heads * HEAD_DIM)[None, :] // HEAD_DIM)).astype(BF16)
    return tri, tris, sel64


def _ssd_param_specs(prm_arrays):
    return [_const_spec(a.shape) for a in prm_arrays]


def _ssd_prompt(sz, xact, dt, n_seq, seq_len, lp):
    d_ssm = sz.shape[1]
    d_xbc = xact.shape[1]
    q = CHUNK
    nc = seq_len // q
    gw = d_ssm // N_GROUPS
    prm = (*lp["ssd_consts_prompt"], lp["a_log"], lp["d_exp"], lp["norm_w"])
    rows = lambda b, c: (b * nc + c, 0)
    return pl.pallas_call(
        functools.partial(_ssd_prompt_kernel, gw=gw),
        grid=(n_seq, nc),
        in_specs=[pl.BlockSpec((q, d_ssm), rows), pl.BlockSpec((q, d_xbc), rows),
                  pl.BlockSpec((q, LANES), rows)] + _ssd_param_specs(prm),
        out_specs=[pl.BlockSpec((q, d_ssm), rows),
                   pl.BlockSpec((1, d_ssm, D_STATE), lambda b, c: (b, 0, 0))],
        out_shape=[jax.ShapeDtypeStruct((n_seq * seq_len, d_ssm), BF16),
                   jax.ShapeDtypeStruct((n_seq, d_ssm, D_STATE), F32)],
        scratch_shapes=[pltpu.VMEM((d_ssm, D_STATE), F32)],
        compiler_params=pltpu.CompilerParams(dimension_semantics=("arbitrary", "arbitrary"),
                                             vmem_limit_bytes=_vmem_limit(40 * 1024 * 1024)),
        name="ssd_prompt",
    )(sz, xact, dt, *prm)


def _ssd_sample(z16, ext, dtr16, h_all, h_new, layer, lp, n_new):
    d_ssm = z16.shape[1]
    d_xbc = ext.shape[1]
    q = CHUNK
    spc = q // SAMPLE_ROWS
    bs = h_all.shape[1]
    gw = d_ssm // N_GROUPS
    prm = (lp["conv_w"], lp["conv_b"], *lp["ssd_consts_sample"], lp["a_log"], lp["d_exp"], lp["norm_w"])
    rows = lambda i: (i, 0)
    slab =pl.BlockSpec((None, spc, d_ssm, D_STATE), lambda i: (layer, i, 0, 0))
    state_bytes = spc * d_ssm * D_STATE * 4
    prev = [] if h_new is None else [h_new]
    n_in = 4 + len(prm)
    return pl.pallas_call(
        functools.partial(_ssd_sample_kernel, gw=gw, n_new=n_new),
        grid=(bs // spc,),
        in_specs=[pl.BlockSpec((q, d_ssm), rows), pl.BlockSpec((spc * SAMPLE_EXT_ROWS, d_xbc), rows),
                  pl.BlockSpec((q, LANES), rows), slab] + _ssd_param_specs(prm)
                 + [pl.BlockSpec(memory_space=pl.ANY) for _ in prev],
        out_specs=[pl.BlockSpec((q, d_ssm), rows), slab],
        out_shape=[jax.ShapeDtypeStruct((bs * SAMPLE_ROWS, d_ssm), BF16),
                   jax.ShapeDtypeStruct(h_all.shape, F32)],
        input_output_aliases={n_in: 1} if prev else {},
        compiler_params=pltpu.CompilerParams(dimension_semantics=("arbitrary",),
                                             vmem_limit_bytes=_vmem_limit(4 * state_bytes + 20 * 1024 * 1024)),
        name="ssd_sample",
    )(z16, ext, dtr16, h_all, *prm, *prev)


def _route(h, rw_ref, rb_ref, tril_ref, cnt_ref, n_experts):
    xh = h.astype(BF16)
    xl = (h - xh.astype(F32)).astype(BF16)
    w = rw_ref[...]
    wh = w.astype(BF16)
    wl = (w - wh.astype(F32)).astype(BF16)
    logits = _dot(xh, wh) + (_dot(xh, wl) + _dot(xl, wh)) + rb_ref[...]
    lane = lax.broadcasted_iota(jnp.int32, logits.shape, 1)
    lane_f = lane.astype(F32)
    logits = jnp.where(lane < n_experts, logits, -jnp.inf)
    m1 = jnp.max(logits, axis=-1, keepdims=True)
    i1 = jnp.min(jnp.where(logits == m1, lane_f, float(LANES)), axis=-1, keepdims=True)
    rest = jnp.where(lane_f == i1, -jnp.inf, logits)
    m2 = jnp.max(rest, axis=-1, keepdims=True)
    i2 = jnp.min(jnp.where(rest == m2, lane_f, float(LANES)), axis=-1, keepdims=True)
    e2 = jnp.exp(m2 - m1)
    den = 1.0 + e2
    oh1 = jnp.where(lane_f == i1, 1.0, 0.0)
    oh2 = jnp.where(lane_f == i2, 1.0, 0.0)
    before1 = _dot(tril_ref[...], oh1.astype(BF16))
    before2 = _dot(tril_ref[...], oh2.astype(BF16))
    c1 = jnp.sum(oh1, axis=0, keepdims=True)
    c2 = jnp.sum(oh2, axis=0, keepdims=True)
    base = cnt_ref[...]
    r1 = jnp.sum(oh1 * (before1 + base), axis=-1, keepdims=True)
    r2 = jnp.sum(oh2 * (before2 + (base + c1)), axis=-1, keepdims=True)
    cnt_ref[...] = base + (c1 + c2)
    idx = jnp.where(lane == 0, i1, jnp.where(lane == 1, i2, 0.0)).astype(jnp.int32)
    gate = jnp.where(lane == 0, 1.0 / den, jnp.where(lane == 1, e2 / den, 0.0))
    rank = jnp.where(lane == 0, r1, jnp.where(lane == 1, r2, 0.0)).astype(jnp.int32)
    return idx, gate, rank


def _outproj_ln_kernel(*refs, alpha, n_seg, offs, route, n_experts):
    it = iter(refs)
    take = lambda n: [next(it) for _ in range(n)]
    x_refs, p_refs, s_refs = take(n_seg[0]), take(n_seg[1]), take(n_seg[2])
    wp_ref, ws_ref, g_ref, b_ref = take(4)
    x = _seg_read(x_refs, offs[0])
    mixed = _dot(_seg_read(p_refs, offs[1]), wp_ref[...]) + _dot(_seg_read(s_refs, offs[2]), ws_ref[...])
    h = _layer_norm(alpha * x + mixed, g_ref[...], b_ref[...])
    if not route:
        (o_ref,) = take(1)
        o_ref[...] = h
        return
    rw_ref, rb_ref, tril_ref = take(3)
    xt_ref, idx_ref, gate_ref, rank_ref, cnt_ref = take(5)

    @pl.when(pl.program_id(0) == 0)
    def _():
        cnt_ref[...] = jnp.zeros(cnt_ref.shape, F32)

    tm, d = h.shape
    rpt = d // LANES
    for k in range(rpt):
        xt_ref[pl.ds(k, tm, stride=rpt), :] = h[:, k * LANES:(k + 1) * LANES]
    idx, gate, rank = _route(h, rw_ref, rb_ref, tril_ref, cnt_ref, n_experts)
    idx_ref[...] = idx
    gate_ref[...] = gate
    rank_ref[...] = rank


def _outproj_ln(x_segs, pool_segs, ssd_segs, w_pool, w_ssd, g, b, alpha, router=None):
    d = x_segs[0].shape[1]
    rows = [a.shape[0] for a in x_segs + pool_segs + ssd_segs]
    tm = _common_tile(rows, (512, 256, 128, 64))
    groups = (x_segs, pool_segs, ssd_segs)
    tiles = [_seg_tiles(s, tm) for s in groups]
    t = sum(tiles[0]) * tm
    assert all(sum(ts) * tm == t for ts in tiles)
    in_specs = [sp for s in groups for sp in _seg_specs(s, tm)]
    consts = [w_pool, w_ssd, g, b]
    row = lambda w, dt: (pl.BlockSpec((tm, w), lambda i: (i, 0)), jax.ShapeDtypeStruct((t, w), dt))
    if router is None:
        outs = [row(d, F32)]
        n_experts = 0
    else:
        rw_pad, rb_pad, n_experts = router
        tril = (jnp.arange(tm)[:, None] > jnp.arange(tm)[None, :]).astype(BF16)
        consts += [rw_pad, rb_pad, tril]
        rpt = d // LANES
        outs = [(pl.BlockSpec((tm * rpt, LANES), lambda i: (i, 0)), jax.ShapeDtypeStruct((t * rpt, LANES), F32)),
                row(LANES, jnp.int32), row(LANES, F32), row(LANES, jnp.int32),
                (_const_spec((1, LANES)), jax.ShapeDtypeStruct((1, LANES), F32))]
    in_specs += [_const_spec(c.shape) for c in consts]
    res = pl.pallas_call(
        functools.partial(_outproj_ln_kernel, alpha=alpha, n_seg=[len(s) for s in groups],
                          offs=[_seg_offsets(ts) for ts in tiles], route=router is not None, n_experts=n_experts),
        grid=(t // tm,),
        in_specs=in_specs,
        out_specs=[o[0] for o in outs],
        out_shape=[o[1] for o in outs],
        compiler_params=pltpu.CompilerParams(dimension_semantics=("arbitrary",),
                                             vmem_limit_bytes=_vmem_limit(40 * 1024 * 1024)),
        name="outproj_ln",
    )(*x_segs, *pool_segs, *ssd_segs, *consts)
    return res[0] if router is None else res


def _swiglu(xb, wg_ref, wu_ref, wd_ref):
    act = (_silu(_dot(xb, wg_ref[0])) * _dot(xb, wu_ref[0])).astype(BF16)
    return _dot(act, wd_ref[0])


def _dense_ffn_ln_kernel(x_ref, wg_ref, wu_ref, wd_ref, g_ref, b_ref, *out_refs, alpha, offs, tiles):
    x = x_ref[...]
    f = _swiglu(x.astype(BF16), wg_ref, wu_ref, wd_ref)
    _seg_write(out_refs, offs, tiles, _layer_norm(alpha * x + f, g_ref[...], b_ref[...]))


def _dense_ffn_ln(x, wg, wu, wd, g, b, alpha, out_rows):
    t, d = x.shape
    ff = wg.shape[2]
    tm = _common_tile(out_rows, (256, 128, 64))
    tiles = [n // tm for n in out_rows]
    offs = _seg_offsets(tiles)
    vmem = 2 * 3 * d * ff * 2 + 4 * tm * d * 4 + tm * ff * 12
    res = pl.pallas_call(
        functools.partial(_dense_ffn_ln_kernel, alpha=alpha, offs=offs, tiles=tiles),
        grid=(t // tm,),
        in_specs=[pl.BlockSpec((tm, d), lambda i: (i, 0)), _const_spec(wg.shape), _const_spec(wu.shape),
                  _const_spec(wd.shape), _const_spec(g.shape), _const_spec(b.shape)],
        out_specs=[pl.BlockSpec((tm, d), lambda i, o=o, n=n: (jnp.clip(i - o, 0, n - 1), 0))
                   for o, n in zip(offs, tiles)],
        out_shape=[jax.ShapeDtypeStruct((n, d), F32) for n in out_rows],
        compiler_params=pltpu.CompilerParams(dimension_semantics=("arbitrary",),
                                             vmem_limit_bytes=_vmem_limit(vmem)),
        name="dense_ffn_ln",
    )(x, wg, wu, wd, g, b)
    return list(res)


def _dispatch_kernel(ends_ref, slot_ref, x_ref, out_hbm, zeros_ref, sem, zsem, *, tokens, rpt, tile, n_experts):
    i = pl.program_id(0)

    def tail_copy(e):
        start = pl.multiple_of((ends_ref[e] - tile) * rpt, tile * rpt)
        return pltpu.make_async_copy(zeros_ref, out_hbm.at[pl.ds(start, tile * rpt)], zsem)

    def nonempty(e):
        return ends_ref[e] > (ends_ref[e - 1] if e > 0 else 0)

    def unused_copy(j):
        start = pl.multiple_of((ends_ref[n_experts - 1] + j * tile) * rpt, tile * rpt)
        return pltpu.make_async_copy(zeros_ref, out_hbm.at[pl.ds(start, tile * rpt)], zsem)

    def unused(j):
        return ends_ref[n_experts - 1] + (j + 1) * tile <= out_hbm.shape[0] // rpt

    @pl.when(i == 0)
    def _():
        zeros_ref[...] = jnp.zeros(zeros_ref.shape, zeros_ref.dtype)
        for e in range(n_experts):
            @pl.when(nonempty(e))
            def _(e=e):
                tail_copy(e).start()

            @pl.when(unused(e))
            def _(e=e):
                unused_copy(e).start()
        for e in range(n_experts):
            @pl.when(nonempty(e))
            def _(e=e):
                tail_copy(e).wait()

            @pl.when(unused(e))
            def _(e=e):
                unused_copy(e).wait()

    def issue(r, carry):
        src = x_ref.at[pl.ds(pl.multiple_of(r * rpt, rpt), rpt)]
        for k in range(TOP_K):
            dst = out_hbm.at[pl.ds(pl.multiple_of(slot_ref[0, 0, k * tokens + r] * rpt, rpt), rpt)]
            pltpu.make_async_copy(src, dst, sem).start()
        return carry

    lax.fori_loop(0, tokens, issue, 0)
    for k in range(TOP_K):
        pltpu.make_async_copy(x_ref, out_hbm.at[pl.ds(0, tokens * rpt)], sem).wait()


def _dispatch(xt, slots, ends, n_slots, tokens, rpt, n_experts):
    steps = slots.shape[0]
    return pl.pallas_call(
        functools.partial(_dispatch_kernel, tokens=tokens, rpt=rpt, tile=MOE_TILE, n_experts=n_experts),
        grid_spec=pltpu.PrefetchScalarGridSpec(
            num_scalar_prefetch=1, grid=(steps,),
            in_specs=[pl.BlockSpec((1, 1, TOP_K * tokens), lambda i, e: (i, 0, 0), memory_space=pltpu.SMEM),
                      pl.BlockSpec((tokens * rpt, LANES), lambda i, e: (i, 0))],
            out_specs=pl.BlockSpec(memory_space=pl.ANY),
            scratch_shapes=[pltpu.VMEM((MOE_TILE * rpt, LANES), xt.dtype),
                            pltpu.SemaphoreType.DMA(()), pltpu.SemaphoreType.DMA(())]),
        out_shape=jax.ShapeDtypeStruct((n_slots * rpt, LANES), xt.dtype),
        compiler_params=pltpu.CompilerParams(dimension_semantics=("arbitrary",)),
        name="moe_dispatch",
    )(ends, slots, xt)


def _moe_ffn_kernel(te_ref, na_ref, x_ref, wg_ref, wu_ref, wd_ref, o_ref, *, tm, rpt):
    active = pl.program_id(0) < na_ref[0]

    @pl.when(active)
    def _():
        xb = jnp.concatenate([x_ref[pl.ds(k, tm, stride=rpt), :] for k in range(rpt)], axis=1).astype(BF16)
        f = _swiglu(xb, wg_ref, wu_ref, wd_ref)
        for k in range(rpt):
            o_ref[pl.ds(k, tm, stride=rpt), :] = f[:, k * LANES:(k + 1) * LANES]

    @pl.when(jnp.logical_not(active))
    def _():
        o_ref[...] = jnp.zeros(o_ref.shape, o_ref.dtype)


def _moe_ffn(xs, wg, wu, wd, tile_expert, n_active, rpt):
    tm = MOE_TILE
    d, ff = wg.shape[1], wg.shape[2]
    n_tiles = xs.shape[0] // (tm * rpt)
    w_idx = lambda i, te, na: (te[i], 0, 0)
    vmem = 2 * 3 * d * ff * 2 + 4 * tm * d * 4 + tm * ff * 12 + 2 * tm * d * 4
    return pl.pallas_call(
        functools.partial(_moe_ffn_kernel, tm=tm, rpt=rpt),
        grid_spec=pltpu.PrefetchScalarGridSpec(
            num_scalar_prefetch=2, grid=(n_tiles,),
            in_specs=[pl.BlockSpec((tm * rpt, LANES), lambda i, te, na: (jnp.minimum(i, na[0] - 1), 0)),
                      pl.BlockSpec((1, d, ff), w_idx), pl.BlockSpec((1, d, ff), w_idx),
                      pl.BlockSpec((1, ff, d), w_idx)],
            out_specs=pl.BlockSpec((tm * rpt, LANES), lambda i, te, na: (i, 0))),
        out_shape=jax.ShapeDtypeStruct(xs.shape, F32),
        compiler_params=pltpu.CompilerParams(dimension_semantics=("arbitrary",),
                                             vmem_limit_bytes=_vmem_limit(vmem)),
        name="moe_ffn",
    )(tile_expert, n_active, xs, wg, wu, wd)


def _combine_ln_kernel(slot_ref, next_ref, x_ref, gate_ref, y_hbm, g_ref, b_ref, *rest, alpha, tm, rpt, offs, tiles):
    out_refs, (buf, sem) = rest[:len(tiles)], rest[len(tiles):]
    i = pl.program_id(0)
    n_rows = TOP_K * tm * rpt

    def issue(s_ref, slot):
        def body(r, carry):
            src = y_hbm.at[pl.ds(pl.multiple_of(s_ref[0, 0, r] * rpt, rpt), rpt)]
            pltpu.make_async_copy(src, buf.at[slot, pl.ds(pl.multiple_of(r * rpt, rpt), rpt)], sem.at[slot]).start()
            return carry
        lax.fori_loop(0, TOP_K * tm, body, 0)

    @pl.when(i == 0)
    def _():
        issue(slot_ref, 0)

    @pl.when(i + 1 < pl.num_programs(0))
    def _():
        issue(next_ref, (i + 1) % 2)

    cur = i % 2
    pltpu.make_async_copy(y_hbm.at[pl.ds(0, n_rows)], buf.at[cur], sem.at[cur]).wait()
    g1 = gate_ref[:, 0:1]
    g2 = gate_ref[:, 1:2]
    cols = []
    for k in range(rpt):
        xk = x_ref[pl.ds(k, tm, stride=rpt), :]
        ya = buf[cur, pl.ds(k, tm, stride=rpt), :]
        yb = buf[cur, pl.ds(tm * rpt + k, tm, stride=rpt), :]
        cols.append(alpha * xk + (g1 * ya + g2 * yb))
    h = jnp.concatenate(cols, axis=1)
    _seg_write(out_refs, offs, tiles, _layer_norm(h, g_ref[...], b_ref[...]))


def _combine_ln(xt, gate, ys, slots, g, b, alpha, tm, rpt, out_rows):
    d = g.shape[1]
    tiles = [n // tm for n in out_rows]
    offs = _seg_offsets(tiles)
    steps = slots.shape[0]
    slot_spec = lambda f: pl.BlockSpec((1, 1, TOP_K * tm), f, memory_space=pltpu.SMEM)
    res = pl.pallas_call(
        functools.partial(_combine_ln_kernel, alpha=alpha, tm=tm, rpt=rpt, offs=offs, tiles=tiles),
        grid=(steps,),
        in_specs=[slot_spec(lambda i: (i, 0, 0)), slot_spec(lambda i: (jnp.minimum(i + 1, steps - 1), 0, 0)),
                  pl.BlockSpec((tm * rpt, LANES), lambda i: (i, 0)), pl.BlockSpec((tm, LANES), lambda i: (i, 0)),
                  pl.BlockSpec(memory_space=pl.ANY), _const_spec(g.shape), _const_spec(b.shape)],
        out_specs=[pl.BlockSpec((tm, d), lambda i, o=o, n=n: (jnp.clip(i - o, 0, n - 1), 0))
                   for o, n in zip(offs, tiles)],
        out_shape=[jax.ShapeDtypeStruct((n, d), F32) for n in out_rows],
        scratch_shapes=[pltpu.VMEM((2, TOP_K * tm * rpt, LANES), F32), pltpu.SemaphoreType.DMA((2,))],
        compiler_params=pltpu.CompilerParams(dimension_semantics=("arbitrary",),
                                             vmem_limit_bytes=_vmem_limit(32 * 1024 * 1024)),
        name="moe_combine_ln",
    )(slots, slots, xt, gate, ys, g, b)
    return list(res)


def _moe_ffn_ln(xt, idx, gate, rank, counts, wg, wu, wd, g, b, alpha, tm, out_rows):
    n_experts = wg.shape[0]
    d = wg.shape[1]
    rpt = d // LANES
    t = idx.shape[0]
    tile = MOE_TILE
    cnt = counts[0, :n_experts].astype(jnp.int32)
    padded = ((cnt + tile - 1) // tile) * tile
    ends = jnp.cumsum(padded).astype(jnp.int32)
    starts = ends - padded
    slot = jnp.take(starts, idx[:, :TOP_K]) + rank[:, :TOP_K]
    n_slots = -(-(TOP_K * t + n_experts * (tile - 1)) // tile) * tile
    n_tiles = n_slots // tile
    n_active = ends[-1:] // tile
    tile_start = jnp.minimum(jnp.arange(n_tiles, dtype=jnp.int32), n_active[0] - 1) * tile
    tile_expert = jnp.sum((tile_start[:, None] >= ends[None, :]).astype(jnp.int32), axis=1)
    slots = jnp.swapaxes(slot.reshape(t // tm, tm, TOP_K), 1, 2).reshape(t // tm, 1, TOP_K * tm)
    xs = _dispatch(xt, slots, ends, n_slots, tm, rpt, n_experts)
    ys = _moe_ffn(xs, wg, wu, wd, tile_expert, n_active, rpt)
    return _combine_ln(xt, gate, ys, slots, g, b, alpha, tm, rpt, out_rows)


def kernel(x_prompt, x_sample, state_pool, state_conv, state_ssm, w_in, conv_w, conv_b, dt_bias, A_log, D_skip,
           ssm_norm_w, pool_w, pool_scale, w_out, ln1_g, ln1_b, ln2_g, ln2_b, ffn_w_gate, ffn_w_up, ffn_w_down,
           router_w, router_b, moe_w_gate, moe_w_up, moe_w_down):
    bp, seq, d = x_prompt.shape
    bs, n_new, _ = x_sample.shape
    depth = w_in.shape[0]
    d_pool = pool_scale.shape[1]
    d_ssm = ssm_norm_w.shape[1]
    d_xbc = conv_w.shape[2]
    n_heads = dt_bias.shape[1]
    tp, ts = bp * seq, bs * n_new
    alpha = (2.0 * depth) ** 0.25
    assert d_pool == POOL_GROUP * len(POOL_WINDOWS) and d_ssm == n_heads * HEAD_DIM
    assert d_xbc == d_ssm + 2 * N_GROUPS * D_STATE and seq % CHUNK == 0 and n_new <= CONV_W
    assert n_heads <= LANES and bs % (CHUNK // SAMPLE_ROWS) == 0 and d % LANES == 0

    consts_prompt = _ssd_consts(CHUNK, n_heads)
    consts_sample = _ssd_consts(SAMPLE_ROWS, n_heads)
    x_segs = [x_prompt.reshape(tp, d), x_sample.reshape(ts, d)]
    new_pool_p, new_conv_p, new_ssm_p, new_pool_s, new_conv_s = [], [], [], [], []
    ssm_s = None
    for l in range(depth):
        out_rows = [tp, ts]
        n_in = w_in.shape[2]
        w_in_pad = jnp.zeros((d, d_pool + d_ssm + d_xbc + LANES), BF16).at[:, :n_in].set(w_in[l].astype(BF16))
        lp = dict(
            conv_w=conv_w[l], conv_b=conv_b[l][None, :],
            dt_bias=jnp.zeros((1, LANES), F32).at[0, :n_heads].set(dt_bias[l]),
            a_log=jnp.zeros((1, LANES), F32).at[0, :n_heads].set(A_log[l]),
            d_exp=jnp.repeat(D_skip[l], HEAD_DIM)[None, :], norm_w=ssm_norm_w[l][None, :],
            ssd_consts_prompt=consts_prompt, ssd_consts_sample=consts_sample)
        pw = pool_w[l].astype(BF16)
        ps = pool_scale[l][None, :]

        proj = (w_in_pad, lp["conv_w"], lp["conv_b"], lp["dt_bias"], d_pool, d_ssm, d_xbc)

        u_p, sz_p, xact_p, dt_p, c_p = _in_proj(x_segs[0], *proj, seq_len=seq)
        pool_p, buf_p = _pool_prompt(u_p, tp, seq, pw, ps)
        y_p, h_p = _ssd_prompt(sz_p, xact_p, dt_p, bp, seq, lp)
        new_pool_p.append(buf_p)
        new_conv_p.append(c_p)
        new_ssm_p.append(h_p.reshape(bp, n_heads, HEAD_DIM, D_STATE))

        u_s, sz_s, xbc_s, dt_s = _in_proj(x_segs[1], *proj)
        u_s = u_s.reshape(bs, n_new, d_pool)
        xbc_s = xbc_s.reshape(bs, n_new, d_xbc)
        pool_ext = jnp.concatenate([state_pool[l], u_s], axis=1)
        pool_s = _pool_sample(jnp.swapaxes(pool_ext, 0, 1), pw, ps, n_new, PAST_LEN)
        pool_s = jnp.swapaxes(pool_s, 0, 1).reshape(ts, d_pool)
        conv_ext = jnp.concatenate([state_conv[l], xbc_s], axis=1)
        lead = CONV_LEAD - (CONV_W - 1)
        ext = jnp.pad(conv_ext, ((0, 0), (lead, SAMPLE_EXT_ROWS - lead - conv_ext.shape[1]), (0, 0)))
        pad_rows = lambda a: jnp.pad(a.reshape(bs, n_new, -1), ((0, 0), (0, SAMPLE_ROWS - n_new), (0, 0))
                                     ).reshape(bs * SAMPLE_ROWS, -1)
        y_s16, ssm_s = _ssd_sample(pad_rows(sz_s), ext.reshape(bs * SAMPLE_EXT_ROWS, d_xbc), pad_rows(dt_s),
                                   state_ssm.reshape(depth, bs, d_ssm, D_STATE), ssm_s, l, lp, n_new)
        y_s = y_s16.reshape(bs, SAMPLE_ROWS, d_ssm)[:, :n_new].reshape(ts, d_ssm)
        new_pool_s.append(pool_ext[:, n_new:])
        new_conv_s.append(conv_ext[:, n_new:])

        w_o = w_out[l].astype(BF16)
        ln1 = (ln1_g[l][None, :], ln1_b[l][None, :])
        g2, b2 = ln2_g[l][None, :], ln2_b[l][None, :]
        j = l // 2
        if l % 2 == 0:
            x1 = _outproj_ln(x_segs, [pool_p, pool_s], [y_p, y_s], w_o[:d_pool], w_o[d_pool:], *ln1, alpha)
            x_segs = _dense_ffn_ln(x1, ffn_w_gate[j:j + 1].astype(BF16), ffn_w_up[j:j + 1].astype(BF16),
                                   ffn_w_down[j:j + 1].astype(BF16), g2, b2, alpha, out_rows)
        else:
            n_experts = router_w.shape[2]
            rw_pad = jnp.zeros((d, LANES), F32).at[:, :n_experts].set(router_w[j])
            rb_pad = jnp.zeros((1, LANES), F32).at[0, :n_experts].set(router_b[j])
            xt, idx, gate, rank, counts = _outproj_ln(x_segs, [pool_p, pool_s], [y_p, y_s], w_o[:d_pool],
                                                      w_o[d_pool:], *ln1, alpha, router=(rw_pad, rb_pad, n_experts))
            tm = _common_tile([tp, ts], (512, 256, 128, 64))
            x_segs = _moe_ffn_ln(xt, idx, gate, rank, counts, moe_w_gate[j].astype(BF16), moe_w_up[j].astype(BF16),
                                 moe_w_down[j].astype(BF16), g2, b2, alpha, tm, out_rows)

    y_prompt, y_sample = x_segs
    return (y_prompt.reshape(bp, seq, d), y_sample.reshape(bs, n_new, d),
            jnp.stack(new_pool_p), jnp.stack(new_conv_p), jnp.stack(new_ssm_p),
            jnp.stack(new_pool_s), jnp.stack(new_conv_s),
            ssm_s.reshape(depth, bs, n_heads, HEAD_DIM, D_STATE))
```

```python
import functools

import jax
import jax.numpy as jnp
from jax import lax
from jax.experimental import pallas as pl
from jax.experimental.pallas import tpu as pltpu

F32 = jnp.float32
BF16 = jnp.bfloat16

PAST_LEN = 16384
POOL_WINDOWS = (2, 4, 8, 16)
POOL_GROUP = 128
POOL_BUF = max(POOL_WINDOWS) - 1
HEAD_DIM = 64
N_GROUPS = 4
D_STATE = 128
CONV_W = 4
CHUNK = 128
TOP_K = 2
LN_EPS = 1e-5
RMS_EPS = 1e-6
LOG2_E = 1.4426950408889634

LANES = 128
SUBLANES = 8
V7X_VMEM_BYTES = 64 * 1024 * 1024
VMEM_CAP = V7X_VMEM_BYTES - 8 * 1024 * 1024

SAMPLE_ROWS = 16
SAMPLE_EXT_ROWS = 24
CONV_LEAD = 8
MOE_TILE = 256
SSD_WAYS = 2


def _vmem_limit(nbytes):
    return int(min(VMEM_CAP, nbytes + 6 * 1024 * 1024))


def _dot(a, b):
    return jnp.dot(a, b, preferred_element_type=F32)


def _dot_nt(a, b):
    return lax.dot_general(a, b, (((1,), (1,)), ((), ())), preferred_element_type=F32)


def _split3(v):
    hi = v.astype(BF16)
    r = v - hi.astype(F32)
    mid = r.astype(BF16)
    lo = (r - mid.astype(F32)).astype(BF16)
    return hi, mid, lo


def _sel_right(parts, m):
    out = _dot(parts[0], m)
    for p in parts[1:]:
        out = out + _dot(p, m)
    return out


def _sel_left(m, parts):
    out = _dot(m, parts[0])
    for p in parts[1:]:
        out = out + _dot(m, p)
    return out


def _silu(x):
    return x / (1.0 + jnp.exp2(x * (-LOG2_E)))


def _layer_norm(h, g, b):
    mu = jnp.mean(h, axis=-1, keepdims=True)
    d = h - mu
    var = jnp.mean(d * d, axis=-1, keepdims=True)
    return d * lax.rsqrt(var + LN_EPS) * g + b


def _pick_tile(n, candidates):
    for c in candidates:
        if n % c == 0:
            return c
    raise ValueError(f"no tile in {candidates} divides {n}")


def _const_spec(shape):
    nd = len(shape)
    return pl.BlockSpec(shape, lambda *_: (0,) * nd)


def _seg_tiles(segs, tm, rows_per_token=1):
    return [a.shape[0] // (tm * rows_per_token) for a in segs]


def _seg_offsets(tiles):
    offs, off = [], 0
    for n in tiles:
        offs.append(off)
        off += n
    return offs


def _seg_specs(segs, tm, rows_per_token=1):
    tiles = _seg_tiles(segs, tm, rows_per_token)
    return [pl.BlockSpec((tm * rows_per_token, a.shape[1]),
                         lambda i, *_, o=o, n=n: (jnp.clip(i - o, 0, n - 1), 0))
            for a, o, n in zip(segs, _seg_offsets(tiles), tiles)]


def _seg_read(refs, offs):
    v = refs[0][...]
    for r, o in zip(refs[1:], offs[1:]):
        v = jnp.where(pl.program_id(0) >= o, r[...], v)
    return v


def _seg_write(refs, offs, tiles, v):
    i = pl.program_id(0)
    for r, o, n in zip(refs, offs, tiles):
        @pl.when(jnp.logical_and(i >= o, i < o + n))
        def _(r=r):
            r[...] = v.astype(r.dtype)


def _common_tile(row_counts, candidates):
    for c in candidates:
        if all(n % c == 0 for n in row_counts):
            return c
    raise ValueError(f"no tile in {candidates} divides all of {row_counts}")


INPROJ_CHUNK = 512


def _inproj_kernel(x_ref, w_ref, convw_ref, convb_ref, dtb_ref, u_ref, sz_ref, xo_ref, dt_ref, *rest,
                   splits, tiles_per_seq):
    (u0, u1), (z0, z1), (c0, c1), (d0, d1) = splits
    tm = x_ref.shape[0]
    xb = x_ref[...].astype(BF16)
    u_ref[...] = _dot(xb, w_ref[:, u0:u1])
    dt_ref[...] = _softplus(_dot(xb, w_ref[:, d0:d1]) + dtb_ref[...])
    for lo in range(z0, z1, INPROJ_CHUNK):
        sz_ref[:, lo - z0:lo - z0 + INPROJ_CHUNK] = _silu(_dot(xb, w_ref[:, lo:lo + INPROJ_CHUNK]))
    if tiles_per_seq is None:
        for lo in range(c0, c1, INPROJ_CHUNK):
            xo_ref[:, lo - c0:lo - c0 + INPROJ_CHUNK] = _dot(xb, w_ref[:, lo:lo + INPROJ_CHUNK])
        return

    cstate_ref, ext_ref = rest
    pos = pl.program_id(0) % tiles_per_seq

    @pl.when(pos == 0)
    def _():
        ext_ref[0:CONV_LEAD, :] = jnp.zeros((CONV_LEAD, ext_ref.shape[1]), F32)

    @pl.when(pos > 0)
    def _():
        ext_ref[0:CONV_LEAD, :] = ext_ref[tm:tm + CONV_LEAD, :]

    first = CONV_LEAD - (CONV_W - 1)
    for lo in range(c0, c1, INPROJ_CHUNK):
        cols = slice(lo - c0, lo - c0 + INPROJ_CHUNK)
        ext_ref[CONV_LEAD:CONV_LEAD + tm, cols] = _dot(xb, w_ref[:, lo:lo + INPROJ_CHUNK])
        rows = ext_ref[:, cols]
        acc = convb_ref[:, cols] + (pltpu.roll(rows, CONV_W - 1, axis=0)[CONV_LEAD:] * convw_ref[0:1, cols])
        for k in range(1, CONV_W - 1):
            acc = acc + pltpu.roll(rows, CONV_W - 1 - k, axis=0)[CONV_LEAD:] * convw_ref[k:k + 1, cols]
        acc = acc + rows[CONV_LEAD:] * convw_ref[CONV_W - 1:CONV_W, cols]
        xo_ref[:, cols] = _silu(acc)

    @pl.when(pos == tiles_per_seq - 1)
    def _():
        cstate_ref[0] = ext_ref[CONV_LEAD + tm - (CONV_W - 1):CONV_LEAD + tm, :]


def _in_proj(x, w_pad, conv_w, conv_b, dt_bias, d_pool, d_ssm, d_xbc, seq_len=None):
    t, d = x.shape
    n = w_pad.shape[1]
    tm = _pick_tile(t if seq_len is None else seq_len, (512, 256, 128, 64))
    splits = ((0, d_pool), (d_pool, d_pool + d_ssm), (d_pool + d_ssm, d_pool + d_ssm + d_xbc),
              (d_pool + d_ssm + d_xbc, n))
    assert d_ssm % INPROJ_CHUNK == 0 and d_xbc % INPROJ_CHUNK == 0
    widths = [hi - lo for lo, hi in splits]
    out_specs = [pl.BlockSpec((tm, w), lambda i: (i, 0)) for w in widths]
    out_shape = [jax.ShapeDtypeStruct((t, w), F32) for w in widths]
    scratch = []
    tiles_per_seq = None
    if seq_len is not None:
        tiles_per_seq = seq_len // tm
        out_specs.append(pl.BlockSpec((1, CONV_W - 1, d_xbc), lambda i: (i // tiles_per_seq, 0, 0)))
        out_shape.append(jax.ShapeDtypeStruct((t // seq_len, CONV_W - 1, d_xbc), F32))
        scratch.append(pltpu.VMEM((tm + CONV_LEAD, d_xbc), F32))
    vmem = 2 * tm * d * 4 + 2 * d * n * 2 + 2 * tm * n * 4 + 2 * tm * d_xbc * 4 + 4 * tm * INPROJ_CHUNK * 4
    return pl.pallas_call(
        functools.partial(_inproj_kernel, splits=splits, tiles_per_seq=tiles_per_seq),
        grid=(t // tm,),
        in_specs=[pl.BlockSpec((tm, d), lambda i: (i, 0)), _const_spec((d, n)), _const_spec(conv_w.shape),
                  _const_spec(conv_b.shape), _const_spec(dt_bias.shape)],
        out_specs=out_specs,
        out_shape=out_shape,
        scratch_shapes=scratch,
        compiler_params=pltpu.CompilerParams(dimension_semantics=("arbitrary",),
                                             vmem_limit_bytes=_vmem_limit(vmem)),
        name="in_proj",
    )(x, w_pad, conv_w, conv_b, dt_bias)


def _pool_prompt_kernel(u_ref, w_ref, scale_ref, o_ref, buf_ref, ext_ref, *, tl):
    j = pl.program_id(1)

    @pl.when(j == pl.num_programs(1) - 1)
    def _():
        buf_ref[0] = u_ref[tl - POOL_BUF:tl, :]

    @pl.when(j == 0)
    def _():
        ext_ref[0:16, :] = jnp.zeros((16, ext_ref.shape[1]), F32)

    @pl.when(j > 0)
    def _():
        ext_ref[0:16, :] = ext_ref[tl:tl + 16, :]

    ext_ref[16:16 + tl, :] = u_ref[...]
    pos = j * tl + lax.broadcasted_iota(jnp.int32, (tl, POOL_GROUP), 0)
    for g, win in enumerate(POOL_WINDOWS):
        lanes = slice(g * POOL_GROUP, (g + 1) * POOL_GROUP)
        cur = ext_ref[16:16 + tl, lanes]
        acc = cur
        for k in range(1, win):
            acc = acc + ext_ref[16 - k:16 - k + tl, lanes]
        cnt = jnp.minimum(pos + 1, win).astype(F32)
        diff = (acc / cnt - cur).astype(BF16)
        out = _dot(diff, w_ref[g]) * scale_ref[:, lanes]
        o_ref[:, lanes] = out.astype(o_ref.dtype)


def _pool_prompt(u, n_rows, seq_len, pool_w, pool_scale):
    d_pool = u.shape[1]
    tl = _pick_tile(seq_len, (512, 256, 128))
    nj = seq_len // tl
    nb = n_rows // seq_len
    return pl.pallas_call(
        functools.partial(_pool_prompt_kernel, tl=tl),
        grid=(nb, nj),
        in_specs=[pl.BlockSpec((tl, d_pool), lambda b, j: (b * nj + j, 0)),
                  _const_spec(pool_w.shape), _const_spec(pool_scale.shape)],
        out_specs=[pl.BlockSpec((tl, d_pool), lambda b, j: (b * nj + j, 0)),
                   pl.BlockSpec((1, POOL_BUF, d_pool), lambda b, j: (b, 0, 0))],
        out_shape=[jax.ShapeDtypeStruct((n_rows, d_pool), BF16),
                   jax.ShapeDtypeStruct((nb, POOL_BUF, d_pool), F32)],
        scratch_shapes=[pltpu.VMEM((tl + 16, d_pool), F32)],
        compiler_params=pltpu.CompilerParams(dimension_semantics=("arbitrary", "arbitrary")),
        name="pool_prompt",
    )(u, pool_w, pool_scale)


def _pool_sample_kernel(ext_ref, w_ref, scale_ref, o_ref, *, n_new, start):
    for t in range(n_new):
        for g, win in enumerate(POOL_WINDOWS):
            lanes = slice(g * POOL_GROUP, (g + 1) * POOL_GROUP)
            cur = ext_ref[POOL_BUF + t, :, lanes]
            acc = cur
            for k in range(1, win):
                acc = acc + ext_ref[POOL_BUF + t - k, :, lanes]
            cnt = float(min(start + t + 1, win))
            diff = (acc / cnt - cur).astype(BF16)
            out = _dot(diff, w_ref[g]) * scale_ref[:, lanes]
            o_ref[t, :, lanes] = out.astype(o_ref.dtype)


def _pool_sample(ext_t, pool_w, pool_scale, n_new, start):
    rows, bs, d_pool = ext_t.shape
    return pl.pallas_call(
        functools.partial(_pool_sample_kernel, n_new=n_new, start=start),
        grid=(1,),
        in_specs=[_const_spec(ext_t.shape), _const_spec(pool_w.shape), _const_spec(pool_scale.shape)],
        out_specs=_const_spec((n_new, bs, d_pool)),
        out_shape=jax.ShapeDtypeStruct((n_new, bs, d_pool), BF16),
        compiler_params=pltpu.CompilerParams(dimension_semantics=("arbitrary",)),
        name="pool_sample",
    )(ext_t, pool_w, pool_scale)


def _conv_silu(window, w_ref, b_ref):
    acc = b_ref[...] + window(0) * w_ref[0:1, :]
    for k in range(1, CONV_W):
        acc = acc + window(k) * w_ref[k:k + 1, :]
    return _silu(acc)


def _softplus(x):
    return jnp.maximum(x, 0.0) + jnp.log1p(jnp.exp(-jnp.abs(x)))


def _ssd_chunk(xact, sz, dt, prm, seq_rows, read_state, write_state, write_out):
    (cums_ref, spread_ref, alog_ref, dexp_ref, normw_ref) = prm
    q = xact.shape[0]
    d_ssm = sz.shape[1]
    gw = d_ssm // N_GROUPS
    n_seq = q // seq_rows
    xs = xact[:, :d_ssm]
    bm = xact[:, d_ssm:d_ssm + N_GROUPS * D_STATE]
    cm = xact[:, d_ssm + N_GROUPS * D_STATE:]

    a = dt * (-jnp.exp(alog_ref[...]) * LOG2_E)
    sums = _dot(cums_ref[...], jnp.concatenate(_split3(a), axis=0))
    cum, rcum = sums[:q], sums[q:]
    w = jnp.exp2(rcum) * dt
    cum_t = cum.T
    dt_t = dt.T
    two_terms = lambda v: jnp.concatenate(_split3(v)[:2], axis=1)
    spread = _dot(jnp.concatenate([two_terms(w), two_terms(jnp.exp2(cum))], axis=0), spread_ref[...])
    wx = xs * spread[:q]
    ecum_e = spread[q:]

    ii = lax.broadcasted_iota(jnp.int32, (q, q), 0)
    jj = lax.broadcasted_iota(jnp.int32, (q, q), 1)
    mask = ii >= jj
    if n_seq > 1:
        mask = jnp.logical_and(mask, (ii // seq_rows) == (jj // seq_rows))
    lo_half = lax.broadcasted_iota(jnp.int32, (q, LANES), 1) < HEAD_DIM
    col = lax.broadcasted_iota(jnp.int32, (gw, q), 1)

    for g in range(N_GROUPS):
        gcols = slice(g * gw, (g + 1) * gw)
        bg = bm[:, g * D_STATE:(g + 1) * D_STATE].astype(BF16)
        cg = cm[:, g * D_STATE:(g + 1) * D_STATE].astype(BF16)
        cb = _dot_nt(cg, bg)
        ydiag = []
        end_decay = {}
        for pr in range(gw // LANES):
            blk = g * (gw // LANES) + pr
            xp = xs[:, blk * LANES:(blk + 1) * LANES]
            mix = []
            for half in range(2):
                h = 2 * blk + half
                colb = jnp.broadcast_to(cum[:, h:h + 1], (q, q))
                for s in range(n_seq):
                    last = (s + 1) * seq_rows - 1
                    end_decay[h, s] = jnp.exp2(colb[last:last + 1, :])
                dec = jnp.exp2(jnp.where(mask, colb - cum_t[h:h + 1, :], -jnp.inf))
                mix.append((cb * dec * dt_t[h:h + 1, :]).astype(BF16))
            x2 = jnp.concatenate([jnp.where(lo_half, xp, 0.0), jnp.where(lo_half, 0.0, xp)], axis=0).astype(BF16)
            ydiag.append(_dot(jnp.concatenate(mix, axis=1), x2))
        wx_t = wx[:, gcols].T
        yoff_rows = []
        for s in range(n_seq):
            r0 = s * seq_rows
            st = read_state(s, g)
            yoff_rows.append(_dot_nt(cg[r0:r0 + seq_rows, :], st.astype(BF16)))
            scale = jnp.concatenate(
                [jnp.broadcast_to(end_decay[h, s], (HEAD_DIM, D_STATE))
                 for h in range(g * (gw // HEAD_DIM), (g + 1) * (gw // HEAD_DIM))], axis=0)
            wsel = wx_t
            if n_seq > 1:
                wsel = jnp.where(jnp.logical_and(col >= r0, col < r0 + seq_rows), wx_t, 0.0)
            write_state(s, g, st * scale + _dot(wsel.astype(BF16), bg))
        yoff = yoff_rows[0] if n_seq == 1 else jnp.concatenate(yoff_rows, axis=0)
        y = jnp.concatenate(ydiag, axis=1) + yoff * ecum_e[:, gcols] + xs[:, gcols] * dexp_ref[:, gcols]
        gz = y * sz[:, gcols]
        ms = jnp.sum(gz * gz, axis=-1, keepdims=True) * (1.0 / gw)
        write_out(g, gz * lax.rsqrt(ms + RMS_EPS) * normw_ref[:, gcols])


def _ssd_prompt_kernel(*refs, gw, ways):
    ins, prm = refs[:3 * ways], refs[3 * ways:3 * ways + 5]
    outs, h_ref = refs[3 * ways + 5:5 * ways + 5], refs[5 * ways + 5]
    c = pl.program_id(1)

    @pl.when(c == 0)
    def _():
        h_ref[...] = jnp.zeros(h_ref.shape, F32)

    for k in range(ways):
        sz_ref, xact_ref, dt_ref = ins[3 * k:3 * k + 3]
        y_ref = outs[2 * k]

        def read_state(s, g, k=k):
            return h_ref[k, g * gw:(g + 1) * gw, :]

        def write_state(s, g, v, k=k):
            h_ref[k, g * gw:(g + 1) * gw, :] = v

        def write_out(g, v, y_ref=y_ref):
            y_ref[:, g * gw:(g + 1) * gw] = v.astype(y_ref.dtype)

        _ssd_chunk(xact_ref[...], sz_ref[...], dt_ref[...], prm, sz_ref.shape[0], read_state, write_state, write_out)

    @pl.when(c == pl.num_programs(1) - 1)
    def _():
        for k in range(ways):
            outs[2 * k + 1][0] = h_ref[k]


def _ssd_sample_kernel(sz_ref, ext_ref, dt_ref, hin_ref, convw_ref, convb_ref,
                       cums_ref, spread_ref, alog_ref, dexp_ref, normw_ref, *rest, gw, n_new):
    y_ref, hout_ref = rest[-2:]
    q = sz_ref.shape[0]
    n_seq = q // SAMPLE_ROWS
    first = CONV_LEAD - (CONV_W - 1)

    def window(k):
        return jnp.concatenate(
            [ext_ref[s * SAMPLE_EXT_ROWS + first + k:s * SAMPLE_EXT_ROWS + first + k + SAMPLE_ROWS, :]
             for s in range(n_seq)], axis=0)

    xact = _conv_silu(window, convw_ref, convb_ref)
    row = lax.broadcasted_iota(jnp.int32, (q, LANES), 0)
    dt = jnp.where((row % SAMPLE_ROWS) < n_new, dt_ref[...], 0.0)

    def read_state(s, g):
        return hin_ref[s, g * gw:(g + 1) * gw, :]

    def write_state(s, g, v):
        hout_ref[s, g * gw:(g + 1) * gw, :] = v

    prm = (cums_ref, spread_ref, alog_ref, dexp_ref, normw_ref)

    def write_out(g, v):
        y_ref[:, g * gw:(g + 1) * gw] = v.astype(y_ref.dtype)

    _ssd_chunk(xact, sz_ref[...], dt, prm, SAMPLE_ROWS, read_state, write_state, write_out)


def _ssd_consts(seq_rows, n_heads):
    q = CHUNK
    i = jnp.arange(q)[:, None]
    j = jnp.arange(q)[None, :]
    same = (i // seq_rows) == (j // seq_rows)
    tri = jnp.logical_and(same, j <= i).astype(BF16)
    tris = jnp.logical_and(same, j > i).astype(BF16)
    cums = jnp.tile(jnp.concatenate([tri, tris], axis=0), (1, 3))
    hrow = jnp.arange(LANES)[:, None]
    sel64 = (hrow == (jnp.arange(n_heads * HEAD_DIM)[None, :] // HEAD_DIM)).astype(BF16)
    spread = jnp.tile(sel64, (2, 1))
    return cums, spread


def _ssd_param_specs(prm_arrays):
    return [_const_spec(a.shape) for a in prm_arrays]


def _ssd_prompt(sz, xact, dt, n_seq, seq_len, lp):
    d_ssm = sz.shape[1]
    d_xbc = xact.shape[1]
    q = CHUNK
    nc = seq_len // q
    gw = d_ssm // N_GROUPS
    prm = (*lp["ssd_consts_prompt"], lp["a_log"], lp["d_exp"], lp["norm_w"])
    ways = SSD_WAYS if n_seq % SSD_WAYS == 0 else 1
    per = n_seq // ways
    in_specs, out_specs, out_shape = [], [], []
    for k in range(ways):
        rows = lambda b, c, k=k: ((k * per + b) * nc + c, 0)
        in_specs += [pl.BlockSpec((q, d_ssm), rows), pl.BlockSpec((q, d_xbc), rows), pl.BlockSpec((q, LANES), rows)]
        out_specs += [pl.BlockSpec((q, d_ssm), lambda b, c: (b * nc + c, 0)),
                      pl.BlockSpec((1, d_ssm, D_STATE), lambda b, c: (b, 0, 0))]
        out_shape += [jax.ShapeDtypeStruct((per * seq_len, d_ssm), BF16),
                      jax.ShapeDtypeStruct((per, d_ssm, D_STATE), F32)]
    res = pl.pallas_call(
        functools.partial(_ssd_prompt_kernel, gw=gw, ways=ways),
        grid=(per, nc),
        in_specs=in_specs + _ssd_param_specs(prm),
        out_specs=out_specs,
        out_shape=out_shape,
        scratch_shapes=[pltpu.VMEM((ways, d_ssm, D_STATE), F32)],
        compiler_params=pltpu.CompilerParams(dimension_semantics=("arbitrary", "arbitrary"),
                                             vmem_limit_bytes=_vmem_limit(48 * 1024 * 1024)),
        name="ssd_prompt",
    )(*([sz, xact, dt] * ways), *prm)
    return list(res[0::2]), list(res[1::2])


def _ssd_sample(z16, ext, dtr16, h_all, h_new, layer, lp, n_new):
    d_ssm = z16.shape[1]
    d_xbc = ext.shape[1]
    q = CHUNK
    spc = q // SAMPLE_ROWS
    bs = h_all.shape[1]
    gw = d_ssm // N_GROUPS
    prm = (lp["conv_w"], lp["conv_b"], *lp["ssd_consts_sample"], lp["a_log"], lp["d_exp"], lp["norm_w"])
    rows = lambda i: (i, 0)
    slab =pl.BlockSpec((None, spc, d_ssm, D_STATE), lambda i: (layer, i, 0, 0))
    state_bytes = spc * d_ssm * D_STATE * 4
    prev = [] if h_new is None else [h_new]
    n_in = 4 + len(prm)
    return pl.pallas_call(
        functools.partial(_ssd_sample_kernel, gw=gw, n_new=n_new),
        grid=(bs // spc,),
        in_specs=[pl.BlockSpec((q, d_ssm), rows), pl.BlockSpec((spc * SAMPLE_EXT_ROWS, d_xbc), rows),
                  pl.BlockSpec((q, LANES), rows), slab] + _ssd_param_specs(prm)
                 + [pl.BlockSpec(memory_space=pl.ANY) for _ in prev],
        out_specs=[pl.BlockSpec((q, d_ssm), rows), slab],
        out_shape=[jax.ShapeDtypeStruct((bs * SAMPLE_ROWS, d_ssm), BF16),
                   jax.ShapeDtypeStruct(h_all.shape, F32)],
        input_output_aliases={n_in: 1} if prev else {},
        compiler_params=pltpu.CompilerParams(dimension_semantics=("arbitrary",),
                                             vmem_limit_bytes=_vmem_limit(4 * state_bytes + 20 * 1024 * 1024)),
        name="ssd_sample",
    )(z16, ext, dtr16, h_all, *prm, *prev)


def _route(h, rw_ref, rb_ref, tril_ref, cnt_ref, n_experts):
    xh = h.astype(BF16)
    xl = (h - xh.astype(F32)).astype(BF16)
    w = rw_ref[...]
    wh = w.astype(BF16)
    wl = (w - wh.astype(F32)).astype(BF16)
    logits = _dot(xh, wh) + (_dot(xh, wl) + _dot(xl, wh)) + rb_ref[...]
    lane = lax.broadcasted_iota(jnp.int32, logits.shape, 1)
    lane_f = lane.astype(F32)
    logits = jnp.where(lane < n_experts, logits, -jnp.inf)
    m1 = jnp.max(logits, axis=-1, keepdims=True)
    i1 = jnp.min(jnp.where(logits == m1, lane_f, float(LANES)), axis=-1, keepdims=True)
    rest = jnp.where(lane_f == i1, -jnp.inf, logits)
    m2 = jnp.max(rest, axis=-1, keepdims=True)
    i2 = jnp.min(jnp.where(rest == m2, lane_f, float(LANES)), axis=-1, keepdims=True)
    e2 = jnp.exp(m2 - m1)
    den = 1.0 + e2
    oh1 = jnp.where(lane_f == i1, 1.0, 0.0)
    oh2 = jnp.where(lane_f == i2, 1.0, 0.0)
    before1 = _dot(tril_ref[...], oh1.astype(BF16))
    before2 = _dot(tril_ref[...], oh2.astype(BF16))
    c1 = jnp.sum(oh1, axis=0, keepdims=True)
    c2 = jnp.sum(oh2, axis=0, keepdims=True)
    base = cnt_ref[...]
    r1 = jnp.sum(oh1 * (before1 + base), axis=-1, keepdims=True)
    r2 = jnp.sum(oh2 * (before2 + (base + c1)), axis=-1, keepdims=True)
    cnt_ref[...] = base + (c1 + c2)
    idx = jnp.where(lane == 0, i1, jnp.where(lane == 1, i2, 0.0)).astype(jnp.int32)
    gate = jnp.where(lane == 0, 1.0 / den, jnp.where(lane == 1, e2 / den, 0.0))
    rank = jnp.where(lane == 0, r1, jnp.where(lane == 1, r2, 0.0)).astype(jnp.int32)
    return idx, gate, rank


def _outproj_ln_kernel(*refs, alpha, n_seg, offs, route, n_experts):
    it = iter(refs)
    take = lambda n: [next(it) for _ in range(n)]
    x_refs, p_refs, s_refs = take(n_seg[0]), take(n_seg[1]), take(n_seg[2])
    wp_ref, ws_ref, g_ref, b_ref = take(4)
    x = _seg_read(x_refs, offs[0])
    mixed = _dot(_seg_read(p_refs, offs[1]), wp_ref[...]) + _dot(_seg_read(s_refs, offs[2]), ws_ref[...])
    h = _layer_norm(alpha * x + mixed, g_ref[...], b_ref[...])
    if not route:
        (o_ref,) = take(1)
        o_ref[...] = h
        return
    rw_ref, rb_ref, tril_ref = take(3)
    xt_ref, idx_ref, gate_ref, rank_ref, cnt_ref = take(5)

    @pl.when(pl.program_id(0) == 0)
    def _():
        cnt_ref[...] = jnp.zeros(cnt_ref.shape, F32)

    tm, d = h.shape
    rpt = d // LANES
    for k in range(rpt):
        xt_ref[pl.ds(k, tm, stride=rpt), :] = h[:, k * LANES:(k + 1) * LANES]
    idx, gate, rank = _route(h, rw_ref, rb_ref, tril_ref, cnt_ref, n_experts)
    idx_ref[...] = idx
    gate_ref[...] = gate
    rank_ref[...] = rank


def _outproj_ln(x_segs, pool_segs, ssd_segs, w_pool, w_ssd, g, b, alpha, router=None):
    d = x_segs[0].shape[1]
    rows = [a.shape[0] for a in x_segs + pool_segs + ssd_segs]
    tm = _common_tile(rows, (512, 256, 128, 64))
    groups = (x_segs, pool_segs, ssd_segs)
    tiles = [_seg_tiles(s, tm) for s in groups]
    t = sum(tiles[0]) * tm
    assert all(sum(ts) * tm == t for ts in tiles)
    in_specs = [sp for s in groups for sp in _seg_specs(s, tm)]
    consts = [w_pool, w_ssd, g, b]
    row = lambda w, dt: (pl.BlockSpec((tm, w), lambda i: (i, 0)), jax.ShapeDtypeStruct((t, w), dt))
    if router is None:
        outs = [row(d, F32)]
        n_experts = 0
    else:
        rw_pad, rb_pad, n_experts = router
        tril = (jnp.arange(tm)[:, None] > jnp.arange(tm)[None, :]).astype(BF16)
        consts += [rw_pad, rb_pad, tril]
        rpt = d // LANES
        outs = [(pl.BlockSpec((tm * rpt, LANES), lambda i: (i, 0)), jax.ShapeDtypeStruct((t * rpt, LANES), F32)),
                row(LANES, jnp.int32), row(LANES, F32), row(LANES, jnp.int32),
                (_const_spec((1, LANES)), jax.ShapeDtypeStruct((1, LANES), F32))]
    in_specs += [_const_spec(c.shape) for c in consts]
    res = pl.pallas_call(
        functools.partial(_outproj_ln_kernel, alpha=alpha, n_seg=[len(s) for s in groups],
                          offs=[_seg_offsets(ts) for ts in tiles], route=router is not None, n_experts=n_experts),
        grid=(t // tm,),
        in_specs=in_specs,
        out_specs=[o[0] for o in outs],
        out_shape=[o[1] for o in outs],
        compiler_params=pltpu.CompilerParams(dimension_semantics=("arbitrary",),
                                             vmem_limit_bytes=_vmem_limit(40 * 1024 * 1024)),
        name="outproj_ln",
    )(*x_segs, *pool_segs, *ssd_segs, *consts)
    return res[0] if router is None else res


def _swiglu(xb, wg_ref, wu_ref, wd_ref):
    act = (_silu(_dot(xb, wg_ref[0])) * _dot(xb, wu_ref[0])).astype(BF16)
    return _dot(act, wd_ref[0])


def _dense_ffn_ln_kernel(x_ref, wg_ref, wu_ref, wd_ref, g_ref, b_ref, *out_refs, alpha, offs, tiles):
    x = x_ref[...]
    f = _swiglu(x.astype(BF16), wg_ref, wu_ref, wd_ref)
    _seg_write(out_refs, offs, tiles, _layer_norm(alpha * x + f, g_ref[...], b_ref[...]))


def _dense_ffn_ln(x, wg, wu, wd, g, b, alpha, out_rows):
    t, d = x.shape
    ff = wg.shape[2]
    tm = _common_tile(out_rows, (256, 128, 64))
    tiles = [n // tm for n in out_rows]
    offs = _seg_offsets(tiles)
    vmem = 2 * 3 * d * ff * 2 + 4 * tm * d * 4 + tm * ff * 12
    res = pl.pallas_call(
        functools.partial(_dense_ffn_ln_kernel, alpha=alpha, offs=offs, tiles=tiles),
        grid=(t // tm,),
        in_specs=[pl.BlockSpec((tm, d), lambda i: (i, 0)), _const_spec(wg.shape), _const_spec(wu.shape),
                  _const_spec(wd.shape), _const_spec(g.shape), _const_spec(b.shape)],
        out_specs=[pl.BlockSpec((tm, d), lambda i, o=o, n=n: (jnp.clip(i - o, 0, n - 1), 0))
                   for o, n in zip(offs, tiles)],
        out_shape=[jax.ShapeDtypeStruct((n, d), F32) for n in out_rows],
        compiler_params=pltpu.CompilerParams(dimension_semantics=("arbitrary",),
                                             vmem_limit_bytes=_vmem_limit(vmem)),
        name="dense_ffn_ln",
    )(x, wg, wu, wd, g, b)
    return list(res)


def _dispatch_kernel(ends_ref, slot_ref, x_ref, out_hbm, zeros_ref, sem, zsem, *, tokens, rpt, tile, n_experts):
    i = pl.program_id(0)

    def tail_copy(e):
        start = pl.multiple_of((ends_ref[e] - tile) * rpt, tile * rpt)
        return pltpu.make_async_copy(zeros_ref, out_hbm.at[pl.ds(start, tile * rpt)], zsem)

    def nonempty(e):
        return ends_ref[e] > (ends_ref[e - 1] if e > 0 else 0)

    def unused_copy(j):
        start = pl.multiple_of((ends_ref[n_experts - 1] + j * tile) * rpt, tile * rpt)
        return pltpu.make_async_copy(zeros_ref, out_hbm.at[pl.ds(start, tile * rpt)], zsem)

    def unused(j):
        return ends_ref[n_experts - 1] + (j + 1) * tile <= out_hbm.shape[0] // rpt

    @pl.when(i == 0)
    def _():
        zeros_ref[...] = jnp.zeros(zeros_ref.shape, zeros_ref.dtype)
        for e in range(n_experts):
            @pl.when(nonempty(e))
            def _(e=e):
                tail_copy(e).start()

            @pl.when(unused(e))
            def _(e=e):
                unused_copy(e).start()
        for e in range(n_experts):
            @pl.when(nonempty(e))
            def _(e=e):
                tail_copy(e).wait()

            @pl.when(unused(e))
            def _(e=e):
                unused_copy(e).wait()

    def issue(r, carry):
        src = x_ref.at[pl.ds(pl.multiple_of(r * rpt, rpt), rpt)]
        for k in range(TOP_K):
            dst = out_hbm.at[pl.ds(pl.multiple_of(slot_ref[0, 0, k * tokens + r] * rpt, rpt), rpt)]
            pltpu.make_async_copy(src, dst, sem).start()
        return carry

    lax.fori_loop(0, tokens, issue, 0)
    for k in range(TOP_K):
        pltpu.make_async_copy(x_ref, out_hbm.at[pl.ds(0, tokens * rpt)], sem).wait()


def _dispatch(xt, slots, ends, n_slots, tokens, rpt, n_experts):
    steps = slots.shape[0]
    return pl.pallas_call(
        functools.partial(_dispatch_kernel, tokens=tokens, rpt=rpt, tile=MOE_TILE, n_experts=n_experts),
        grid_spec=pltpu.PrefetchScalarGridSpec(
            num_scalar_prefetch=1, grid=(steps,),
            in_specs=[pl.BlockSpec((1, 1, TOP_K * tokens), lambda i, e: (i, 0, 0), memory_space=pltpu.SMEM),
                      pl.BlockSpec((tokens * rpt, LANES), lambda i, e: (i, 0))],
            out_specs=pl.BlockSpec(memory_space=pl.ANY),
            scratch_shapes=[pltpu.VMEM((MOE_TILE * rpt, LANES), xt.dtype),
                            pltpu.SemaphoreType.DMA(()), pltpu.SemaphoreType.DMA(())]),
        out_shape=jax.ShapeDtypeStruct((n_slots * rpt, LANES), xt.dtype),
        compiler_params=pltpu.CompilerParams(dimension_semantics=("arbitrary",)),
        name="moe_dispatch",
    )(ends, slots, xt)


def _moe_ffn_kernel(te_ref, na_ref, x_ref, wg_ref, wu_ref, wd_ref, o_ref, *, tm, rpt):
    active = pl.program_id(0) < na_ref[0]

    @pl.when(active)
    def _():
        xb = jnp.concatenate([x_ref[pl.ds(k, tm, stride=rpt), :] for k in range(rpt)], axis=1).astype(BF16)
        f = _swiglu(xb, wg_ref, wu_ref, wd_ref)
        for k in range(rpt):
            o_ref[pl.ds(k, tm, stride=rpt), :] = f[:, k * LANES:(k + 1) * LANES]

    @pl.when(jnp.logical_not(active))
    def _():
        o_ref[...] = jnp.zeros(o_ref.shape, o_ref.dtype)


def _moe_ffn(xs, wg, wu, wd, tile_expert, n_active, rpt):
    tm = MOE_TILE
    d, ff = wg.shape[1], wg.shape[2]
    n_tiles = xs.shape[0] // (tm * rpt)
    w_idx = lambda i, te, na: (te[i], 0, 0)
    vmem = 2 * 3 * d * ff * 2 + 4 * tm * d * 4 + tm * ff * 12 + 2 * tm * d * 4
    return pl.pallas_call(
        functools.partial(_moe_ffn_kernel, tm=tm, rpt=rpt),
        grid_spec=pltpu.PrefetchScalarGridSpec(
            num_scalar_prefetch=2, grid=(n_tiles,),
            in_specs=[pl.BlockSpec((tm * rpt, LANES), lambda i, te, na: (jnp.minimum(i, na[0] - 1), 0)),
                      pl.BlockSpec((1, d, ff), w_idx), pl.BlockSpec((1, d, ff), w_idx),
                      pl.BlockSpec((1, ff, d), w_idx)],
            out_specs=pl.BlockSpec((tm * rpt, LANES), lambda i, te, na: (i, 0))),
        out_shape=jax.ShapeDtypeStruct(xs.shape, F32),
        compiler_params=pltpu.CompilerParams(dimension_semantics=("arbitrary",),
                                             vmem_limit_bytes=_vmem_limit(vmem)),
        name="moe_ffn",
    )(tile_expert, n_active, xs, wg, wu, wd)


def _combine_ln_kernel(slot_ref, next_ref, x_ref, gate_ref, y_hbm, g_ref, b_ref, *rest, alpha, tm, rpt, offs, tiles):
    out_refs, (buf, sem) = rest[:len(tiles)], rest[len(tiles):]
    i = pl.program_id(0)
    n_rows = TOP_K * tm * rpt

    def issue(s_ref, slot):
        def body(r, carry):
            src = y_hbm.at[pl.ds(pl.multiple_of(s_ref[0, 0, r] * rpt, rpt), rpt)]
            pltpu.make_async_copy(src, buf.at[slot, pl.ds(pl.multiple_of(r * rpt, rpt), rpt)], sem.at[slot]).start()
            return carry
        lax.fori_loop(0, TOP_K * tm, body, 0)

    @pl.when(i == 0)
    def _():
        issue(slot_ref, 0)

    @pl.when(i + 1 < pl.num_programs(0))
    def _():
        issue(next_ref, (i + 1) % 2)

    cur = i % 2
    pltpu.make_async_copy(y_hbm.at[pl.ds(0, n_rows)], buf.at[cur], sem.at[cur]).wait()
    g1 = gate_ref[:, 0:1]
    g2 = gate_ref[:, 1:2]
    cols = []
    for k in range(rpt):
        xk = x_ref[pl.ds(k, tm, stride=rpt), :]
        ya = buf[cur, pl.ds(k, tm, stride=rpt), :]
        yb = buf[cur, pl.ds(tm * rpt + k, tm, stride=rpt), :]
        cols.append(alpha * xk + (g1 * ya + g2 * yb))
    h = jnp.concatenate(cols, axis=1)
    _seg_write(out_refs, offs, tiles, _layer_norm(h, g_ref[...], b_ref[...]))


def _combine_ln(xt, gate, ys, slots, g, b, alpha, tm, rpt, out_rows):
    d = g.shape[1]
    tiles = [n // tm for n in out_rows]
    offs = _seg_offsets(tiles)
    steps = slots.shape[0]
    slot_spec = lambda f: pl.BlockSpec((1, 1, TOP_K * tm), f, memory_space=pltpu.SMEM)
    res = pl.pallas_call(
        functools.partial(_combine_ln_kernel, alpha=alpha, tm=tm, rpt=rpt, offs=offs, tiles=tiles),
        grid=(steps,),
        in_specs=[slot_spec(lambda i: (i, 0, 0)), slot_spec(lambda i: (jnp.minimum(i + 1, steps - 1), 0, 0)),
                  pl.BlockSpec((tm * rpt, LANES), lambda i: (i, 0)), pl.BlockSpec((tm, LANES), lambda i: (i, 0)),
                  pl.BlockSpec(memory_space=pl.ANY), _const_spec(g.shape), _const_spec(b.shape)],
        out_specs=[pl.BlockSpec((tm, d), lambda i, o=o, n=n: (jnp.clip(i - o, 0, n - 1), 0))
                   for o, n in zip(offs, tiles)],
        out_shape=[jax.ShapeDtypeStruct((n, d), F32) for n in out_rows],
        scratch_shapes=[pltpu.VMEM((2, TOP_K * tm * rpt, LANES), F32), pltpu.SemaphoreType.DMA((2,))],
        compiler_params=pltpu.CompilerParams(dimension_semantics=("arbitrary",),
                                             vmem_limit_bytes=_vmem_limit(32 * 1024 * 1024)),
        name="moe_combine_ln",
    )(slots, slots, xt, gate, ys, g, b)
    return list(res)


def _moe_ffn_ln(xt, idx, gate, rank, counts, wg, wu, wd, g, b, alpha, tm, out_rows):
    n_experts = wg.shape[0]
    d = wg.shape[1]
    rpt = d // LANES
    t = idx.shape[0]
    tile = MOE_TILE
    cnt = counts[0, :n_experts].astype(jnp.int32)
    padded = ((cnt + tile - 1) // tile) * tile
    ends = jnp.cumsum(padded).astype(jnp.int32)
    starts = ends - padded
    slot = jnp.take(starts, idx[:, :TOP_K]) + rank[:, :TOP_K]
    n_slots = -(-(TOP_K * t + n_experts * (tile - 1)) // tile) * tile
    n_tiles = n_slots // tile
    n_active = ends[-1:] // tile
    tile_start = jnp.minimum(jnp.arange(n_tiles, dtype=jnp.int32), n_active[0] - 1) * tile
    tile_expert = jnp.sum((tile_start[:, None] >= ends[None, :]).astype(jnp.int32), axis=1)
    slots = jnp.swapaxes(slot.reshape(t // tm, tm, TOP_K), 1, 2).reshape(t // tm, 1, TOP_K * tm)
    xs = _dispatch(xt, slots, ends, n_slots, tm, rpt, n_experts)
    ys = _moe_ffn(xs, wg, wu, wd, tile_expert, n_active, rpt)
    return _combine_ln(xt, gate, ys, slots, g, b, alpha, tm, rpt, out_rows)


def kernel(x_prompt, x_sample, state_pool, state_conv, state_ssm, w_in, conv_w, conv_b, dt_bias, A_log, D_skip,
           ssm_norm_w, pool_w, pool_scale, w_out, ln1_g, ln1_b, ln2_g, ln2_b, ffn_w_gate, ffn_w_up, ffn_w_down,
           router_w, router_b, moe_w_gate, moe_w_up, moe_w_down):
    bp, seq, d = x_prompt.shape
    bs, n_new, _ = x_sample.shape
    depth = w_in.shape[0]
    d_pool = pool_scale.shape[1]
    d_ssm = ssm_norm_w.shape[1]
    d_xbc = conv_w.shape[2]
    n_heads = dt_bias.shape[1]
    tp, ts = bp * seq, bs * n_new
    alpha = (2.0 * depth) ** 0.25
    assert d_pool == POOL_GROUP * len(POOL_WINDOWS) and d_ssm == n_heads * HEAD_DIM
    assert d_xbc == d_ssm + 2 * N_GROUPS * D_STATE and seq % CHUNK == 0 and n_new <= CONV_W
    assert n_heads <= LANES and bs % (CHUNK // SAMPLE_ROWS) == 0 and d % LANES == 0

    consts_prompt = _ssd_consts(CHUNK, n_heads)
    consts_sample = _ssd_consts(SAMPLE_ROWS, n_heads)
    x_segs = [x_prompt.reshape(tp, d), x_sample.reshape(ts, d)]
    new_pool_p, new_conv_p, new_ssm_p, new_pool_s, new_conv_s = [], [], [], [], []
    ssm_s = None
    for l in range(depth):
        out_rows = [tp, ts]
        n_in = w_in.shape[2]
        w_in_pad = jnp.pad(w_in[l].astype(BF16), ((0, 0), (0, d_pool + d_ssm + d_xbc + LANES - n_in)))
        lp = dict(
            conv_w=conv_w[l], conv_b=conv_b[l][None, :],
            dt_bias=jnp.zeros((1, LANES), F32).at[0, :n_heads].set(dt_bias[l]),
            a_log=jnp.zeros((1, LANES), F32).at[0, :n_heads].set(A_log[l]),
            d_exp=jnp.repeat(D_skip[l], HEAD_DIM)[None, :], norm_w=ssm_norm_w[l][None, :],
            ssd_consts_prompt=consts_prompt, ssd_consts_sample=consts_sample)
        pw = pool_w[l].astype(BF16)
        ps = pool_scale[l][None, :]

        proj = (w_in_pad, lp["conv_w"], lp["conv_b"], lp["dt_bias"], d_pool, d_ssm, d_xbc)

        u_p, sz_p, xact_p, dt_p, c_p = _in_proj(x_segs[0], *proj, seq_len=seq)
        pool_p, buf_p = _pool_prompt(u_p, tp, seq, pw, ps)
        y_p, h_p = _ssd_prompt(sz_p, xact_p, dt_p, bp, seq, lp)
        new_pool_p.append(buf_p)
        new_conv_p.append(c_p)
        new_ssm_p.append(jnp.concatenate(h_p, axis=0).reshape(bp, n_heads, HEAD_DIM, D_STATE))

        u_s, sz_s, xbc_s, dt_s = _in_proj(x_segs[1], *proj)
        u_s = u_s.reshape(bs, n_new, d_pool)
        xbc_s = xbc_s.reshape(bs, n_new, d_xbc)
        pool_ext = jnp.concatenate([state_pool[l], u_s], axis=1)
        pool_s = _pool_sample(jnp.swapaxes(pool_ext, 0, 1), pw, ps, n_new, PAST_LEN)
        pool_s = jnp.swapaxes(pool_s, 0, 1).reshape(ts, d_pool)
        conv_ext = jnp.concatenate([state_conv[l], xbc_s], axis=1)
        lead = CONV_LEAD - (CONV_W - 1)
        ext = jnp.pad(conv_ext, ((0, 0), (lead, SAMPLE_EXT_ROWS - lead - conv_ext.shape[1]), (0, 0)))
        pad_rows = lambda a: jnp.pad(a.reshape(bs, n_new, -1), ((0, 0), (0, SAMPLE_ROWS - n_new), (0, 0))
                                     ).reshape(bs * SAMPLE_ROWS, -1)
        y_s16, ssm_s = _ssd_sample(pad_rows(sz_s), ext.reshape(bs * SAMPLE_EXT_ROWS, d_xbc), pad_rows(dt_s),
                                   state_ssm.reshape(depth, bs, d_ssm, D_STATE), ssm_s, l, lp, n_new)
        y_s = y_s16.reshape(bs, SAMPLE_ROWS, d_ssm)[:, :n_new].reshape(ts, d_ssm)
        new_pool_s.append(pool_ext[:, n_new:])
        new_conv_s.append(conv_ext[:, n_new:])

        w_o = w_out[l].astype(BF16)
        ln1 = (ln1_g[l][None, :], ln1_b[l][None, :])
        g2, b2 = ln2_g[l][None, :], ln2_b[l][None, :]
        j = l // 2
        if l % 2 == 0:
            x1 = _outproj_ln(x_segs, [pool_p, pool_s], [*y_p, y_s], w_o[:d_pool], w_o[d_pool:], *ln1, alpha)
            x_segs = _dense_ffn_ln(x1, ffn_w_gate[j:j + 1].astype(BF16), ffn_w_up[j:j + 1].astype(BF16),
                                   ffn_w_down[j:j + 1].astype(BF16), g2, b2, alpha, out_rows)
        else:
            n_experts = router_w.shape[2]
            rw_pad = jnp.zeros((d, LANES), F32).at[:, :n_experts].set(router_w[j])
            rb_pad = jnp.zeros((1, LANES), F32).at[0, :n_experts].set(router_b[j])
            xt, idx, gate, rank, counts = _outproj_ln(x_segs, [pool_p, pool_s], [*y_p, y_s], w_o[:d_pool],
                                                      w_o[d_pool:], *ln1, alpha, router=(rw_pad, rb_pad, n_experts))
            tm = _common_tile([tp, ts], (512, 256, 128, 64))
            x_segs = _moe_ffn_ln(xt, idx, gate, rank, counts, moe_w_gate[j].astype(BF16), moe_w_up[j].astype(BF16),
                                 moe_w_down[j].astype(BF16), g2, b2, alpha, tm, out_rows)

    y_prompt, y_sample = x_segs
    return (y_prompt.reshape(bp, seq, d), y_sample.reshape(bs, n_new, d),
            jnp.stack(new_pool_p), jnp.stack(new_conv_p), jnp.stack(new_ssm_p),
            jnp.stack(new_pool_s), jnp.stack(new_conv_s),
            ssm_s.reshape(depth, bs, n_heads, HEAD_DIM, D_STATE))
```

```python
import functools

import jax
import jax.numpy as jnp
from jax import lax
from jax.experimental import pallas as pl
from jax.experimental.pallas import tpu as pltpu

F32 = jnp.float32
BF16 = jnp.bfloat16

PAST_LEN = 16384
POOL_WINDOWS = (2, 4, 8, 16)
POOL_GROUP = 128
POOL_BUF = max(POOL_WINDOWS) - 1
HEAD_DIM = 64
N_GROUPS = 4
D_STATE = 128
CONV_W = 4
CHUNK = 128
TOP_K = 2
LN_EPS = 1e-5
RMS_EPS = 1e-6
LOG2_E = 1.4426950408889634

LANES = 128
SUBLANES = 8
V7X_VMEM_BYTES = 64 * 1024 * 1024
VMEM_CAP = V7X_VMEM_BYTES - 8 * 1024 * 1024

SAMPLE_ROWS = 16
SAMPLE_EXT_ROWS = 24
CONV_LEAD = 8
MOE_TILE = 256
SSD_WAYS = 2


def _vmem_limit(nbytes):
    return int(min(VMEM_CAP, nbytes + 6 * 1024 * 1024))


def _dot(a, b):
    return jnp.dot(a, b, preferred_element_type=F32)


def _dot_nt(a, b):
    return lax.dot_general(a, b, (((1,), (1,)), ((), ())), preferred_element_type=F32)


def _split3(v):
    hi = v.astype(BF16)
    r = v - hi.astype(F32)
    mid = r.astype(BF16)
    lo = (r - mid.astype(F32)).astype(BF16)
    return hi, mid, lo


def _sel_right(parts, m):
    out = _dot(parts[0], m)
    for p in parts[1:]:
        out = out + _dot(p, m)
    return out


def _sel_left(m, parts):
    out = _dot(m, parts[0])
    for p in parts[1:]:
        out = out + _dot(m, p)
    return out


def _silu(x):
    return x / (1.0 + jnp.exp2(x * (-LOG2_E)))


def _layer_norm(h, g, b):
    mu = jnp.mean(h, axis=-1, keepdims=True)
    d = h - mu
    var = jnp.mean(d * d, axis=-1, keepdims=True)
    return d * lax.rsqrt(var + LN_EPS) * g + b


def _pick_tile(n, candidates):
    for c in candidates:
        if n % c == 0:
            return c
    raise ValueError(f"no tile in {candidates} divides {n}")


def _const_spec(shape, single=False):
    nd = len(shape)
    if single:
        return pl.BlockSpec(shape, lambda *_: (0,) * nd, pipeline_mode=pl.Buffered(1))
    return pl.BlockSpec(shape, lambda *_: (0,) * nd)


def _seg_tiles(segs, tm, rows_per_token=1):
    return [a.shape[0] // (tm * rows_per_token) for a in segs]


def _seg_offsets(tiles):
    offs, off = [], 0
    for n in tiles:
        offs.append(off)
        off += n
    return offs


def _seg_specs(segs, tm, rows_per_token=1):
    tiles = _seg_tiles(segs, tm, rows_per_token)
    return [pl.BlockSpec((tm * rows_per_token, a.shape[1]),
                         lambda i, *_, o=o, n=n: (jnp.clip(i - o, 0, n - 1), 0))
            for a, o, n in zip(segs, _seg_offsets(tiles), tiles)]


def _seg_read(refs, offs):
    v = refs[0][...]
    for r, o in zip(refs[1:], offs[1:]):
        v = jnp.where(pl.program_id(0) >= o, r[...], v)
    return v


def _seg_write(refs, offs, tiles, v):
    i = pl.program_id(0)
    for r, o, n in zip(refs, offs, tiles):
        @pl.when(jnp.logical_and(i >= o, i < o + n))
        def _(r=r):
            r[...] = v.astype(r.dtype)


def _common_tile(row_counts, candidates):
    for c in candidates:
        if all(n % c == 0 for n in row_counts):
            return c
    raise ValueError(f"no tile in {candidates} divides all of {row_counts}")


INPROJ_CHUNK = 512


def _inproj_kernel(x_ref, w_ref, convw_ref, convb_ref, dtb_ref, u_ref, sz_ref, xo_ref, dt_ref, *rest,
                   splits, tiles_per_seq):
    (u0, u1), (z0, z1), (c0, c1), (d0, d1) = splits
    tm = x_ref.shape[0]
    xb = x_ref[...].astype(BF16)
    u_ref[...] = _dot(xb, w_ref[:, u0:u1])
    dt_ref[...] = _softplus(_dot(xb, w_ref[:, d0:d1]) + dtb_ref[...])
    for lo in range(z0, z1, INPROJ_CHUNK):
        sz_ref[:, lo - z0:lo - z0 + INPROJ_CHUNK] = _silu(_dot(xb, w_ref[:, lo:lo + INPROJ_CHUNK]))

    cstate_ref, ext_ref = rest
    pos = pl.program_id(0) % tiles_per_seq

    @pl.when(pos == 0)
    def _():
        ext_ref[0:CONV_LEAD, :] = jnp.zeros((CONV_LEAD, ext_ref.shape[1]), F32)

    @pl.when(pos > 0)
    def _():
        ext_ref[0:CONV_LEAD, :] = ext_ref[tm:tm + CONV_LEAD, :]

    first = CONV_LEAD - (CONV_W - 1)
    for lo in range(c0, c1, INPROJ_CHUNK):
        cols = slice(lo - c0, lo - c0 + INPROJ_CHUNK)
        ext_ref[CONV_LEAD:CONV_LEAD + tm, cols] = _dot(xb, w_ref[:, lo:lo + INPROJ_CHUNK])
        rows = ext_ref[:, cols]
        acc = convb_ref[:, cols] + (pltpu.roll(rows, CONV_W - 1, axis=0)[CONV_LEAD:] * convw_ref[0:1, cols])
        for k in range(1, CONV_W - 1):
            acc = acc + pltpu.roll(rows, CONV_W - 1 - k, axis=0)[CONV_LEAD:] * convw_ref[k:k + 1, cols]
        acc = acc + rows[CONV_LEAD:] * convw_ref[CONV_W - 1:CONV_W, cols]
        xo_ref[:, cols] = _silu(acc)

    @pl.when(pos == tiles_per_seq - 1)
    def _():
        cstate_ref[0] = ext_ref[CONV_LEAD + tm - (CONV_W - 1):CONV_LEAD + tm, :]


def _inproj_splits(w_pad, d_pool, d_ssm, d_xbc):
    assert d_ssm % INPROJ_CHUNK == 0 and d_xbc % INPROJ_CHUNK == 0
    return ((0, d_pool), (d_pool, d_pool + d_ssm), (d_pool + d_ssm, d_pool + d_ssm + d_xbc),
            (d_pool + d_ssm + d_xbc, w_pad.shape[1]))


def _in_proj(x, w_pad, conv_w, conv_b, dt_bias, d_pool, d_ssm, d_xbc, seq_len):
    t, d = x.shape
    n = w_pad.shape[1]
    tm = _pick_tile(seq_len, (512, 256, 128, 64))
    splits = _inproj_splits(w_pad, d_pool, d_ssm, d_xbc)
    widths = [hi - lo for lo, hi in splits]
    tiles_per_seq = seq_len // tm
    vmem = 2 * tm * d * 4 + d * n * 2 + 2 * tm * n * 4 + 2 * tm * d_xbc * 4 + 4 * tm * INPROJ_CHUNK * 4
    return pl.pallas_call(
        functools.partial(_inproj_kernel, splits=splits, tiles_per_seq=tiles_per_seq),
        grid=(t // tm,),
        in_specs=[pl.BlockSpec((tm, d), lambda i: (i, 0)), _const_spec((d, n), single=True),
                  _const_spec(conv_w.shape), _const_spec(conv_b.shape), _const_spec(dt_bias.shape)],
        out_specs=[pl.BlockSpec((tm, w), lambda i: (i, 0)) for w in widths]
                  + [pl.BlockSpec((1, CONV_W - 1, d_xbc), lambda i: (i // tiles_per_seq, 0, 0))],
        out_shape=[jax.ShapeDtypeStruct((t, w), F32) for w in widths]
                  + [jax.ShapeDtypeStruct((t // seq_len, CONV_W - 1, d_xbc), F32)],
        scratch_shapes=[pltpu.VMEM((tm + CONV_LEAD, d_xbc), F32)],
        compiler_params=pltpu.CompilerParams(dimension_semantics=("arbitrary",),
                                             vmem_limit_bytes=_vmem_limit(vmem)),
        name="in_proj",
    )(x, w_pad, conv_w, conv_b, dt_bias)


def _inproj_sample_kernel(x_ref, w_ref, dtb_ref, u_ref, sz_ref, xbc_ref, dt_ref, *, splits):
    (u0, u1), (z0, z1), (c0, c1), (d0, d1) = splits
    xb = x_ref[...].astype(BF16)
    u_ref[...] = _dot(xb, w_ref[:, u0:u1])
    sz_ref[...] = _silu(_dot(xb, w_ref[:, z0:z1]))
    xbc_ref[...] = _dot(xb, w_ref[:, c0:c1])
    dt_ref[...] = _softplus(_dot(xb, w_ref[:, d0:d1]) + dtb_ref[...])


def _in_proj_sample(x, w_pad, dt_bias, d_pool, d_ssm, d_xbc):
    t, d = x.shape
    n = w_pad.shape[1]
    tm = _pick_tile(t, (512, 256, 128, 64))
    splits = _inproj_splits(w_pad, d_pool, d_ssm, d_xbc)
    widths = [hi - lo for lo, hi in splits]
    vmem = 2 * tm * d * 4 + 2 * d * n * 2 + 4 * tm * n * 4
    return pl.pallas_call(
        functools.partial(_inproj_sample_kernel, splits=splits),
        grid=(t // tm,),
        in_specs=[pl.BlockSpec((tm, d), lambda i: (i, 0)), _const_spec((d, n)), _const_spec(dt_bias.shape)],
        out_specs=[pl.BlockSpec((tm, w), lambda i: (i, 0)) for w in widths],
        out_shape=[jax.ShapeDtypeStruct((t, w), F32) for w in widths],
        compiler_params=pltpu.CompilerParams(dimension_semantics=("arbitrary",),
                                             vmem_limit_bytes=_vmem_limit(vmem)),
        name="in_proj_sample",
    )(x, w_pad, dt_bias)


def _pool_prompt_kernel(u_ref, w_ref, scale_ref, o_ref, buf_ref, ext_ref, *, tl):
    j = pl.program_id(1)

    @pl.when(j == pl.num_programs(1) - 1)
    def _():
        buf_ref[0] = u_ref[tl - POOL_BUF:tl, :]

    @pl.when(j == 0)
    def _():
        ext_ref[0:16, :] = jnp.zeros((16, ext_ref.shape[1]), F32)

    @pl.when(j > 0)
    def _():
        ext_ref[0:16, :] = ext_ref[tl:tl + 16, :]

    ext_ref[16:16 + tl, :] = u_ref[...]
    pos = j * tl + lax.broadcasted_iota(jnp.int32, (tl, POOL_GROUP), 0)
    for g, win in enumerate(POOL_WINDOWS):
        lanes = slice(g * POOL_GROUP, (g + 1) * POOL_GROUP)
        cur = ext_ref[16:16 + tl, lanes]
        acc = cur
        for k in range(1, win):
            acc = acc + ext_ref[16 - k:16 - k + tl, lanes]
        cnt = jnp.minimum(pos + 1, win).astype(F32)
        diff = (acc / cnt - cur).astype(BF16)
        out = _dot(diff, w_ref[g]) * scale_ref[:, lanes]
        o_ref[:, lanes] = out.astype(o_ref.dtype)


def _pool_prompt(u, n_rows, seq_len, pool_w, pool_scale):
    d_pool = u.shape[1]
    tl = _pick_tile(seq_len, (512, 256, 128))
    nj = seq_len // tl
    nb = n_rows // seq_len
    return pl.pallas_call(
        functools.partial(_pool_prompt_kernel, tl=tl),
        grid=(nb, nj),
        in_specs=[pl.BlockSpec((tl, d_pool), lambda b, j: (b * nj + j, 0)),
                  _const_spec(pool_w.shape), _const_spec(pool_scale.shape)],
        out_specs=[pl.BlockSpec((tl, d_pool), lambda b, j: (b * nj + j, 0)),
                   pl.BlockSpec((1, POOL_BUF, d_pool), lambda b, j: (b, 0, 0))],
        out_shape=[jax.ShapeDtypeStruct((n_rows, d_pool), BF16),
                   jax.ShapeDtypeStruct((nb, POOL_BUF, d_pool), F32)],
        scratch_shapes=[pltpu.VMEM((tl + 16, d_pool), F32)],
        compiler_params=pltpu.CompilerParams(dimension_semantics=("arbitrary", "arbitrary")),
        name="pool_prompt",
    )(u, pool_w, pool_scale)


def _pool_sample_kernel(ext_ref, w_ref, scale_ref, o_ref, *, n_new, start):
    for t in range(n_new):
        for g, win in enumerate(POOL_WINDOWS):
            lanes = slice(g * POOL_GROUP, (g + 1) * POOL_GROUP)
            cur = ext_ref[POOL_BUF + t, :, lanes]
            acc = cur
            for k in range(1, win):
                acc = acc + ext_ref[POOL_BUF + t - k, :, lanes]
            cnt = float(min(start + t + 1, win))
            diff = (acc / cnt - cur).astype(BF16)
            out = _dot(diff, w_ref[g]) * scale_ref[:, lanes]
            o_ref[t, :, lanes] = out.astype(o_ref.dtype)


def _pool_sample(ext_t, pool_w, pool_scale, n_new, start):
    rows, bs, d_pool = ext_t.shape
    return pl.pallas_call(
        functools.partial(_pool_sample_kernel, n_new=n_new, start=start),
        grid=(1,),
        in_specs=[_const_spec(ext_t.shape), _const_spec(pool_w.shape), _const_spec(pool_scale.shape)],
        out_specs=_const_spec((n_new, bs, d_pool)),
        out_shape=jax.ShapeDtypeStruct((n_new, bs, d_pool), BF16),
        compiler_params=pltpu.CompilerParams(dimension_semantics=("arbitrary",)),
        name="pool_sample",
    )(ext_t, pool_w, pool_scale)


def _conv_silu(window, w_ref, b_ref):
    acc = b_ref[...] + window(0) * w_ref[0:1, :]
    for k in range(1, CONV_W):
        acc = acc + window(k) * w_ref[k:k + 1, :]
    return _silu(acc)


def _softplus(x):
    return jnp.maximum(x, 0.0) + jnp.log1p(jnp.exp(-jnp.abs(x)))


def _ssd_chunk(xact, sz, dt, prm, seq_rows, read_state, write_state, write_out):
    (cums_ref, spread_ref, alog_ref, dexp_ref, normw_ref) = prm
    q = xact.shape[0]
    d_ssm = sz.shape[1]
    gw = d_ssm // N_GROUPS
    n_seq = q // seq_rows
    xs = xact[:, :d_ssm]
    bm = xact[:, d_ssm:d_ssm + N_GROUPS * D_STATE]
    cm = xact[:, d_ssm + N_GROUPS * D_STATE:]

    a = dt * (-jnp.exp(alog_ref[...]) * LOG2_E)
    sums = _dot(cums_ref[...], jnp.concatenate(_split3(a), axis=0))
    cum, rcum = sums[:q], sums[q:]
    w = jnp.exp2(rcum) * dt
    cum_t = cum.T
    dt_t = dt.T
    two_terms = lambda v: jnp.concatenate(_split3(v)[:2], axis=1)
    spread = _dot(jnp.concatenate([two_terms(w), two_terms(jnp.exp2(cum))], axis=0), spread_ref[...])
    wx = xs * spread[:q]
    ecum_e = spread[q:]

    ii = lax.broadcasted_iota(jnp.int32, (q, q), 0)
    jj = lax.broadcasted_iota(jnp.int32, (q, q), 1)
    mask = ii >= jj
    if n_seq > 1:
        mask = jnp.logical_and(mask, (ii // seq_rows) == (jj // seq_rows))
    lo_half = lax.broadcasted_iota(jnp.int32, (q, LANES), 1) < HEAD_DIM
    col = lax.broadcasted_iota(jnp.int32, (gw, q), 1)

    for g in range(N_GROUPS):
        gcols = slice(g * gw, (g + 1) * gw)
        bg = bm[:, g * D_STATE:(g + 1) * D_STATE].astype(BF16)
        cg = cm[:, g * D_STATE:(g + 1) * D_STATE].astype(BF16)
        cb = _dot_nt(cg, bg)
        ydiag = []
        end_decay = {}
        for pr in range(gw // LANES):
            blk = g * (gw // LANES) + pr
            xp = xs[:, blk * LANES:(blk + 1) * LANES]
            mix = []
            for half in range(2):
                h = 2 * blk + half
                colb = jnp.broadcast_to(cum[:, h:h + 1], (q, q))
                for s in range(n_seq):
                    last = (s + 1) * seq_rows - 1
                    end_decay[h, s] = jnp.exp2(colb[last:last + 1, :])
                dec = jnp.exp2(jnp.where(mask, colb - cum_t[h:h + 1, :], -jnp.inf))
                mix.append((cb * dec * dt_t[h:h + 1, :]).astype(BF16))
            x2 = jnp.concatenate([jnp.where(lo_half, xp, 0.0), jnp.where(lo_half, 0.0, xp)], axis=0).astype(BF16)
            ydiag.append(_dot(jnp.concatenate(mix, axis=1), x2))
        wx_t = wx[:, gcols].T
        yoff_rows = []
        for s in range(n_seq):
            r0 = s * seq_rows
            st = read_state(s, g)
            yoff_rows.append(_dot_nt(cg[r0:r0 + seq_rows, :], st.astype(BF16)))
            scale = jnp.concatenate(
                [jnp.broadcast_to(end_decay[h, s], (HEAD_DIM, D_STATE))
                 for h in range(g * (gw // HEAD_DIM), (g + 1) * (gw // HEAD_DIM))], axis=0)
            wsel = wx_t
            if n_seq > 1:
                wsel = jnp.where(jnp.logical_and(col >= r0, col < r0 + seq_rows), wx_t, 0.0)
            write_state(s, g, st * scale + _dot(wsel.astype(BF16), bg))
        yoff = yoff_rows[0] if n_seq == 1 else jnp.concatenate(yoff_rows, axis=0)
        y = jnp.concatenate(ydiag, axis=1) + yoff * ecum_e[:, gcols] + xs[:, gcols] * dexp_ref[:, gcols]
        gz = y * sz[:, gcols]
        ms = jnp.sum(gz * gz, axis=-1, keepdims=True) * (1.0 / gw)
        write_out(g, gz * lax.rsqrt(ms + RMS_EPS) * normw_ref[:, gcols])


def _ssd_prompt_kernel(*refs, gw, ways):
    ins, prm = refs[:3 * ways], refs[3 * ways:3 * ways + 5]
    outs, h_ref = refs[3 * ways + 5:5 * ways + 5], refs[5 * ways + 5]
    c = pl.program_id(1)

    @pl.when(c == 0)
    def _():
        h_ref[...] = jnp.zeros(h_ref.shape, F32)

    for k in range(ways):
        sz_ref, xact_ref, dt_ref = ins[3 * k:3 * k + 3]
        y_ref = outs[2 * k]

        def read_state(s, g, k=k):
            return h_ref[k, g * gw:(g + 1) * gw, :]

        def write_state(s, g, v, k=k):
            h_ref[k, g * gw:(g + 1) * gw, :] = v

        def write_out(g, v, y_ref=y_ref):
            y_ref[:, g * gw:(g + 1) * gw] = v.astype(y_ref.dtype)

        _ssd_chunk(xact_ref[...], sz_ref[...], dt_ref[...], prm, sz_ref.shape[0], read_state, write_state, write_out)

    @pl.when(c == pl.num_programs(1) - 1)
    def _():
        for k in range(ways):
            outs[2 * k + 1][0] = h_ref[k]


def _ssd_sample_kernel(sz_ref, ext_ref, dt_ref, hin_ref, convw_ref, convb_ref,
                       cums_ref, spread_ref, alog_ref, dexp_ref, normw_ref, *rest, gw, n_new):
    y_ref, hout_ref = rest[-2:]
    q = sz_ref.shape[0]
    n_seq = q // SAMPLE_ROWS
    first = CONV_LEAD - (CONV_W - 1)

    def window(k):
        return jnp.concatenate(
            [ext_ref[s * SAMPLE_EXT_ROWS + first + k:s * SAMPLE_EXT_ROWS + first + k + SAMPLE_ROWS, :]
             for s in range(n_seq)], axis=0)

    xact = _conv_silu(window, convw_ref, convb_ref)
    row = lax.broadcasted_iota(jnp.int32, (q, LANES), 0)
    dt = jnp.where((row % SAMPLE_ROWS) < n_new, dt_ref[...], 0.0)

    def read_state(s, g):
        return hin_ref[s, g * gw:(g + 1) * gw, :]

    def write_state(s, g, v):
        hout_ref[s, g * gw:(g + 1) * gw, :] = v

    prm = (cums_ref, spread_ref, alog_ref, dexp_ref, normw_ref)

    def write_out(g, v):
        y_ref[:, g * gw:(g + 1) * gw] = v.astype(y_ref.dtype)

    _ssd_chunk(xact, sz_ref[...], dt, prm, SAMPLE_ROWS, read_state, write_state, write_out)


def _ssd_consts(seq_rows, n_heads):
    q = CHUNK
    i = jnp.arange(q)[:, None]
    j = jnp.arange(q)[None, :]
    same = (i // seq_rows) == (j // seq_rows)
    tri = jnp.logical_and(same, j <= i).astype(BF16)
    tris = jnp.logical_and(same, j > i).astype(BF16)
    cums = jnp.tile(jnp.concatenate([tri, tris], axis=0), (1, 3))
    hrow = jnp.arange(LANES)[:, None]
    sel64 = (hrow == (jnp.arange(n_heads * HEAD_DIM)[None, :] // HEAD_DIM)).astype(BF16)
    spread = jnp.tile(sel64, (2, 1))
    return cums, spread


def _ssd_param_specs(prm_arrays):
    return [_const_spec(a.shape) for a in prm_arrays]


def _ssd_prompt(sz, xact, dt, n_seq, seq_len, lp):
    d_ssm = sz.shape[1]
    d_xbc = xact.shape[1]
    q = CHUNK
    nc = seq_len // q
    gw = d_ssm // N_GROUPS
    prm = (*lp["ssd_consts_prompt"], lp["a_log"], lp["d_exp"], lp["norm_w"])
    ways = SSD_WAYS if n_seq % SSD_WAYS == 0 else 1
    per = n_seq // ways
    in_specs, out_specs, out_shape = [], [], []
    for k in range(ways):
        rows = lambda b, c, k=k: ((k * per + b) * nc + c, 0)
        in_specs += [pl.BlockSpec((q, d_ssm), rows), pl.BlockSpec((q, d_xbc), rows), pl.BlockSpec((q, LANES), rows)]
        out_specs += [pl.BlockSpec((q, d_ssm), lambda b, c: (b * nc + c, 0)),
                      pl.BlockSpec((1, d_ssm, D_STATE), lambda b, c: (b, 0, 0))]
        out_shape += [jax.ShapeDtypeStruct((per * seq_len, d_ssm), BF16),
                      jax.ShapeDtypeStruct((per, d_ssm, D_STATE), F32)]
    res = pl.pallas_call(
        functools.partial(_ssd_prompt_kernel, gw=gw, ways=ways),
        grid=(per, nc),
        in_specs=in_specs + _ssd_param_specs(prm),
        out_specs=out_specs,
        out_shape=out_shape,
        scratch_shapes=[pltpu.VMEM((ways, d_ssm, D_STATE), F32)],
        compiler_params=pltpu.CompilerParams(dimension_semantics=("arbitrary", "arbitrary"),
                                             vmem_limit_bytes=_vmem_limit(48 * 1024 * 1024)),
        name="ssd_prompt",
    )(*([sz, xact, dt] * ways), *prm)
    return list(res[0::2]), list(res[1::2])


def _ssd_sample(z16, ext, dtr16, h_all, h_new, layer, lp, n_new):
    d_ssm = z16.shape[1]
    d_xbc = ext.shape[1]
    q = CHUNK
    spc = q // SAMPLE_ROWS
    bs = h_all.shape[1]
    gw = d_ssm // N_GROUPS
    prm = (lp["conv_w"], lp["conv_b"], *lp["ssd_consts_sample"], lp["a_log"], lp["d_exp"], lp["norm_w"])
    rows = lambda i: (i, 0)
    slab =pl.BlockSpec((None, spc, d_ssm, D_STATE), lambda i: (layer, i, 0, 0))
    state_bytes = spc * d_ssm * D_STATE * 4
    prev = [] if h_new is None else [h_new]
    n_in = 4 + len(prm)
    return pl.pallas_call(
        functools.partial(_ssd_sample_kernel, gw=gw, n_new=n_new),
        grid=(bs // spc,),
        in_specs=[pl.BlockSpec((q, d_ssm), rows), pl.BlockSpec((spc * SAMPLE_EXT_ROWS, d_xbc), rows),
                  pl.BlockSpec((q, LANES), rows), slab] + _ssd_param_specs(prm)
                 + [pl.BlockSpec(memory_space=pl.ANY) for _ in prev],
        out_specs=[pl.BlockSpec((q, d_ssm), rows), slab],
        out_shape=[jax.ShapeDtypeStruct((bs * SAMPLE_ROWS, d_ssm), BF16),
                   jax.ShapeDtypeStruct(h_all.shape, F32)],
        input_output_aliases={n_in: 1} if prev else {},
        compiler_params=pltpu.CompilerParams(dimension_semantics=("arbitrary",),
                                             vmem_limit_bytes=_vmem_limit(4 * state_bytes + 20 * 1024 * 1024)),
        name="ssd_sample",
    )(z16, ext, dtr16, h_all, *prm, *prev)


def _route(h, rw_ref, rb_ref, tril_ref, cnt_ref, n_experts):
    xh = h.astype(BF16)
    xl = (h - xh.astype(F32)).astype(BF16)
    w = rw_ref[...]
    wh = w.astype(BF16)
    wl = (w - wh.astype(F32)).astype(BF16)
    logits = _dot(xh, wh) + (_dot(xh, wl) + _dot(xl, wh)) + rb_ref[...]
    lane = lax.broadcasted_iota(jnp.int32, logits.shape, 1)
    lane_f = lane.astype(F32)
    logits = jnp.where(lane < n_experts, logits, -jnp.inf)
    m1 = jnp.max(logits, axis=-1, keepdims=True)
    i1 = jnp.min(jnp.where(logits == m1, lane_f, float(LANES)), axis=-1, keepdims=True)
    rest = jnp.where(lane_f == i1, -jnp.inf, logits)
    m2 = jnp.max(rest, axis=-1, keepdims=True)
    i2 = jnp.min(jnp.where(rest == m2, lane_f, float(LANES)), axis=-1, keepdims=True)
    e2 = jnp.exp(m2 - m1)
    den = 1.0 + e2
    oh1 = jnp.where(lane_f == i1, 1.0, 0.0)
    oh2 = jnp.where(lane_f == i2, 1.0, 0.0)
    before1 = _dot(tril_ref[...], oh1.astype(BF16))
    before2 = _dot(tril_ref[...], oh2.astype(BF16))
    c1 = jnp.sum(oh1, axis=0, keepdims=True)
    c2 = jnp.sum(oh2, axis=0, keepdims=True)
    base = cnt_ref[...]
    r1 = jnp.sum(oh1 * (before1 + base), axis=-1, keepdims=True)
    r2 = jnp.sum(oh2 * (before2 + (base + c1)), axis=-1, keepdims=True)
    cnt_ref[...] = base + (c1 + c2)
    gate = jnp.where(lane == 0, 1.0 / den, jnp.where(lane == 1, e2 / den, 0.0))
    plan = jnp.where(lane == 0, i1, jnp.where(lane == 1, i2, jnp.where(lane == 2, r1, jnp.where(lane == 3, r2, 0.0))))
    return gate, plan


def _outproj_ln_kernel(*refs, alpha, n_seg, offs, route, n_experts):
    it = iter(refs)
    take = lambda n: [next(it) for _ in range(n)]
    x_refs, p_refs, s_refs = take(n_seg[0]), take(n_seg[1]), take(n_seg[2])
    wp_ref, ws_ref, g_ref, b_ref = take(4)
    x = _seg_read(x_refs, offs[0])
    mixed = _dot(_seg_read(p_refs, offs[1]), wp_ref[...]) + _dot(_seg_read(s_refs, offs[2]), ws_ref[...])
    h = _layer_norm(alpha * x + mixed, g_ref[...], b_ref[...])
    if not route:
        (o_ref,) = take(1)
        o_ref[...] = h
        return
    rw_ref, rb_ref, tril_ref = take(3)
    xt_ref, gate_ref, plan_ref, cnt_ref = take(4)

    @pl.when(pl.program_id(0) == 0)
    def _():
        cnt_ref[...] = jnp.zeros(cnt_ref.shape, F32)

    tm, d = h.shape
    rpt = d // LANES
    for k in range(rpt):
        xt_ref[pl.ds(k, tm, stride=rpt), :] = h[:, k * LANES:(k + 1) * LANES]
    gate, plan = _route(h, rw_ref, rb_ref, tril_ref, cnt_ref, n_experts)
    gate_ref[...] = gate
    plan_ref[...] = plan.T[:plan_ref.shape[0], :]


def _outproj_ln(x_segs, pool_segs, ssd_segs, w_pool, w_ssd, g, b, alpha, router=None):
    d = x_segs[0].shape[1]
    rows = [a.shape[0] for a in x_segs + pool_segs + ssd_segs]
    tm = _common_tile(rows, (512, 256, 128, 64))
    groups = (x_segs, pool_segs, ssd_segs)
    tiles = [_seg_tiles(s, tm) for s in groups]
    t = sum(tiles[0]) * tm
    assert all(sum(ts) * tm == t for ts in tiles)
    in_specs = [sp for s in groups for sp in _seg_specs(s, tm)]
    consts = [w_pool, w_ssd, g, b]
    row = lambda w, dt: (pl.BlockSpec((tm, w), lambda i: (i, 0)), jax.ShapeDtypeStruct((t, w), dt))
    if router is None:
        outs = [row(d, F32)]
        n_experts = 0
    else:
        rw_pad, rb_pad, n_experts = router
        tril = (jnp.arange(tm)[:, None] > jnp.arange(tm)[None, :]).astype(BF16)
        consts += [rw_pad, rb_pad, tril]
        rpt = d // LANES
        outs = [(pl.BlockSpec((tm * rpt, LANES), lambda i: (i, 0)), jax.ShapeDtypeStruct((t * rpt, LANES), F32)),
                row(LANES, F32),
                (pl.BlockSpec((SUBLANES, tm), lambda i: (0, i)), jax.ShapeDtypeStruct((SUBLANES, t), F32)),
                (_const_spec((1, LANES)), jax.ShapeDtypeStruct((1, LANES), F32))]
    in_specs += [_const_spec(c.shape) for c in consts]
    res = pl.pallas_call(
        functools.partial(_outproj_ln_kernel, alpha=alpha, n_seg=[len(s) for s in groups],
                          offs=[_seg_offsets(ts) for ts in tiles], route=router is not None, n_experts=n_experts),
        grid=(t // tm,),
        in_specs=in_specs,
        out_specs=[o[0] for o in outs],
        out_shape=[o[1] for o in outs],
        compiler_params=pltpu.CompilerParams(dimension_semantics=("arbitrary",),
                                             vmem_limit_bytes=_vmem_limit(40 * 1024 * 1024)),
        name="outproj_ln",
    )(*x_segs, *pool_segs, *ssd_segs, *consts)
    return res[0] if router is None else res


def _swiglu(xb, wg_ref, wu_ref, wd_ref):
    act = (_silu(_dot(xb, wg_ref[0])) * _dot(xb, wu_ref[0])).astype(BF16)
    return _dot(act, wd_ref[0])


def _dense_ffn_ln_kernel(*refs, alpha, offs, tiles, n_side):
    x_ref, wg_ref, wu_ref, wd_ref, g_ref, b_ref = refs[:6]
    side_in = refs[6:6 + n_side]
    out_refs = refs[6 + n_side:6 + n_side + len(tiles)]
    side_out = refs[6 + n_side + len(tiles):]
    x = x_ref[...]
    f = _swiglu(x.astype(BF16), wg_ref, wu_ref, wd_ref)
    _seg_write(out_refs, offs, tiles, _layer_norm(alpha * x + f, g_ref[...], b_ref[...]))
    for src, dst in zip(side_in, side_out):
        dst[...] = src[...].astype(dst.dtype)


def _row_blocks(rows, steps, align):
    for n in range(min(steps, rows), 0, -1):
        if rows % n == 0 and (rows // n) % align == 0:
            return n
    raise ValueError(f"cannot split {rows} rows into at most {steps} blocks of a multiple of {align} rows")


def _dense_ffn_ln(x, wg, wu, wd, g, b, alpha, out_rows, side_casts=()):
    t, d = x.shape
    ff = wg.shape[2]
    tm = _common_tile(out_rows, (256, 128, 64))
    tiles = [n // tm for n in out_rows]
    offs = _seg_offsets(tiles)
    steps = t // tm
    side2d = [a.reshape(-1, a.shape[-1]) for a in side_casts]
    side_specs = []
    for a in side2d:
        nb = _row_blocks(a.shape[0], steps, 2 * SUBLANES)
        side_specs.append(pl.BlockSpec((a.shape[0] // nb, a.shape[1]), lambda i, nb=nb: (jnp.minimum(i, nb - 1), 0)))
    side_bytes = sum(2 * (4 + 2) * s.block_shape[0] * s.block_shape[1] for s in side_specs)
    vmem = 3 * d * ff * 2 + 4 * tm * d * 4 + tm * ff * 12 + side_bytes
    res = pl.pallas_call(
        functools.partial(_dense_ffn_ln_kernel, alpha=alpha, offs=offs, tiles=tiles, n_side=len(side2d)),
        grid=(steps,),
        in_specs=[pl.BlockSpec((tm, d), lambda i: (i, 0)), _const_spec(wg.shape, single=True),
                  _const_spec(wu.shape, single=True), _const_spec(wd.shape, single=True),
                  _const_spec(g.shape), _const_spec(b.shape)] + side_specs,
        out_specs=[pl.BlockSpec((tm, d), lambda i, o=o, n=n: (jnp.clip(i - o, 0, n - 1), 0))
                   for o, n in zip(offs, tiles)] + side_specs,
        out_shape=[jax.ShapeDtypeStruct((n, d), F32) for n in out_rows]
                  + [jax.ShapeDtypeStruct(a.shape, BF16) for a in side2d],
        compiler_params=pltpu.CompilerParams(dimension_semantics=("arbitrary",),
                                             vmem_limit_bytes=_vmem_limit(vmem)),
        name="dense_ffn_ln",
    )(x, wg, wu, wd, g, b, *side2d)
    return list(res[:len(tiles)]), [r.reshape(a.shape) for r, a in zip(res[len(tiles):], side_casts)]


def _dispatch_kernel(ends_ref, slot_ref, x_ref, out_hbm, zeros_ref, sem, zsem, *, tokens, rpt, tile, n_experts):
    i = pl.program_id(0)

    def tail_copy(e):
        start = pl.multiple_of((ends_ref[e] - tile) * rpt, tile * rpt)
        return pltpu.make_async_copy(zeros_ref, out_hbm.at[pl.ds(start, tile * rpt)], zsem)

    def nonempty(e):
        return ends_ref[e] > (ends_ref[e - 1] if e > 0 else 0)

    def unused_copy(j):
        start = pl.multiple_of((ends_ref[n_experts - 1] + j * tile) * rpt, tile * rpt)
        return pltpu.make_async_copy(zeros_ref, out_hbm.at[pl.ds(start, tile * rpt)], zsem)

    def unused(j):
        return ends_ref[n_experts - 1] + (j + 1) * tile <= out_hbm.shape[0] // rpt

    @pl.when(i == 0)
    def _():
        zeros_ref[...] = jnp.zeros(zeros_ref.shape, zeros_ref.dtype)
        for e in range(n_experts):
            @pl.when(nonempty(e))
            def _(e=e):
                tail_copy(e).start()

            @pl.when(unused(e))
            def _(e=e):
                unused_copy(e).start()
        for e in range(n_experts):
            @pl.when(nonempty(e))
            def _(e=e):
                tail_copy(e).wait()

            @pl.when(unused(e))
            def _(e=e):
                unused_copy(e).wait()

    def issue(r, carry):
        src = x_ref.at[pl.ds(pl.multiple_of(r * rpt, rpt), rpt)]
        for k in range(TOP_K):
            dst = out_hbm.at[pl.ds(pl.multiple_of(slot_ref[0, 0, k * tokens + r] * rpt, rpt), rpt)]
            pltpu.make_async_copy(src, dst, sem).start()
        return carry

    lax.fori_loop(0, tokens, issue, 0)
    for k in range(TOP_K):
        pltpu.make_async_copy(x_ref, out_hbm.at[pl.ds(0, tokens * rpt)], sem).wait()


def _dispatch(xt, slots, ends, n_slots, tokens, rpt, n_experts):
    steps = slots.shape[0]
    return pl.pallas_call(
        functools.partial(_dispatch_kernel, tokens=tokens, rpt=rpt, tile=MOE_TILE, n_experts=n_experts),
        grid_spec=pltpu.PrefetchScalarGridSpec(
            num_scalar_prefetch=1, grid=(steps,),
            in_specs=[pl.BlockSpec((1, 1, TOP_K * tokens), lambda i, e: (i, 0, 0), memory_space=pltpu.SMEM),
                      pl.BlockSpec((tokens * rpt, LANES), lambda i, e: (i, 0))],
            out_specs=pl.BlockSpec(memory_space=pl.ANY),
            scratch_shapes=[pltpu.VMEM((MOE_TILE * rpt, LANES), xt.dtype),
                            pltpu.SemaphoreType.DMA(()), pltpu.SemaphoreType.DMA(())]),
        out_shape=jax.ShapeDtypeStruct((n_slots * rpt, LANES), xt.dtype),
        compiler_params=pltpu.CompilerParams(dimension_semantics=("arbitrary",)),
        name="moe_dispatch",
    )(ends, slots, xt)


def _moe_ffn_kernel(te_ref, na_ref, x_ref, wg_ref, wu_ref, wd_ref, o_ref, *, tm, rpt):
    active = pl.program_id(0) < na_ref[0]

    @pl.when(active)
    def _():
        xb = jnp.concatenate([x_ref[pl.ds(k, tm, stride=rpt), :] for k in range(rpt)], axis=1).astype(BF16)
        f = _swiglu(xb, wg_ref, wu_ref, wd_ref)
        for k in range(rpt):
            o_ref[pl.ds(k, tm, stride=rpt), :] = f[:, k * LANES:(k + 1) * LANES]

    @pl.when(jnp.logical_not(active))
    def _():
        o_ref[...] = jnp.zeros(o_ref.shape, o_ref.dtype)


def _moe_ffn(xs, wg, wu, wd, tile_expert, n_active, rpt):
    tm = MOE_TILE
    d, ff = wg.shape[1], wg.shape[2]
    n_tiles = xs.shape[0] // (tm * rpt)
    w_idx = lambda i, te, na: (te[i], 0, 0)
    vmem = 2 * 3 * d * ff * 2 + 4 * tm * d * 4 + tm * ff * 12 + 2 * tm * d * 4
    return pl.pallas_call(
        functools.partial(_moe_ffn_kernel, tm=tm, rpt=rpt),
        grid_spec=pltpu.PrefetchScalarGridSpec(
            num_scalar_prefetch=2, grid=(n_tiles,),
            in_specs=[pl.BlockSpec((tm * rpt, LANES), lambda i, te, na: (jnp.minimum(i, na[0] - 1), 0)),
                      pl.BlockSpec((1, d, ff), w_idx), pl.BlockSpec((1, d, ff), w_idx),
                      pl.BlockSpec((1, ff, d), w_idx)],
            out_specs=pl.BlockSpec((tm * rpt, LANES), lambda i, te, na: (i, 0))),
        out_shape=jax.ShapeDtypeStruct(xs.shape, F32),
        compiler_params=pltpu.CompilerParams(dimension_semantics=("arbitrary",),
                                             vmem_limit_bytes=_vmem_limit(vmem)),
        name="moe_ffn",
    )(tile_expert, n_active, xs, wg, wu, wd)


def _combine_ln_kernel(slot_ref, next_ref, x_ref, gate_ref, y_hbm, g_ref, b_ref, *rest, alpha, tm, rpt, offs, tiles):
    out_refs, (buf, sem) = rest[:len(tiles)], rest[len(tiles):]
    i = pl.program_id(0)
    n_rows = TOP_K * tm * rpt

    def issue(s_ref, slot):
        def body(r, carry):
            src = y_hbm.at[pl.ds(pl.multiple_of(s_ref[0, 0, r] * rpt, rpt), rpt)]
            pltpu.make_async_copy(src, buf.at[slot, pl.ds(pl.multiple_of(r * rpt, rpt), rpt)], sem.at[slot]).start()
            return carry
        lax.fori_loop(0, TOP_K * tm, body, 0)

    @pl.when(i == 0)
    def _():
        issue(slot_ref, 0)

    @pl.when(i + 1 < pl.num_programs(0))
    def _():
        issue(next_ref, (i + 1) % 2)

    cur = i % 2
    pltpu.make_async_copy(y_hbm.at[pl.ds(0, n_rows)], buf.at[cur], sem.at[cur]).wait()
    g1 = gate_ref[:, 0:1]
    g2 = gate_ref[:, 1:2]
    cols = []
    for k in range(rpt):
        xk = x_ref[pl.ds(k, tm, stride=rpt), :]
        ya = buf[cur, pl.ds(k, tm, stride=rpt), :]
        yb = buf[cur, pl.ds(tm * rpt + k, tm, stride=rpt), :]
        cols.append(alpha * xk + (g1 * ya + g2 * yb))
    h = jnp.concatenate(cols, axis=1)
    _seg_write(out_refs, offs, tiles, _layer_norm(h, g_ref[...], b_ref[...]))


def _combine_ln(xt, gate, ys, slots, g, b, alpha, tm, rpt, out_rows):
    d = g.shape[1]
    tiles = [n // tm for n in out_rows]
    offs = _seg_offsets(tiles)
    steps = slots.shape[0]
    slot_spec = lambda f: pl.BlockSpec((1, 1, TOP_K * tm), f, memory_space=pltpu.SMEM)
    res = pl.pallas_call(
        functools.partial(_combine_ln_kernel, alpha=alpha, tm=tm, rpt=rpt, offs=offs, tiles=tiles),
        grid=(steps,),
        in_specs=[slot_spec(lambda i: (i, 0, 0)), slot_spec(lambda i: (jnp.minimum(i + 1, steps - 1), 0, 0)),
                  pl.BlockSpec((tm * rpt, LANES), lambda i: (i, 0)), pl.BlockSpec((tm, LANES), lambda i: (i, 0)),
                  pl.BlockSpec(memory_space=pl.ANY), _const_spec(g.shape), _const_spec(b.shape)],
        out_specs=[pl.BlockSpec((tm, d), lambda i, o=o, n=n: (jnp.clip(i - o, 0, n - 1), 0))
                   for o, n in zip(offs, tiles)],
        out_shape=[jax.ShapeDtypeStruct((n, d), F32) for n in out_rows],
        scratch_shapes=[pltpu.VMEM((2, TOP_K * tm * rpt, LANES), F32), pltpu.SemaphoreType.DMA((2,))],
        compiler_params=pltpu.CompilerParams(dimension_semantics=("arbitrary",),
                                             vmem_limit_bytes=_vmem_limit(32 * 1024 * 1024)),
        name="moe_combine_ln",
    )(slots, slots, xt, gate, ys, g, b)
    return list(res)


def _moe_ffn_ln(xt, gate, plan, counts, wg, wu, wd, g, b, alpha, tm, out_rows):
    n_experts = wg.shape[0]
    d = wg.shape[1]
    rpt = d // LANES
    t = gate.shape[0]
    tile = MOE_TILE
    cnt = counts[0, :n_experts].astype(jnp.int32)
    padded = ((cnt + tile - 1) // tile) * tile
    ends = jnp.cumsum(padded).astype(jnp.int32)
    starts = ends - padded
    plan = plan.astype(jnp.int32)
    slot = jnp.take(starts, plan[:TOP_K]) + plan[TOP_K:2 * TOP_K]
    n_slots = -(-(TOP_K * t + n_experts * (tile - 1)) // tile) * tile
    n_tiles = n_slots // tile
    n_active = ends[-1:] // tile
    tile_start = jnp.minimum(jnp.arange(n_tiles, dtype=jnp.int32), n_active[0] - 1) * tile
    tile_expert = jnp.sum((tile_start[:, None] >= ends[None, :]).astype(jnp.int32), axis=1)
    slots = jnp.swapaxes(slot.reshape(TOP_K, t // tm, tm), 0, 1).reshape(t // tm, 1, TOP_K * tm)
    xs = _dispatch(xt, slots, ends, n_slots, tm, rpt, n_experts)
    ys = _moe_ffn(xs, wg, wu, wd, tile_expert, n_active, rpt)
    return _combine_ln(xt, gate, ys, slots, g, b, alpha, tm, rpt, out_rows)


def kernel(x_prompt, x_sample, state_pool, state_conv, state_ssm, w_in, conv_w, conv_b, dt_bias, A_log, D_skip,
           ssm_norm_w, pool_w, pool_scale, w_out, ln1_g, ln1_b, ln2_g, ln2_b, ffn_w_gate, ffn_w_up, ffn_w_down,
           router_w, router_b, moe_w_gate, moe_w_up, moe_w_down):
    bp, seq, d = x_prompt.shape
    bs, n_new, _ = x_sample.shape
    depth = w_in.shape[0]
    d_pool = pool_scale.shape[1]
    d_ssm = ssm_norm_w.shape[1]
    d_xbc = conv_w.shape[2]
    n_heads = dt_bias.shape[1]
    tp, ts = bp * seq, bs * n_new
    alpha = (2.0 * depth) ** 0.25
    assert d_pool == POOL_GROUP * len(POOL_WINDOWS) and d_ssm == n_heads * HEAD_DIM
    assert d_xbc == d_ssm + 2 * N_GROUPS * D_STATE and seq % CHUNK == 0 and n_new <= CONV_W
    assert n_heads <= LANES and bs % (CHUNK // SAMPLE_ROWS) == 0 and d % LANES == 0

    consts_prompt = _ssd_consts(CHUNK, n_heads)
    consts_sample = _ssd_consts(SAMPLE_ROWS, n_heads)
    x_segs = [x_prompt.reshape(tp, d), x_sample.reshape(ts, d)]
    new_pool_p, new_conv_p, new_ssm_p, new_pool_s, new_conv_s = [], [], [], [], []
    ssm_s = None
    for l in range(depth):
        out_rows = [tp, ts]
        n_in = w_in.shape[2]
        w_in_pad = jnp.pad(w_in[l].astype(BF16), ((0, 0), (0, d_pool + d_ssm + d_xbc + LANES - n_in)))
        lp = dict(
            conv_w=conv_w[l], conv_b=conv_b[l][None, :],
            dt_bias=jnp.zeros((1, LANES), F32).at[0, :n_heads].set(dt_bias[l]),
            a_log=jnp.zeros((1, LANES), F32).at[0, :n_heads].set(A_log[l]),
            d_exp=jnp.repeat(D_skip[l], HEAD_DIM)[None, :], norm_w=ssm_norm_w[l][None, :],
            ssd_consts_prompt=consts_prompt, ssd_consts_sample=consts_sample)
        pw = pool_w[l].astype(BF16)
        ps = pool_scale[l][None, :]

        proj = (w_in_pad, lp["conv_w"], lp["conv_b"], lp["dt_bias"], d_pool, d_ssm, d_xbc)

        u_p, sz_p, xact_p, dt_p, c_p = _in_proj(x_segs[0], *proj, seq_len=seq)
        pool_p, buf_p = _pool_prompt(u_p, tp, seq, pw, ps)
        y_p, h_p = _ssd_prompt(sz_p, xact_p, dt_p, bp, seq, lp)
        new_pool_p.append(buf_p)
        new_conv_p.append(c_p)
        new_ssm_p.append(jnp.concatenate(h_p, axis=0).reshape(bp, n_heads, HEAD_DIM, D_STATE))

        u_s, sz_s, xbc_s, dt_s = _in_proj_sample(x_segs[1], w_in_pad, lp["dt_bias"], d_pool, d_ssm, d_xbc)
        u_s = u_s.reshape(bs, n_new, d_pool)
        xbc_s = xbc_s.reshape(bs, n_new, d_xbc)
        pool_ext = jnp.concatenate([state_pool[l], u_s], axis=1)
        pool_s = _pool_sample(jnp.swapaxes(pool_ext, 0, 1), pw, ps, n_new, PAST_LEN)
        pool_s = jnp.swapaxes(pool_s, 0, 1).reshape(ts, d_pool)
        conv_ext = jnp.concatenate([state_conv[l], xbc_s], axis=1)
        lead = CONV_LEAD - (CONV_W - 1)
        ext = jnp.pad(conv_ext, ((0, 0), (lead, SAMPLE_EXT_ROWS - lead - conv_ext.shape[1]), (0, 0)))
        pad_rows = lambda a: jnp.pad(a.reshape(bs, n_new, -1), ((0, 0), (0, SAMPLE_ROWS - n_new), (0, 0))
                                     ).reshape(bs * SAMPLE_ROWS, -1)
        y_s16, ssm_s = _ssd_sample(pad_rows(sz_s), ext.reshape(bs * SAMPLE_EXT_ROWS, d_xbc), pad_rows(dt_s),
                                   state_ssm.reshape(depth, bs, d_ssm, D_STATE), ssm_s, l, lp, n_new)
        y_s = y_s16.reshape(bs, SAMPLE_ROWS, d_ssm)[:, :n_new].reshape(ts, d_ssm)
        new_pool_s.append(pool_ext[:, n_new:])
        new_conv_s.append(conv_ext[:, n_new:])

        w_o = w_out[l].astype(BF16)
        ln1 = (ln1_g[l][None, :], ln1_b[l][None, :])
        g2, b2 = ln2_g[l][None, :], ln2_b[l][None, :]
        j = l // 2
        if l % 2 == 0:
            x1 = _outproj_ln(x_segs, [pool_p, pool_s], [*y_p, y_s], w_o[:d_pool], w_o[d_pool:], *ln1, alpha)
            side = (moe_w_gate[j], moe_w_up[j], moe_w_down[j]) if l + 1 < depth else ()
            x_segs, moe_bf16 = _dense_ffn_ln(x1, ffn_w_gate[j:j + 1].astype(BF16), ffn_w_up[j:j + 1].astype(BF16),
                                             ffn_w_down[j:j + 1].astype(BF16), g2, b2, alpha, out_rows, side)
        else:
            n_experts = router_w.shape[2]
            rw_pad = jnp.zeros((d, LANES), F32).at[:, :n_experts].set(router_w[j])
            rb_pad = jnp.zeros((1, LANES), F32).at[0, :n_experts].set(router_b[j])
            xt, gate, plan, counts = _outproj_ln(x_segs, [pool_p, pool_s], [*y_p, y_s], w_o[:d_pool], w_o[d_pool:],
                                                 *ln1, alpha, router=(rw_pad, rb_pad, n_experts))
            tm = _common_tile([tp, ts], (512, 256, 128, 64))
            x_segs = _moe_ffn_ln(xt, gate, plan, counts, *moe_bf16, g2, b2, alpha, tm, out_rows)

    y_prompt, y_sample = x_segs
    return (y_prompt.reshape(bp, seq, d), y_sample.reshape(bs, n_new, d),
            jnp.stack(new_pool_p), jnp.stack(new_conv_p), jnp.stack(new_ssm_p),
            jnp.stack(new_pool_s), jnp.stack(new_conv_s),
            ssm_s.reshape(depth, bs, n_heads, HEAD_DIM, D_STATE))
```

```python
import functools

import jax
import jax.numpy as jnp
from jax import lax
from jax.experimental import pallas as pl
from jax.experimental.pallas import tpu as pltpu

F32 = jnp.float32
BF16 = jnp.bfloat16

PAST_LEN = 16384
POOL_WINDOWS = (2, 4, 8, 16)
POOL_GROUP = 128
POOL_BUF = max(POOL_WINDOWS) - 1
HEAD_DIM = 64
N_GROUPS = 4
D_STATE = 128
CONV_W = 4
CHUNK = 128
TOP_K = 2
LN_EPS = 1e-5
RMS_EPS = 1e-6
LOG2_E = 1.4426950408889634

LANES = 128
SUBLANES = 8
V7X_VMEM_BYTES = 64 * 1024 * 1024
VMEM_CAP = V7X_VMEM_BYTES - 8 * 1024 * 1024

SAMPLE_ROWS = 16
SAMPLE_EXT_ROWS = 24
CONV_LEAD = 8
MOE_TILE = 256
SSD_WAYS = 2


def _vmem_limit(nbytes):
    return int(min(VMEM_CAP, nbytes + 6 * 1024 * 1024))


def _dot(a, b):
    return jnp.dot(a, b, preferred_element_type=F32)


def _dot_nt(a, b):
    return lax.dot_general(a, b, (((1,), (1,)), ((), ())), preferred_element_type=F32)


def _split3(v):
    hi = v.astype(BF16)
    r = v - hi.astype(F32)
    mid = r.astype(BF16)
    lo = (r - mid.astype(F32)).astype(BF16)
    return hi, mid, lo


def _sel_right(parts, m):
    out = _dot(parts[0], m)
    for p in parts[1:]:
        out = out + _dot(p, m)
    return out


def _sel_left(m, parts):
    out = _dot(m, parts[0])
    for p in parts[1:]:
        out = out + _dot(m, p)
    return out


def _silu(x):
    return x / (1.0 + jnp.exp2(x * (-LOG2_E)))


def _layer_norm(h, g, b):
    mu = jnp.mean(h, axis=-1, keepdims=True)
    d = h - mu
    var = jnp.mean(d * d, axis=-1, keepdims=True)
    return d * lax.rsqrt(var + LN_EPS) * g + b


def _pick_tile(n, candidates):
    for c in candidates:
        if n % c == 0:
            return c
    raise ValueError(f"no tile in {candidates} divides {n}")


def _const_spec(shape, single=False):
    nd = len(shape)
    if single:
        return pl.BlockSpec(shape, lambda *_: (0,) * nd, pipeline_mode=pl.Buffered(1))
    return pl.BlockSpec(shape, lambda *_: (0,) * nd)


def _seg_tiles(segs, tm, rows_per_token=1):
    return [a.shape[0] // (tm * rows_per_token) for a in segs]


def _seg_offsets(tiles):
    offs, off = [], 0
    for n in tiles:
        offs.append(off)
        off += n
    return offs


def _seg_specs(segs, tm, rows_per_token=1):
    tiles = _seg_tiles(segs, tm, rows_per_token)
    return [pl.BlockSpec((tm * rows_per_token, a.shape[1]),
                         lambda i, *_, o=o, n=n: (jnp.clip(i - o, 0, n - 1), 0))
            for a, o, n in zip(segs, _seg_offsets(tiles), tiles)]


def _seg_read(refs, offs):
    v = refs[0][...]
    for r, o in zip(refs[1:], offs[1:]):
        v = jnp.where(pl.program_id(0) >= o, r[...], v)
    return v


def _seg_write(refs, offs, tiles, v):
    i = pl.program_id(0)
    for r, o, n in zip(refs, offs, tiles):
        @pl.when(jnp.logical_and(i >= o, i < o + n))
        def _(r=r):
            r[...] = v.astype(r.dtype)


def _common_tile(row_counts, candidates):
    for c in candidates:
        if all(n % c == 0 for n in row_counts):
            return c
    raise ValueError(f"no tile in {candidates} divides all of {row_counts}")


INPROJ_CHUNK = 512


def _inproj_kernel(x_ref, w_ref, convw_ref, convb_ref, dtb_ref, u_ref, sz_ref, xo_ref, dt_ref, *rest,
                   splits, tiles_per_seq):
    (u0, u1), (z0, z1), (c0, c1), (d0, d1) = splits
    tm = x_ref.shape[0]
    xb = x_ref[...].astype(BF16)
    u_ref[...] = _dot(xb, w_ref[:, u0:u1])
    dt_ref[...] = _softplus(_dot(xb, w_ref[:, d0:d1]) + dtb_ref[...])
    for lo in range(z0, z1, INPROJ_CHUNK):
        sz_ref[:, lo - z0:lo - z0 + INPROJ_CHUNK] = _silu(_dot(xb, w_ref[:, lo:lo + INPROJ_CHUNK]))

    cstate_ref, ext_ref = rest
    pos = pl.program_id(0) % tiles_per_seq

    @pl.when(pos == 0)
    def _():
        ext_ref[0:CONV_LEAD, :] = jnp.zeros((CONV_LEAD, ext_ref.shape[1]), F32)

    @pl.when(pos > 0)
    def _():
        ext_ref[0:CONV_LEAD, :] = ext_ref[tm:tm + CONV_LEAD, :]

    first = CONV_LEAD - (CONV_W - 1)
    for lo in range(c0, c1, INPROJ_CHUNK):
        cols = slice(lo - c0, lo - c0 + INPROJ_CHUNK)
        ext_ref[CONV_LEAD:CONV_LEAD + tm, cols] = _dot(xb, w_ref[:, lo:lo + INPROJ_CHUNK])
        rows = ext_ref[:, cols]
        acc = convb_ref[:, cols] + (pltpu.roll(rows, CONV_W - 1, axis=0)[CONV_LEAD:] * convw_ref[0:1, cols])
        for k in range(1, CONV_W - 1):
            acc = acc + pltpu.roll(rows, CONV_W - 1 - k, axis=0)[CONV_LEAD:] * convw_ref[k:k + 1, cols]
        acc = acc + rows[CONV_LEAD:] * convw_ref[CONV_W - 1:CONV_W, cols]
        xo_ref[:, cols] = _silu(acc)

    @pl.when(pos == tiles_per_seq - 1)
    def _():
        cstate_ref[0] = ext_ref[CONV_LEAD + tm - (CONV_W - 1):CONV_LEAD + tm, :]


def _inproj_splits(w_pad, d_pool, d_ssm, d_xbc):
    assert d_ssm % INPROJ_CHUNK == 0 and d_xbc % INPROJ_CHUNK == 0
    return ((0, d_pool), (d_pool, d_pool + d_ssm), (d_pool + d_ssm, d_pool + d_ssm + d_xbc),
            (d_pool + d_ssm + d_xbc, w_pad.shape[1]))


def _in_proj(x, w_pad, conv_w, conv_b, dt_bias, d_pool, d_ssm, d_xbc, seq_len):
    t, d = x.shape
    n = w_pad.shape[1]
    tm = _pick_tile(seq_len, (512, 256, 128, 64))
    splits = _inproj_splits(w_pad, d_pool, d_ssm, d_xbc)
    widths = [hi - lo for lo, hi in splits]
    tiles_per_seq = seq_len // tm
    vmem = 2 * tm * d * 4 + d * n * 2 + 2 * tm * n * 4 + 2 * tm * d_xbc * 4 + 4 * tm * INPROJ_CHUNK * 4
    return pl.pallas_call(
        functools.partial(_inproj_kernel, splits=splits, tiles_per_seq=tiles_per_seq),
        grid=(t // tm,),
        in_specs=[pl.BlockSpec((tm, d), lambda i: (i, 0)), _const_spec((d, n), single=True),
                  _const_spec(conv_w.shape), _const_spec(conv_b.shape), _const_spec(dt_bias.shape)],
        out_specs=[pl.BlockSpec((tm, w), lambda i: (i, 0)) for w in widths]
                  + [pl.BlockSpec((1, CONV_W - 1, d_xbc), lambda i: (i // tiles_per_seq, 0, 0))],
        out_shape=[jax.ShapeDtypeStruct((t, w), F32) for w in widths]
                  + [jax.ShapeDtypeStruct((t // seq_len, CONV_W - 1, d_xbc), F32)],
        scratch_shapes=[pltpu.VMEM((tm + CONV_LEAD, d_xbc), F32)],
        compiler_params=pltpu.CompilerParams(dimension_semantics=("arbitrary",),
                                             vmem_limit_bytes=_vmem_limit(vmem)),
        name="in_proj",
    )(x, w_pad, conv_w, conv_b, dt_bias)


def _inproj_sample_kernel(x_ref, w_ref, dtb_ref, u_ref, sz_ref, xbc_ref, dt_ref, *, splits):
    (u0, u1), (z0, z1), (c0, c1), (d0, d1) = splits
    xb = x_ref[...].astype(BF16)
    u_ref[...] = _dot(xb, w_ref[:, u0:u1])
    sz_ref[...] = _silu(_dot(xb, w_ref[:, z0:z1]))
    xbc_ref[...] = _dot(xb, w_ref[:, c0:c1])
    dt_ref[...] = _softplus(_dot(xb, w_ref[:, d0:d1]) + dtb_ref[...])


def _in_proj_sample(x, w_pad, dt_bias, d_pool, d_ssm, d_xbc):
    t, d = x.shape
    n = w_pad.shape[1]
    tm = _pick_tile(t, (512, 256, 128, 64))
    splits = _inproj_splits(w_pad, d_pool, d_ssm, d_xbc)
    widths = [hi - lo for lo, hi in splits]
    vmem = 2 * tm * d * 4 + 2 * d * n * 2 + 4 * tm * n * 4
    return pl.pallas_call(
        functools.partial(_inproj_sample_kernel, splits=splits),
        grid=(t // tm,),
        in_specs=[pl.BlockSpec((tm, d), lambda i: (i, 0)), _const_spec((d, n)), _const_spec(dt_bias.shape)],
        out_specs=[pl.BlockSpec((tm, w), lambda i: (i, 0)) for w in widths],
        out_shape=[jax.ShapeDtypeStruct((t, w), F32) for w in widths],
        compiler_params=pltpu.CompilerParams(dimension_semantics=("arbitrary",),
                                             vmem_limit_bytes=_vmem_limit(vmem)),
        name="in_proj_sample",
    )(x, w_pad, dt_bias)


def _pool_prompt_kernel(u_ref, w_ref, scale_ref, o_ref, buf_ref, ext_ref, *, tl):
    j = pl.program_id(1)

    @pl.when(j == pl.num_programs(1) - 1)
    def _():
        buf_ref[0] = u_ref[tl - POOL_BUF:tl, :]

    @pl.when(j == 0)
    def _():
        ext_ref[0:16, :] = jnp.zeros((16, ext_ref.shape[1]), F32)

    @pl.when(j > 0)
    def _():
        ext_ref[0:16, :] = ext_ref[tl:tl + 16, :]

    ext_ref[16:16 + tl, :] = u_ref[...]
    pos = j * tl + lax.broadcasted_iota(jnp.int32, (tl, POOL_GROUP), 0)
    for g, win in enumerate(POOL_WINDOWS):
        lanes = slice(g * POOL_GROUP, (g + 1) * POOL_GROUP)
        rows = ext_ref[:, lanes]
        cur = rows[16:]
        span = 1
        while span < win:
            rows = rows + pltpu.roll(rows, span, axis=0)
            span *= 2
        acc = rows[16:]
        cnt = jnp.minimum(pos + 1, win).astype(F32)
        diff = (acc / cnt - cur).astype(BF16)
        out = _dot(diff, w_ref[g]) * scale_ref[:, lanes]
        o_ref[:, lanes] = out.astype(o_ref.dtype)


def _pool_prompt(u, n_rows, seq_len, pool_w, pool_scale):
    d_pool = u.shape[1]
    tl = _pick_tile(seq_len, (512, 256, 128))
    nj = seq_len // tl
    nb = n_rows // seq_len
    return pl.pallas_call(
        functools.partial(_pool_prompt_kernel, tl=tl),
        grid=(nb, nj),
        in_specs=[pl.BlockSpec((tl, d_pool), lambda b, j: (b * nj + j, 0)),
                  _const_spec(pool_w.shape), _const_spec(pool_scale.shape)],
        out_specs=[pl.BlockSpec((tl, d_pool), lambda b, j: (b * nj + j, 0)),
                   pl.BlockSpec((1, POOL_BUF, d_pool), lambda b, j: (b, 0, 0))],
        out_shape=[jax.ShapeDtypeStruct((n_rows, d_pool), BF16),
                   jax.ShapeDtypeStruct((nb, POOL_BUF, d_pool), F32)],
        scratch_shapes=[pltpu.VMEM((tl + 16, d_pool), F32)],
        compiler_params=pltpu.CompilerParams(dimension_semantics=("arbitrary", "arbitrary")),
        name="pool_prompt",
    )(u, pool_w, pool_scale)


def _pool_sample_kernel(ext_ref, w_ref, scale_ref, o_ref, *, n_new, start):
    for t in range(n_new):
        for g, win in enumerate(POOL_WINDOWS):
            lanes = slice(g * POOL_GROUP, (g + 1) * POOL_GROUP)
            cur = ext_ref[POOL_BUF + t, :, lanes]
            acc = cur
            for k in range(1, win):
                acc = acc + ext_ref[POOL_BUF + t - k, :, lanes]
            cnt = float(min(start + t + 1, win))
            diff = (acc / cnt - cur).astype(BF16)
            out = _dot(diff, w_ref[g]) * scale_ref[:, lanes]
            o_ref[t, :, lanes] = out.astype(o_ref.dtype)


def _pool_sample(ext_t, pool_w, pool_scale, n_new, start):
    rows, bs, d_pool = ext_t.shape
    return pl.pallas_call(
        functools.partial(_pool_sample_kernel, n_new=n_new, start=start),
        grid=(1,),
        in_specs=[_const_spec(ext_t.shape), _const_spec(pool_w.shape), _const_spec(pool_scale.shape)],
        out_specs=_const_spec((n_new, bs, d_pool)),
        out_shape=jax.ShapeDtypeStruct((n_new, bs, d_pool), BF16),
        compiler_params=pltpu.CompilerParams(dimension_semantics=("arbitrary",)),
        name="pool_sample",
    )(ext_t, pool_w, pool_scale)


def _conv_silu(window, w_ref, b_ref):
    acc = b_ref[...] + window(0) * w_ref[0:1, :]
    for k in range(1, CONV_W):
        acc = acc + window(k) * w_ref[k:k + 1, :]
    return _silu(acc)


def _softplus(x):
    return jnp.maximum(x, 0.0) + jnp.log1p(jnp.exp(-jnp.abs(x)))


def _ssd_chunk(xact, sz, dt, prm, seq_rows, read_state, write_state, write_out):
    (cums_ref, spread_ref, alog_ref, dexp_ref, normw_ref) = prm
    q = xact.shape[0]
    d_ssm = sz.shape[1]
    gw = d_ssm // N_GROUPS
    n_seq = q // seq_rows
    xs = xact[:, :d_ssm]
    bm = xact[:, d_ssm:d_ssm + N_GROUPS * D_STATE]
    cm = xact[:, d_ssm + N_GROUPS * D_STATE:]

    a = dt * (-jnp.exp(alog_ref[...]) * LOG2_E)
    sums = _dot(cums_ref[...], jnp.concatenate(_split3(a), axis=0))
    cum, rcum = sums[:q], sums[q:]
    w = jnp.exp2(rcum) * dt
    cum_t = cum.T
    dt_t = dt.T
    two_terms = lambda v: jnp.concatenate(_split3(v)[:2], axis=1)
    spread = _dot(jnp.concatenate([two_terms(w), two_terms(jnp.exp2(cum))], axis=0), spread_ref[...])
    wx = xs * spread[:q]
    ecum_e = spread[q:]

    ii = lax.broadcasted_iota(jnp.int32, (q, q), 0)
    jj = lax.broadcasted_iota(jnp.int32, (q, q), 1)
    mask = ii >= jj
    if n_seq > 1:
        mask = jnp.logical_and(mask, (ii // seq_rows) == (jj // seq_rows))
    lo_half = lax.broadcasted_iota(jnp.int32, (q, LANES), 1) < HEAD_DIM
    col = lax.broadcasted_iota(jnp.int32, (gw, q), 1)

    for g in range(N_GROUPS):
        gcols = slice(g * gw, (g + 1) * gw)
        bg = bm[:, g * D_STATE:(g + 1) * D_STATE].astype(BF16)
        cg = cm[:, g * D_STATE:(g + 1) * D_STATE].astype(BF16)
        cb = _dot_nt(cg, bg)
        ydiag = []
        end_decay = {}
        for pr in range(gw // LANES):
            blk = g * (gw // LANES) + pr
            xp = xs[:, blk * LANES:(blk + 1) * LANES]
            mix = []
            for half in range(2):
                h = 2 * blk + half
                colb = jnp.broadcast_to(cum[:, h:h + 1], (q, q))
                for s in range(n_seq):
                    last = (s + 1) * seq_rows - 1
                    end_decay[h, s] = jnp.exp2(colb[last:last + 1, :])
                dec = jnp.exp2(jnp.where(mask, colb - cum_t[h:h + 1, :], -jnp.inf))
                mix.append((cb * dec * dt_t[h:h + 1, :]).astype(BF16))
            x2 = jnp.concatenate([jnp.where(lo_half, xp, 0.0), jnp.where(lo_half, 0.0, xp)], axis=0).astype(BF16)
            ydiag.append(_dot(jnp.concatenate(mix, axis=1), x2))
        wx_t = wx[:, gcols].T
        yoff_rows = []
        for s in range(n_seq):
            r0 = s * seq_rows
            st = read_state(s, g)
            yoff_rows.append(_dot_nt(cg[r0:r0 + seq_rows, :], st.astype(BF16)))
            scale = jnp.concatenate(
                [jnp.broadcast_to(end_decay[h, s], (HEAD_DIM, D_STATE))
                 for h in range(g * (gw // HEAD_DIM), (g + 1) * (gw // HEAD_DIM))], axis=0)
            wsel = wx_t
            if n_seq > 1:
                wsel = jnp.where(jnp.logical_and(col >= r0, col < r0 + seq_rows), wx_t, 0.0)
            write_state(s, g, st * scale + _dot(wsel.astype(BF16), bg))
        yoff = yoff_rows[0] if n_seq == 1 else jnp.concatenate(yoff_rows, axis=0)
        y = jnp.concatenate(ydiag, axis=1) + yoff * ecum_e[:, gcols] + xs[:, gcols] * dexp_ref[:, gcols]
        gz = y * sz[:, gcols]
        ms = jnp.sum(gz * gz, axis=-1, keepdims=True) * (1.0 / gw)
        write_out(g, gz * lax.rsqrt(ms + RMS_EPS) * normw_ref[:, gcols])


def _ssd_prompt_kernel(*refs, gw, ways):
    ins, prm = refs[:3 * ways], refs[3 * ways:3 * ways + 5]
    outs, h_ref = refs[3 * ways + 5:5 * ways + 5], refs[5 * ways + 5]
    c = pl.program_id(1)

    @pl.when(c == 0)
    def _():
        h_ref[...] = jnp.zeros(h_ref.shape, F32)

    for k in range(ways):
        sz_ref, xact_ref, dt_ref = ins[3 * k:3 * k + 3]
        y_ref = outs[2 * k]

        def read_state(s, g, k=k):
            return h_ref[k, g * gw:(g + 1) * gw, :]

        def write_state(s, g, v, k=k):
            h_ref[k, g * gw:(g + 1) * gw, :] = v

        def write_out(g, v, y_ref=y_ref):
            y_ref[:, g * gw:(g + 1) * gw] = v.astype(y_ref.dtype)

        _ssd_chunk(xact_ref[...], sz_ref[...], dt_ref[...], prm, sz_ref.shape[0], read_state, write_state, write_out)

    @pl.when(c == pl.num_programs(1) - 1)
    def _():
        for k in range(ways):
            outs[2 * k + 1][0] = h_ref[k]


def _ssd_sample_kernel(sz_ref, ext_ref, dt_ref, hin_ref, convw_ref, convb_ref,
                       cums_ref, spread_ref, alog_ref, dexp_ref, normw_ref, *rest, gw, n_new):
    y_ref, hout_ref = rest[-2:]
    q = sz_ref.shape[0]
    n_seq = q // SAMPLE_ROWS
    first = CONV_LEAD - (CONV_W - 1)

    def window(k):
        return jnp.concatenate(
            [ext_ref[s * SAMPLE_EXT_ROWS + first + k:s * SAMPLE_EXT_ROWS + first + k + SAMPLE_ROWS, :]
             for s in range(n_seq)], axis=0)

    xact = _conv_silu(window, convw_ref, convb_ref)
    row = lax.broadcasted_iota(jnp.int32, (q, LANES), 0)
    dt = jnp.where((row % SAMPLE_ROWS) < n_new, dt_ref[...], 0.0)

    def read_state(s, g):
        return hin_ref[s, g * gw:(g + 1) * gw, :]

    def write_state(s, g, v):
        hout_ref[s, g * gw:(g + 1) * gw, :] = v

    prm = (cums_ref, spread_ref, alog_ref, dexp_ref, normw_ref)

    def write_out(g, v):
        y_ref[:, g * gw:(g + 1) * gw] = v.astype(y_ref.dtype)

    _ssd_chunk(xact, sz_ref[...], dt, prm, SAMPLE_ROWS, read_state, write_state, write_out)


def _ssd_consts(seq_rows, n_heads):
    q = CHUNK
    i = jnp.arange(q)[:, None]
    j = jnp.arange(q)[None, :]
    same = (i // seq_rows) == (j // seq_rows)
    tri = jnp.logical_and(same, j <= i).astype(BF16)
    tris = jnp.logical_and(same, j > i).astype(BF16)
    cums = jnp.tile(jnp.concatenate([tri, tris], axis=0), (1, 3))
    hrow = jnp.arange(LANES)[:, None]
    sel64 = (hrow == (jnp.arange(n_heads * HEAD_DIM)[None, :] // HEAD_DIM)).astype(BF16)
    spread = jnp.tile(sel64, (2, 1))
    return cums, spread


def _ssd_param_specs(prm_arrays):
    return [_const_spec(a.shape) for a in prm_arrays]


def _ssd_prompt(sz, xact, dt, n_seq, seq_len, lp):
    d_ssm = sz.shape[1]
    d_xbc = xact.shape[1]
    q = CHUNK
    nc = seq_len // q
    gw = d_ssm // N_GROUPS
    prm = (*lp["ssd_consts_prompt"], lp["a_log"], lp["d_exp"], lp["norm_w"])
    ways = SSD_WAYS if n_seq % SSD_WAYS == 0 else 1
    per = n_seq // ways
    in_specs, out_specs, out_shape = [], [], []
    for k in range(ways):
        rows = lambda b, c, k=k: ((k * per + b) * nc + c, 0)
        in_specs += [pl.BlockSpec((q, d_ssm), rows), pl.BlockSpec((q, d_xbc), rows), pl.BlockSpec((q, LANES), rows)]
        out_specs += [pl.BlockSpec((q, d_ssm), lambda b, c: (b * nc + c, 0)),
                      pl.BlockSpec((1, d_ssm, D_STATE), lambda b, c: (b, 0, 0))]
        out_shape += [jax.ShapeDtypeStruct((per * seq_len, d_ssm), BF16),
                      jax.ShapeDtypeStruct((per, d_ssm, D_STATE), F32)]
    res = pl.pallas_call(
        functools.partial(_ssd_prompt_kernel, gw=gw, ways=ways),
        grid=(per, nc),
        in_specs=in_specs + _ssd_param_specs(prm),
        out_specs=out_specs,
        out_shape=out_shape,
        scratch_shapes=[pltpu.VMEM((ways, d_ssm, D_STATE), F32)],
        compiler_params=pltpu.CompilerParams(dimension_semantics=("arbitrary", "arbitrary"),
                                             vmem_limit_bytes=_vmem_limit(48 * 1024 * 1024)),
        name="ssd_prompt",
    )(*([sz, xact, dt] * ways), *prm)
    return list(res[0::2]), list(res[1::2])


def _ssd_sample(z16, ext, dtr16, h_all, h_new, layer, lp, n_new):
    d_ssm = z16.shape[1]
    d_xbc = ext.shape[1]
    q = CHUNK
    spc = q // SAMPLE_ROWS
    bs = h_all.shape[1]
    gw = d_ssm // N_GROUPS
    prm = (lp["conv_w"], lp["conv_b"], *lp["ssd_consts_sample"], lp["a_log"], lp["d_exp"], lp["norm_w"])
    rows = lambda i: (i, 0)
    slab =pl.BlockSpec((None, spc, d_ssm, D_STATE), lambda i: (layer, i, 0, 0))
    state_bytes = spc * d_ssm * D_STATE * 4
    prev = [] if h_new is None else [h_new]
    n_in = 4 + len(prm)
    return pl.pallas_call(
        functools.partial(_ssd_sample_kernel, gw=gw, n_new=n_new),
        grid=(bs // spc,),
        in_specs=[pl.BlockSpec((q, d_ssm), rows), pl.BlockSpec((spc * SAMPLE_EXT_ROWS, d_xbc), rows),
                  pl.BlockSpec((q, LANES), rows), slab] + _ssd_param_specs(prm)
                 + [pl.BlockSpec(memory_space=pl.ANY) for _ in prev],
        out_specs=[pl.BlockSpec((q, d_ssm), rows), slab],
        out_shape=[jax.ShapeDtypeStruct((bs * SAMPLE_ROWS, d_ssm), BF16),
                   jax.ShapeDtypeStruct(h_all.shape, F32)],
        input_output_aliases={n_in: 1} if prev else {},
        compiler_params=pltpu.CompilerParams(dimension_semantics=("arbitrary",),
                                             vmem_limit_bytes=_vmem_limit(4 * state_bytes + 20 * 1024 * 1024)),
        name="ssd_sample",
    )(z16, ext, dtr16, h_all, *prm, *prev)


def _route(h, rw_ref, rb_ref, tril_ref, cnt_ref, n_experts):
    xh = h.astype(BF16)
    xl = (h - xh.astype(F32)).astype(BF16)
    w = rw_ref[...]
    wh = w.astype(BF16)
    wl = (w - wh.astype(F32)).astype(BF16)
    hi = _dot(xh, jnp.concatenate([wh, wl], axis=1))
    logits = hi[:, :LANES] + (hi[:, LANES:] + _dot(xl, wh)) + rb_ref[...]
    lane = lax.broadcasted_iota(jnp.int32, logits.shape, 1)
    lane_f = lane.astype(F32)
    logits = jnp.where(lane < n_experts, logits, -jnp.inf)
    m1 = jnp.max(logits, axis=-1, keepdims=True)
    i1 = jnp.min(jnp.where(logits == m1, lane_f, float(LANES)), axis=-1, keepdims=True)
    rest = jnp.where(lane_f == i1, -jnp.inf, logits)
    m2 = jnp.max(rest, axis=-1, keepdims=True)
    i2 = jnp.min(jnp.where(rest == m2, lane_f, float(LANES)), axis=-1, keepdims=True)
    e2 = jnp.exp(m2 - m1)
    den = 1.0 + e2
    oh1 = jnp.where(lane_f == i1, 1.0, 0.0)
    oh2 = jnp.where(lane_f == i2, 1.0, 0.0)
    before = _dot(tril_ref[...], jnp.concatenate([oh1, oh2], axis=1).astype(BF16))
    before1, before2 = before[:, :LANES], before[:, LANES:]
    c1 = jnp.sum(oh1, axis=0, keepdims=True)
    c2 = jnp.sum(oh2, axis=0, keepdims=True)
    base = cnt_ref[...]
    r1 = jnp.sum(oh1 * (before1 + base), axis=-1, keepdims=True)
    r2 = jnp.sum(oh2 * (before2 + (base + c1)), axis=-1, keepdims=True)
    cnt_ref[...] = base + (c1 + c2)
    gate = jnp.where(lane == 0, 1.0 / den, jnp.where(lane == 1, e2 / den, 0.0))
    plan = jnp.where(lane == 0, i1, jnp.where(lane == 1, i2, jnp.where(lane == 2, r1, jnp.where(lane == 3, r2, 0.0))))
    return gate, plan


def _outproj_ln_kernel(*refs, alpha, n_seg, offs, route, n_experts):
    it = iter(refs)
    take = lambda n: [next(it) for _ in range(n)]
    x_refs, p_refs, s_refs = take(n_seg[0]), take(n_seg[1]), take(n_seg[2])
    wp_ref, ws_ref, g_ref, b_ref = take(4)
    x = _seg_read(x_refs, offs[0])
    mixed = _dot(_seg_read(p_refs, offs[1]), wp_ref[...]) + _dot(_seg_read(s_refs, offs[2]), ws_ref[...])
    h = _layer_norm(alpha * x + mixed, g_ref[...], b_ref[...])
    if not route:
        (o_ref,) = take(1)
        o_ref[...] = h
        return
    rw_ref, rb_ref, tril_ref = take(3)
    xt_ref, gate_ref, plan_ref, cnt_ref = take(4)

    @pl.when(pl.program_id(0) == 0)
    def _():
        cnt_ref[...] = jnp.zeros(cnt_ref.shape, F32)

    tm, d = h.shape
    rpt = d // LANES
    for k in range(rpt):
        xt_ref[pl.ds(k, tm, stride=rpt), :] = h[:, k * LANES:(k + 1) * LANES]
    gate, plan = _route(h, rw_ref, rb_ref, tril_ref, cnt_ref, n_experts)
    gate_ref[...] = gate
    plan_ref[...] = plan.T[:plan_ref.shape[0], :]


def _outproj_ln(x_segs, pool_segs, ssd_segs, w_pool, w_ssd, g, b, alpha, router=None):
    d = x_segs[0].shape[1]
    rows = [a.shape[0] for a in x_segs + pool_segs + ssd_segs]
    tm = _common_tile(rows, (512, 256, 128, 64))
    groups = (x_segs, pool_segs, ssd_segs)
    tiles = [_seg_tiles(s, tm) for s in groups]
    t = sum(tiles[0]) * tm
    assert all(sum(ts) * tm == t for ts in tiles)
    in_specs = [sp for s in groups for sp in _seg_specs(s, tm)]
    consts = [w_pool, w_ssd, g, b]
    row = lambda w, dt: (pl.BlockSpec((tm, w), lambda i: (i, 0)), jax.ShapeDtypeStruct((t, w), dt))
    if router is None:
        outs = [row(d, F32)]
        n_experts = 0
    else:
        rw_pad, rb_pad, n_experts = router
        tril = (jnp.arange(tm)[:, None] > jnp.arange(tm)[None, :]).astype(BF16)
        consts += [rw_pad, rb_pad, tril]
        rpt = d // LANES
        outs = [(pl.BlockSpec((tm * rpt, LANES), lambda i: (i, 0)), jax.ShapeDtypeStruct((t * rpt, LANES), F32)),
                row(LANES, F32),
                (pl.BlockSpec((SUBLANES, tm), lambda i: (0, i)), jax.ShapeDtypeStruct((SUBLANES, t), F32)),
                (_const_spec((1, LANES)), jax.ShapeDtypeStruct((1, LANES), F32))]
    in_specs += [_const_spec(c.shape) for c in consts]
    res = pl.pallas_call(
        functools.partial(_outproj_ln_kernel, alpha=alpha, n_seg=[len(s) for s in groups],
                          offs=[_seg_offsets(ts) for ts in tiles], route=router is not None, n_experts=n_experts),
        grid=(t // tm,),
        in_specs=in_specs,
        out_specs=[o[0] for o in outs],
        out_shape=[o[1] for o in outs],
        compiler_params=pltpu.CompilerParams(dimension_semantics=("arbitrary",),
                                             vmem_limit_bytes=_vmem_limit(40 * 1024 * 1024)),
        name="outproj_ln",
    )(*x_segs, *pool_segs, *ssd_segs, *consts)
    return res[0] if router is None else res


def _swiglu(xb, wg_ref, wu_ref, wd_ref):
    act = (_silu(_dot(xb, wg_ref[0])) * _dot(xb, wu_ref[0])).astype(BF16)
    return _dot(act, wd_ref[0])


def _dense_ffn_ln_kernel(*refs, alpha, offs, tiles, n_side):
    x_ref, wg_ref, wu_ref, wd_ref, g_ref, b_ref = refs[:6]
    side_in = refs[6:6 + n_side]
    out_refs = refs[6 + n_side:6 + n_side + len(tiles)]
    side_out = refs[6 + n_side + len(tiles):]
    x = x_ref[...]
    f = _swiglu(x.astype(BF16), wg_ref, wu_ref, wd_ref)
    _seg_write(out_refs, offs, tiles, _layer_norm(alpha * x + f, g_ref[...], b_ref[...]))
    for src, dst in zip(side_in, side_out):
        dst[...] = src[...].astype(dst.dtype)


def _row_blocks(rows, steps, align):
    for n in range(min(steps, rows), 0, -1):
        if rows % n == 0 and (rows // n) % align == 0:
            return n
    raise ValueError(f"cannot split {rows} rows into at most {steps} blocks of a multiple of {align} rows")


def _dense_ffn_ln(x, wg, wu, wd, g, b, alpha, out_rows, side_casts=()):
    t, d = x.shape
    ff = wg.shape[2]
    tm = _common_tile(out_rows, (256, 128, 64))
    tiles = [n // tm for n in out_rows]
    offs = _seg_offsets(tiles)
    steps = t // tm
    side2d = [a.reshape(-1, a.shape[-1]) for a in side_casts]
    side_specs = []
    for a in side2d:
        nb = _row_blocks(a.shape[0], steps, 2 * SUBLANES)
        side_specs.append(pl.BlockSpec((a.shape[0] // nb, a.shape[1]), lambda i, nb=nb: (jnp.minimum(i, nb - 1), 0)))
    side_bytes = sum(2 * (4 + 2) * s.block_shape[0] * s.block_shape[1] for s in side_specs)
    vmem = 3 * d * ff * 2 + 4 * tm * d * 4 + tm * ff * 12 + side_bytes
    res = pl.pallas_call(
        functools.partial(_dense_ffn_ln_kernel, alpha=alpha, offs=offs, tiles=tiles, n_side=len(side2d)),
        grid=(steps,),
        in_specs=[pl.BlockSpec((tm, d), lambda i: (i, 0)), _const_spec(wg.shape, single=True),
                  _const_spec(wu.shape, single=True), _const_spec(wd.shape, single=True),
                  _const_spec(g.shape), _const_spec(b.shape)] + side_specs,
        out_specs=[pl.BlockSpec((tm, d), lambda i, o=o, n=n: (jnp.clip(i - o, 0, n - 1), 0))
                   for o, n in zip(offs, tiles)] + side_specs,
        out_shape=[jax.ShapeDtypeStruct((n, d), F32) for n in out_rows]
                  + [jax.ShapeDtypeStruct(a.shape, BF16) for a in side2d],
        compiler_params=pltpu.CompilerParams(dimension_semantics=("arbitrary",),
                                             vmem_limit_bytes=_vmem_limit(vmem)),
        name="dense_ffn_ln",
    )(x, wg, wu, wd, g, b, *side2d)
    return list(res[:len(tiles)]), [r.reshape(a.shape) for r, a in zip(res[len(tiles):], side_casts)]


def _dispatch_kernel(ends_ref, slot_ref, x_ref, out_hbm, zeros_ref, sem, zsem, *, tokens, rpt, tile, n_experts):
    i = pl.program_id(0)

    def tail_copy(e):
        start = pl.multiple_of((ends_ref[e] - tile) * rpt, tile * rpt)
        return pltpu.make_async_copy(zeros_ref, out_hbm.at[pl.ds(start, tile * rpt)], zsem)

    def nonempty(e):
        return ends_ref[e] > (ends_ref[e - 1] if e > 0 else 0)

    def unused_copy(j):
        start = pl.multiple_of((ends_ref[n_experts - 1] + j * tile) * rpt, tile * rpt)
        return pltpu.make_async_copy(zeros_ref, out_hbm.at[pl.ds(start, tile * rpt)], zsem)

    def unused(j):
        return ends_ref[n_experts - 1] + (j + 1) * tile <= out_hbm.shape[0] // rpt

    @pl.when(i == 0)
    def _():
        zeros_ref[...] = jnp.zeros(zeros_ref.shape, zeros_ref.dtype)
        for e in range(n_experts):
            @pl.when(nonempty(e))
            def _(e=e):
                tail_copy(e).start()

            @pl.when(unused(e))
            def _(e=e):
                unused_copy(e).start()
        for e in range(n_experts):
            @pl.when(nonempty(e))
            def _(e=e):
                tail_copy(e).wait()

            @pl.when(unused(e))
            def _(e=e):
                unused_copy(e).wait()

    def issue(r, carry):
        src = x_ref.at[pl.ds(pl.multiple_of(r * rpt, rpt), rpt)]
        for k in range(TOP_K):
            dst = out_hbm.at[pl.ds(pl.multiple_of(slot_ref[0, 0, k * tokens + r] * rpt, rpt), rpt)]
            pltpu.make_async_copy(src, dst, sem).start()
        return carry

    lax.fori_loop(0, tokens, issue, 0)
    for k in range(TOP_K):
        pltpu.make_async_copy(x_ref, out_hbm.at[pl.ds(0, tokens * rpt)], sem).wait()


def _dispatch(xt, slots, ends, n_slots, tokens, rpt, n_experts):
    steps = slots.shape[0]
    return pl.pallas_call(
        functools.partial(_dispatch_kernel, tokens=tokens, rpt=rpt, tile=MOE_TILE, n_experts=n_experts),
        grid_spec=pltpu.PrefetchScalarGridSpec(
            num_scalar_prefetch=1, grid=(steps,),
            in_specs=[pl.BlockSpec((1, 1, TOP_K * tokens), lambda i, e: (i, 0, 0), memory_space=pltpu.SMEM),
                      pl.BlockSpec((tokens * rpt, LANES), lambda i, e: (i, 0))],
            out_specs=pl.BlockSpec(memory_space=pl.ANY),
            scratch_shapes=[pltpu.VMEM((MOE_TILE * rpt, LANES), xt.dtype),
                            pltpu.SemaphoreType.DMA(()), pltpu.SemaphoreType.DMA(())]),
        out_shape=jax.ShapeDtypeStruct((n_slots * rpt, LANES), xt.dtype),
        compiler_params=pltpu.CompilerParams(dimension_semantics=("arbitrary",)),
        name="moe_dispatch",
    )(ends, slots, xt)


def _moe_ffn_kernel(te_ref, na_ref, x_ref, wg_ref, wu_ref, wd_ref, o_ref, *, tm, rpt):
    active = pl.program_id(0) < na_ref[0]

    @pl.when(active)
    def _():
        xb = jnp.concatenate([x_ref[pl.ds(k, tm, stride=rpt), :] for k in range(rpt)], axis=1).astype(BF16)
        f = _swiglu(xb, wg_ref, wu_ref, wd_ref)
        for k in range(rpt):
            o_ref[pl.ds(k, tm, stride=rpt), :] = f[:, k * LANES:(k + 1) * LANES]

    @pl.when(jnp.logical_not(active))
    def _():
        o_ref[...] = jnp.zeros(o_ref.shape, o_ref.dtype)


def _moe_ffn(xs, wg, wu, wd, tile_expert, n_active, rpt):
    tm = MOE_TILE
    d, ff = wg.shape[1], wg.shape[2]
    n_tiles = xs.shape[0] // (tm * rpt)
    w_idx = lambda i, te, na: (te[i], 0, 0)
    vmem = 2 * 3 * d * ff * 2 + 4 * tm * d * 4 + tm * ff * 12 + 2 * tm * d * 4
    return pl.pallas_call(
        functools.partial(_moe_ffn_kernel, tm=tm, rpt=rpt),
        grid_spec=pltpu.PrefetchScalarGridSpec(
            num_scalar_prefetch=2, grid=(n_tiles,),
            in_specs=[pl.BlockSpec((tm * rpt, LANES), lambda i, te, na: (jnp.minimum(i, na[0] - 1), 0)),
                      pl.BlockSpec((1, d, ff), w_idx), pl.BlockSpec((1, d, ff), w_idx),
                      pl.BlockSpec((1, ff, d), w_idx)],
            out_specs=pl.BlockSpec((tm * rpt, LANES), lambda i, te, na: (i, 0))),
        out_shape=jax.ShapeDtypeStruct(xs.shape, F32),
        compiler_params=pltpu.CompilerParams(dimension_semantics=("arbitrary",),
                                             vmem_limit_bytes=_vmem_limit(vmem)),
        name="moe_ffn",
    )(tile_expert, n_active, xs, wg, wu, wd)


def _combine_ln_kernel(slot_ref, next_ref, x_ref, gate_ref, y_hbm, g_ref, b_ref, *rest, alpha, tm, rpt, offs, tiles):
    out_refs, (buf, sem) = rest[:len(tiles)], rest[len(tiles):]
    i = pl.program_id(0)
    n_rows = TOP_K * tm * rpt

    def issue(s_ref, slot):
        def body(r, carry):
            src = y_hbm.at[pl.ds(pl.multiple_of(s_ref[0, 0, r] * rpt, rpt), rpt)]
            pltpu.make_async_copy(src, buf.at[slot, pl.ds(pl.multiple_of(r * rpt, rpt), rpt)], sem.at[slot]).start()
            return carry
        lax.fori_loop(0, TOP_K * tm, body, 0)

    @pl.when(i == 0)
    def _():
        issue(slot_ref, 0)

    @pl.when(i + 1 < pl.num_programs(0))
    def _():
        issue(next_ref, (i + 1) % 2)

    cur = i % 2
    pltpu.make_async_copy(y_hbm.at[pl.ds(0, n_rows)], buf.at[cur], sem.at[cur]).wait()
    g1 = gate_ref[:, 0:1]
    g2 = gate_ref[:, 1:2]
    cols = []
    for k in range(rpt):
        xk = x_ref[pl.ds(k, tm, stride=rpt), :]
        ya = buf[cur, pl.ds(k, tm, stride=rpt), :]
        yb = buf[cur, pl.ds(tm * rpt + k, tm, stride=rpt), :]
        cols.append(alpha * xk + (g1 * ya + g2 * yb))
    h = jnp.concatenate(cols, axis=1)
    _seg_write(out_refs, offs, tiles, _layer_norm(h, g_ref[...], b_ref[...]))


def _combine_ln(xt, gate, ys, slots, g, b, alpha, tm, rpt, out_rows):
    d = g.shape[1]
    tiles = [n // tm for n in out_rows]
    offs = _seg_offsets(tiles)
    steps = slots.shape[0]
    slot_spec = lambda f: pl.BlockSpec((1, 1, TOP_K * tm), f, memory_space=pltpu.SMEM)
    res = pl.pallas_call(
        functools.partial(_combine_ln_kernel, alpha=alpha, tm=tm, rpt=rpt, offs=offs, tiles=tiles),
        grid=(steps,),
        in_specs=[slot_spec(lambda i: (i, 0, 0)), slot_spec(lambda i: (jnp.minimum(i + 1, steps - 1), 0, 0)),
                  pl.BlockSpec((tm * rpt, LANES), lambda i: (i, 0)), pl.BlockSpec((tm, LANES), lambda i: (i, 0)),
                  pl.BlockSpec(memory_space=pl.ANY), _const_spec(g.shape), _const_spec(b.shape)],
        out_specs=[pl.BlockSpec((tm, d), lambda i, o=o, n=n: (jnp.clip(i - o, 0, n - 1), 0))
                   for o, n in zip(offs, tiles)],
        out_shape=[jax.ShapeDtypeStruct((n, d), F32) for n in out_rows],
        scratch_shapes=[pltpu.VMEM((2, TOP_K * tm * rpt, LANES), F32), pltpu.SemaphoreType.DMA((2,))],
        compiler_params=pltpu.CompilerParams(dimension_semantics=("arbitrary",),
                                             vmem_limit_bytes=_vmem_limit(32 * 1024 * 1024)),
        name="moe_combine_ln",
    )(slots, slots, xt, gate, ys, g, b)
    return list(res)


def _moe_ffn_ln(xt, gate, plan, counts, wg, wu, wd, g, b, alpha, tm, out_rows):
    n_experts = wg.shape[0]
    d = wg.shape[1]
    rpt = d // LANES
    t = gate.shape[0]
    tile = MOE_TILE
    cnt = counts[0, :n_experts].astype(jnp.int32)
    padded = ((cnt + tile - 1) // tile) * tile
    ends = jnp.cumsum(padded).astype(jnp.int32)
    starts = ends - padded
    plan = plan.astype(jnp.int32)
    slot = plan[TOP_K:2 * TOP_K]
    for e in range(n_experts):
        slot = slot + jnp.where(plan[:TOP_K] == e, starts[e], 0)
    n_slots = -(-(TOP_K * t + n_experts * (tile - 1)) // tile) * tile
    n_tiles = n_slots // tile
    n_active = ends[-1:] // tile
    tile_start = jnp.minimum(jnp.arange(n_tiles, dtype=jnp.int32), n_active[0] - 1) * tile
    tile_expert = jnp.sum((tile_start[:, None] >= ends[None, :]).astype(jnp.int32), axis=1)
    slots = jnp.swapaxes(slot.reshape(TOP_K, t // tm, tm), 0, 1).reshape(t // tm, 1, TOP_K * tm)
    xs = _dispatch(xt, slots, ends, n_slots, tm, rpt, n_experts)
    ys = _moe_ffn(xs, wg, wu, wd, tile_expert, n_active, rpt)
    return _combine_ln(xt, gate, ys, slots, g, b, alpha, tm, rpt, out_rows)


def kernel(x_prompt, x_sample, state_pool, state_conv, state_ssm, w_in, conv_w, conv_b, dt_bias, A_log, D_skip,
           ssm_norm_w, pool_w, pool_scale, w_out, ln1_g, ln1_b, ln2_g, ln2_b, ffn_w_gate, ffn_w_up, ffn_w_down,
           router_w, router_b, moe_w_gate, moe_w_up, moe_w_down):
    bp, seq, d = x_prompt.shape
    bs, n_new, _ = x_sample.shape
    depth = w_in.shape[0]
    d_pool = pool_scale.shape[1]
    d_ssm = ssm_norm_w.shape[1]
    d_xbc = conv_w.shape[2]
    n_heads = dt_bias.shape[1]
    tp, ts = bp * seq, bs * n_new
    alpha = (2.0 * depth) ** 0.25
    assert d_pool == POOL_GROUP * len(POOL_WINDOWS) and d_ssm == n_heads * HEAD_DIM
    assert d_xbc == d_ssm + 2 * N_GROUPS * D_STATE and seq % CHUNK == 0 and n_new <= CONV_W
    assert n_heads <= LANES and bs % (CHUNK // SAMPLE_ROWS) == 0 and d % LANES == 0

    consts_prompt = _ssd_consts(CHUNK, n_heads)
    consts_sample = _ssd_consts(SAMPLE_ROWS, n_heads)
    x_segs = [x_prompt.reshape(tp, d), x_sample.reshape(ts, d)]
    new_pool_p, new_conv_p, new_ssm_p, new_pool_s, new_conv_s = [], [], [], [], []
    ssm_s = None
    for l in range(depth):
        out_rows = [tp, ts]
        n_in = w_in.shape[2]
        w_in_pad = jnp.pad(w_in[l].astype(BF16), ((0, 0), (0, d_pool + d_ssm + d_xbc + LANES - n_in)))
        lp = dict(
            conv_w=conv_w[l], conv_b=conv_b[l][None, :],
            dt_bias=jnp.zeros((1, LANES), F32).at[0, :n_heads].set(dt_bias[l]),
            a_log=jnp.zeros((1, LANES), F32).at[0, :n_heads].set(A_log[l]),
            d_exp=jnp.repeat(D_skip[l], HEAD_DIM)[None, :], norm_w=ssm_norm_w[l][None, :],
            ssd_consts_prompt=consts_prompt, ssd_consts_sample=consts_sample)
        pw = pool_w[l].astype(BF16)
        ps = pool_scale[l][None, :]

        proj = (w_in_pad, lp["conv_w"], lp["conv_b"], lp["dt_bias"], d_pool, d_ssm, d_xbc)

        u_p, sz_p, xact_p, dt_p, c_p = _in_proj(x_segs[0], *proj, seq_len=seq)
        pool_p, buf_p = _pool_prompt(u_p, tp, seq, pw, ps)
        y_p, h_p = _ssd_prompt(sz_p, xact_p, dt_p, bp, seq, lp)
        new_pool_p.append(buf_p)
        new_conv_p.append(c_p)
        new_ssm_p.append(jnp.concatenate(h_p, axis=0).reshape(bp, n_heads, HEAD_DIM, D_STATE))

        u_s, sz_s, xbc_s, dt_s = _in_proj_sample(x_segs[1], w_in_pad, lp["dt_bias"], d_pool, d_ssm, d_xbc)
        u_s = u_s.reshape(bs, n_new, d_pool)
        xbc_s = xbc_s.reshape(bs, n_new, d_xbc)
        pool_ext = jnp.concatenate([state_pool[l], u_s], axis=1)
        pool_s = _pool_sample(jnp.swapaxes(pool_ext, 0, 1), pw, ps, n_new, PAST_LEN)
        pool_s = jnp.swapaxes(pool_s, 0, 1).reshape(ts, d_pool)
        conv_ext = jnp.concatenate([state_conv[l], xbc_s], axis=1)
        lead = CONV_LEAD - (CONV_W - 1)
        ext = jnp.pad(conv_ext, ((0, 0), (lead, SAMPLE_EXT_ROWS - lead - conv_ext.shape[1]), (0, 0)))
        pad_rows = lambda a: jnp.pad(a.reshape(bs, n_new, -1), ((0, 0), (0, SAMPLE_ROWS - n_new), (0, 0))
                                     ).reshape(bs * SAMPLE_ROWS, -1)
        y_s16, ssm_s = _ssd_sample(pad_rows(sz_s), ext.reshape(bs * SAMPLE_EXT_ROWS, d_xbc), pad_rows(dt_s),
                                   state_ssm.reshape(depth, bs, d_ssm, D_STATE), ssm_s, l, lp, n_new)
        y_s = y_s16.reshape(bs, SAMPLE_ROWS, d_ssm)[:, :n_new].reshape(ts, d_ssm)
        new_pool_s.append(pool_ext[:, n_new:])
        new_conv_s.append(conv_ext[:, n_new:])

        w_o = w_out[l].astype(BF16)
        ln1 = (ln1_g[l][None, :], ln1_b[l][None, :])
        g2, b2 = ln2_g[l][None, :], ln2_b[l][None, :]
        j = l // 2
        if l % 2 == 0:
            x1 = _outproj_ln(x_segs, [pool_p, pool_s], [*y_p, y_s], w_o[:d_pool], w_o[d_pool:], *ln1, alpha)
            side = (moe_w_gate[j], moe_w_up[j], moe_w_down[j]) if l + 1 < depth else ()
            x_segs, moe_bf16 = _dense_ffn_ln(x1, ffn_w_gate[j:j + 1].astype(BF16), ffn_w_up[j:j + 1].astype(BF16),
                                             ffn_w_down[j:j + 1].astype(BF16), g2, b2, alpha, out_rows, side)
        else:
            n_experts = router_w.shape[2]
            rw_pad = jnp.zeros((d, LANES), F32).at[:, :n_experts].set(router_w[j])
            rb_pad = jnp.zeros((1, LANES), F32).at[0, :n_experts].set(router_b[j])
            xt, gate, plan, counts = _outproj_ln(x_segs, [pool_p, pool_s], [*y_p, y_s], w_o[:d_pool], w_o[d_pool:],
                                                 *ln1, alpha, router=(rw_pad, rb_pad, n_experts))
            tm = _common_tile([tp, ts], (512, 256, 128, 64))
            x_segs = _moe_ffn_ln(xt, gate, plan, counts, *moe_bf16, g2, b2, alpha, tm, out_rows)

    y_prompt, y_sample = x_segs
    return (y_prompt.reshape(bp, seq, d), y_sample.reshape(bs, n_new, d),
            jnp.stack(new_pool_p), jnp.stack(new_conv_p), jnp.stack(new_ssm_p),
            jnp.stack(new_pool_s), jnp.stack(new_conv_s),
            ssm_s.reshape(depth, bs, n_heads, HEAD_DIM, D_STATE))
```

```python
import functools

import jax
import jax.numpy as jnp
from jax import lax
from jax.experimental import pallas as pl
from jax.experimental.pallas import tpu as pltpu

F32 = jnp.float32
BF16 = jnp.bfloat16

PAST_LEN = 16384
POOL_WINDOWS = (2, 4, 8, 16)
POOL_GROUP = 128
POOL_BUF = max(POOL_WINDOWS) - 1
HEAD_DIM = 64
N_GROUPS = 4
D_STATE = 128
CONV_W = 4
CHUNK = 128
TOP_K = 2
LN_EPS = 1e-5
RMS_EPS = 1e-6
LOG2_E = 1.4426950408889634

LANES = 128
SUBLANES = 8
V7X_VMEM_BYTES = 64 * 1024 * 1024
VMEM_CAP = V7X_VMEM_BYTES - 8 * 1024 * 1024

SAMPLE_ROWS = 16
SAMPLE_EXT_ROWS = 24
CONV_LEAD = 8
MOE_TILE = 256
SSD_WAYS = 2


def _vmem_limit(nbytes):
    return int(min(VMEM_CAP, nbytes + 6 * 1024 * 1024))


def _dot(a, b):
    return jnp.dot(a, b, preferred_element_type=F32)


def _dot_nt(a, b):
    return lax.dot_general(a, b, (((1,), (1,)), ((), ())), preferred_element_type=F32)


def _split3(v):
    hi = v.astype(BF16)
    r = v - hi.astype(F32)
    mid = r.astype(BF16)
    lo = (r - mid.astype(F32)).astype(BF16)
    return hi, mid, lo


def _sel_right(parts, m):
    out = _dot(parts[0], m)
    for p in parts[1:]:
        out = out + _dot(p, m)
    return out


def _sel_left(m, parts):
    out = _dot(m, parts[0])
    for p in parts[1:]:
        out = out + _dot(m, p)
    return out


def _silu(x):
    return x / (1.0 + jnp.exp2(x * (-LOG2_E)))


def _layer_norm(h, g, b):
    mu = jnp.mean(h, axis=-1, keepdims=True)
    d = h - mu
    var = jnp.mean(d * d, axis=-1, keepdims=True)
    return d * lax.rsqrt(var + LN_EPS) * g + b


def _pick_tile(n, candidates):
    for c in candidates:
        if n % c == 0:
            return c
    raise ValueError(f"no tile in {candidates} divides {n}")


def _const_spec(shape, single=False):
    nd = len(shape)
    if single:
        return pl.BlockSpec(shape, lambda *_: (0,) * nd, pipeline_mode=pl.Buffered(1))
    return pl.BlockSpec(shape, lambda *_: (0,) * nd)


def _seg_tiles(segs, tm, rows_per_token=1):
    return [a.shape[0] // (tm * rows_per_token) for a in segs]


def _seg_offsets(tiles):
    offs, off = [], 0
    for n in tiles:
        offs.append(off)
        off += n
    return offs


def _seg_specs(segs, tm, rows_per_token=1):
    tiles = _seg_tiles(segs, tm, rows_per_token)
    return [pl.BlockSpec((tm * rows_per_token, a.shape[1]),
                         lambda i, *_, o=o, n=n: (jnp.clip(i - o, 0, n - 1), 0))
            for a, o, n in zip(segs, _seg_offsets(tiles), tiles)]


def _seg_read(refs, offs):
    v = refs[0][...]
    for r, o in zip(refs[1:], offs[1:]):
        v = jnp.where(pl.program_id(0) >= o, r[...], v)
    return v


def _seg_write(refs, offs, tiles, v):
    i = pl.program_id(0)
    for r, o, n in zip(refs, offs, tiles):
        @pl.when(jnp.logical_and(i >= o, i < o + n))
        def _(r=r):
            r[...] = v.astype(r.dtype)


def _common_tile(row_counts, candidates):
    for c in candidates:
        if all(n % c == 0 for n in row_counts):
            return c
    raise ValueError(f"no tile in {candidates} divides all of {row_counts}")


INPROJ_CHUNK = 512


def _row_blocks(rows, steps, align):
    for n in range(min(steps, rows), 0, -1):
        if rows % n == 0 and (rows // n) % align == 0:
            return n
    raise ValueError(f"cannot split {rows} rows into at most {steps} blocks of a multiple of {align} rows")


def _side_cast_plan(arrays, steps):
    side2d = [a.reshape(-1, a.shape[-1]) for a in arrays]
    specs = []
    for a in side2d:
        nb = _row_blocks(a.shape[0], steps, 2 * SUBLANES)
        specs.append(pl.BlockSpec((a.shape[0] // nb, a.shape[1]), lambda i, nb=nb: (jnp.minimum(i, nb - 1), 0)))
    vmem = sum(2 * (4 + 2) * s.block_shape[0] * s.block_shape[1] for s in specs)
    return side2d, specs, vmem


def _side_cast(side_in, side_out):
    for src, dst in zip(side_in, side_out):
        dst[...] = src[...].astype(dst.dtype)


def _inproj_kernel(*refs, splits, tiles_per_seq, n_side):
    x_ref, w_ref, convw_ref, convb_ref, dtb_ref = refs[:5]
    side_in = refs[5:5 + n_side]
    u_ref, sz_ref, xo_ref, dt_ref, cstate_ref = refs[5 + n_side:10 + n_side]
    side_out, ext_ref = refs[10 + n_side:10 + 2 * n_side], refs[10 + 2 * n_side]
    (u0, u1), (z0, z1), (c0, c1), (d0, d1) = splits
    tm = x_ref.shape[0]
    xb = x_ref[...].astype(BF16)
    u_ref[...] = _dot(xb, w_ref[:, u0:u1])
    dt_ref[...] = _softplus(_dot(xb, w_ref[:, d0:d1]) + dtb_ref[...])
    for lo in range(z0, z1, INPROJ_CHUNK):
        sz_ref[:, lo - z0:lo - z0 + INPROJ_CHUNK] = _silu(_dot(xb, w_ref[:, lo:lo + INPROJ_CHUNK]))
    _side_cast(side_in, side_out)

    pos = pl.program_id(0) % tiles_per_seq

    @pl.when(pos == 0)
    def _():
        ext_ref[0:CONV_LEAD, :] = jnp.zeros((CONV_LEAD, ext_ref.shape[1]), F32)

    @pl.when(pos > 0)
    def _():
        ext_ref[0:CONV_LEAD, :] = ext_ref[tm:tm + CONV_LEAD, :]

    first = CONV_LEAD - (CONV_W - 1)
    for lo in range(c0, c1, INPROJ_CHUNK):
        cols = slice(lo - c0, lo - c0 + INPROJ_CHUNK)
        ext_ref[CONV_LEAD:CONV_LEAD + tm, cols] = _dot(xb, w_ref[:, lo:lo + INPROJ_CHUNK])
        rows = ext_ref[:, cols]
        acc = convb_ref[:, cols] + (pltpu.roll(rows, CONV_W - 1, axis=0)[CONV_LEAD:] * convw_ref[0:1, cols])
        for k in range(1, CONV_W - 1):
            acc = acc + pltpu.roll(rows, CONV_W - 1 - k, axis=0)[CONV_LEAD:] * convw_ref[k:k + 1, cols]
        acc = acc + rows[CONV_LEAD:] * convw_ref[CONV_W - 1:CONV_W, cols]
        xo_ref[:, cols] = _silu(acc)

    @pl.when(pos == tiles_per_seq - 1)
    def _():
        cstate_ref[0] = ext_ref[CONV_LEAD + tm - (CONV_W - 1):CONV_LEAD + tm, :]


def _inproj_splits(w_pad, d_pool, d_ssm, d_xbc):
    assert d_ssm % INPROJ_CHUNK == 0 and d_xbc % INPROJ_CHUNK == 0
    return ((0, d_pool), (d_pool, d_pool + d_ssm), (d_pool + d_ssm, d_pool + d_ssm + d_xbc),
            (d_pool + d_ssm + d_xbc, w_pad.shape[1]))


def _in_proj(x, w_pad, conv_w, conv_b, dt_bias, d_pool, d_ssm, d_xbc, seq_len, side_casts=()):
    t, d = x.shape
    n = w_pad.shape[1]
    tm = _pick_tile(seq_len, (512, 256, 128, 64))
    splits = _inproj_splits(w_pad, d_pool, d_ssm, d_xbc)
    widths = [hi - lo for lo, hi in splits]
    tiles_per_seq = seq_len // tm
    side2d, side_specs, side_bytes = _side_cast_plan(side_casts, t // tm)
    vmem = (2 * tm * d * 4 + d * n * 2 + 2 * tm * n * 4 + 2 * tm * d_xbc * 4 + 4 * tm * INPROJ_CHUNK * 4
            + side_bytes)
    res = pl.pallas_call(
        functools.partial(_inproj_kernel, splits=splits, tiles_per_seq=tiles_per_seq, n_side=len(side2d)),
        grid=(t // tm,),
        in_specs=[pl.BlockSpec((tm, d), lambda i: (i, 0)), _const_spec((d, n), single=True),
                  _const_spec(conv_w.shape), _const_spec(conv_b.shape), _const_spec(dt_bias.shape)] + side_specs,
        out_specs=[pl.BlockSpec((tm, w), lambda i: (i, 0)) for w in widths]
                  + [pl.BlockSpec((1, CONV_W - 1, d_xbc), lambda i: (i // tiles_per_seq, 0, 0))] + side_specs,
        out_shape=[jax.ShapeDtypeStruct((t, w), F32) for w in widths]
                  + [jax.ShapeDtypeStruct((t // seq_len, CONV_W - 1, d_xbc), F32)]
                  + [jax.ShapeDtypeStruct(a.shape, BF16) for a in side2d],
        scratch_shapes=[pltpu.VMEM((tm + CONV_LEAD, d_xbc), F32)],
        compiler_params=pltpu.CompilerParams(dimension_semantics=("arbitrary",),
                                             vmem_limit_bytes=_vmem_limit(vmem)),
        name="in_proj",
    )(x, w_pad, conv_w, conv_b, dt_bias, *side2d)
    return list(res[:5]), [r.reshape(a.shape) for r, a in zip(res[5:], side_casts)]


def _inproj_sample_kernel(x_ref, w_ref, dtb_ref, u_ref, sz_ref, xbc_ref, dt_ref, *, splits):
    (u0, u1), (z0, z1), (c0, c1), (d0, d1) = splits
    xb = x_ref[...].astype(BF16)
    u_ref[...] = _dot(xb, w_ref[:, u0:u1])
    sz_ref[...] = _silu(_dot(xb, w_ref[:, z0:z1]))
    xbc_ref[...] = _dot(xb, w_ref[:, c0:c1])
    dt_ref[...] = _softplus(_dot(xb, w_ref[:, d0:d1]) + dtb_ref[...])


def _in_proj_sample(x, w_pad, dt_bias, d_pool, d_ssm, d_xbc):
    t, d = x.shape
    n = w_pad.shape[1]
    tm = _pick_tile(t, (512, 256, 128, 64))
    splits = _inproj_splits(w_pad, d_pool, d_ssm, d_xbc)
    widths = [hi - lo for lo, hi in splits]
    vmem = 2 * tm * d * 4 + 2 * d * n * 2 + 4 * tm * n * 4
    return pl.pallas_call(
        functools.partial(_inproj_sample_kernel, splits=splits),
        grid=(t // tm,),
        in_specs=[pl.BlockSpec((tm, d), lambda i: (i, 0)), _const_spec((d, n)), _const_spec(dt_bias.shape)],
        out_specs=[pl.BlockSpec((tm, w), lambda i: (i, 0)) for w in widths],
        out_shape=[jax.ShapeDtypeStruct((t, w), F32) for w in widths],
        compiler_params=pltpu.CompilerParams(dimension_semantics=("arbitrary",),
                                             vmem_limit_bytes=_vmem_limit(vmem)),
        name="in_proj_sample",
    )(x, w_pad, dt_bias)


def _pool_prompt_kernel(u_ref, w_ref, scale_ref, o_ref, buf_ref, ext_ref, *, tl):
    j = pl.program_id(1)

    @pl.when(j == pl.num_programs(1) - 1)
    def _():
        buf_ref[0] = u_ref[tl - POOL_BUF:tl, :]

    @pl.when(j == 0)
    def _():
        ext_ref[0:16, :] = jnp.zeros((16, ext_ref.shape[1]), F32)

    @pl.when(j > 0)
    def _():
        ext_ref[0:16, :] = ext_ref[tl:tl + 16, :]

    ext_ref[16:16 + tl, :] = u_ref[...]
    pos = j * tl + lax.broadcasted_iota(jnp.int32, (tl, POOL_GROUP), 0)
    for g, win in enumerate(POOL_WINDOWS):
        lanes = slice(g * POOL_GROUP, (g + 1) * POOL_GROUP)
        rows = ext_ref[:, lanes]
        cur = rows[16:]
        span = 1
        while span < win:
            rows = rows + pltpu.roll(rows, span, axis=0)
            span *= 2
        acc = rows[16:]
        cnt = jnp.minimum(pos + 1, win).astype(F32)
        diff = (acc / cnt - cur).astype(BF16)
        out = _dot(diff, w_ref[g]) * scale_ref[:, lanes]
        o_ref[:, lanes] = out.astype(o_ref.dtype)


def _pool_prompt(u, n_rows, seq_len, pool_w, pool_scale):
    d_pool = u.shape[1]
    tl = _pick_tile(seq_len, (512, 256, 128))
    nj = seq_len // tl
    nb = n_rows // seq_len
    return pl.pallas_call(
        functools.partial(_pool_prompt_kernel, tl=tl),
        grid=(nb, nj),
        in_specs=[pl.BlockSpec((tl, d_pool), lambda b, j: (b * nj + j, 0)),
                  _const_spec(pool_w.shape), _const_spec(pool_scale.shape)],
        out_specs=[pl.BlockSpec((tl, d_pool), lambda b, j: (b * nj + j, 0)),
                   pl.BlockSpec((1, POOL_BUF, d_pool), lambda b, j: (b, 0, 0))],
        out_shape=[jax.ShapeDtypeStruct((n_rows, d_pool), BF16),
                   jax.ShapeDtypeStruct((nb, POOL_BUF, d_pool), F32)],
        scratch_shapes=[pltpu.VMEM((tl + 16, d_pool), F32)],
        compiler_params=pltpu.CompilerParams(dimension_semantics=("arbitrary", "arbitrary")),
        name="pool_prompt",
    )(u, pool_w, pool_scale)


def _pool_sample_kernel(ext_ref, w_ref, scale_ref, o_ref, *, n_new, start):
    for t in range(n_new):
        for g, win in enumerate(POOL_WINDOWS):
            lanes = slice(g * POOL_GROUP, (g + 1) * POOL_GROUP)
            cur = ext_ref[POOL_BUF + t, :, lanes]
            acc = cur
            for k in range(1, win):
                acc = acc + ext_ref[POOL_BUF + t - k, :, lanes]
            cnt = float(min(start + t + 1, win))
            diff = (acc / cnt - cur).astype(BF16)
            out = _dot(diff, w_ref[g]) * scale_ref[:, lanes]
            o_ref[t, :, lanes] = out.astype(o_ref.dtype)


def _pool_sample(ext_t, pool_w, pool_scale, n_new, start):
    rows, bs, d_pool = ext_t.shape
    return pl.pallas_call(
        functools.partial(_pool_sample_kernel, n_new=n_new, start=start),
        grid=(1,),
        in_specs=[_const_spec(ext_t.shape), _const_spec(pool_w.shape), _const_spec(pool_scale.shape)],
        out_specs=_const_spec((n_new, bs, d_pool)),
        out_shape=jax.ShapeDtypeStruct((n_new, bs, d_pool), BF16),
        compiler_params=pltpu.CompilerParams(dimension_semantics=("arbitrary",)),
        name="pool_sample",
    )(ext_t, pool_w, pool_scale)


def _conv_silu(window, w_ref, b_ref):
    acc = b_ref[...] + window(0) * w_ref[0:1, :]
    for k in range(1, CONV_W):
        acc = acc + window(k) * w_ref[k:k + 1, :]
    return _silu(acc)


def _softplus(x):
    return jnp.maximum(x, 0.0) + jnp.log1p(jnp.exp(-jnp.abs(x)))


def _ssd_chunk(xact, sz, dt, prm, seq_rows, read_state, write_state, write_out):
    (cums_ref, spread_ref, alog_ref, dexp_ref, normw_ref) = prm
    q = xact.shape[0]
    d_ssm = sz.shape[1]
    gw = d_ssm // N_GROUPS
    n_seq = q // seq_rows
    xs = xact[:, :d_ssm]
    bm = xact[:, d_ssm:d_ssm + N_GROUPS * D_STATE]
    cm = xact[:, d_ssm + N_GROUPS * D_STATE:]

    a = dt * (-jnp.exp(alog_ref[...]) * LOG2_E)
    sums = _dot(cums_ref[...], jnp.concatenate(_split3(a), axis=0))
    cum, rcum = sums[:q], sums[q:]
    w = jnp.exp2(rcum) * dt
    cum_t = cum.T
    dt_t = dt.T
    two_terms = lambda v: jnp.concatenate(_split3(v)[:2], axis=1)
    spread = _dot(jnp.concatenate([two_terms(w), two_terms(jnp.exp2(cum))], axis=0), spread_ref[...])
    wx = xs * spread[:q]
    ecum_e = spread[q:]

    ii = lax.broadcasted_iota(jnp.int32, (q, q), 0)
    jj = lax.broadcasted_iota(jnp.int32, (q, q), 1)
    mask = ii >= jj
    if n_seq > 1:
        mask = jnp.logical_and(mask, (ii // seq_rows) == (jj // seq_rows))
    lo_half = lax.broadcasted_iota(jnp.int32, (q, LANES), 1) < HEAD_DIM
    col = lax.broadcasted_iota(jnp.int32, (gw, q), 1)

    for g in range(N_GROUPS):
        gcols = slice(g * gw, (g + 1) * gw)
        bg = bm[:, g * D_STATE:(g + 1) * D_STATE].astype(BF16)
        cg = cm[:, g * D_STATE:(g + 1) * D_STATE].astype(BF16)
        cb = _dot_nt(cg, bg)
        ydiag = []
        end_decay = {}
        for pr in range(gw // LANES):
            blk = g * (gw // LANES) + pr
            xp = xs[:, blk * LANES:(blk + 1) * LANES]
            mix = []
            for half in range(2):
                h = 2 * blk + half
                colb = jnp.broadcast_to(cum[:, h:h + 1], (q, q))
                for s in range(n_seq):
                    last = (s + 1) * seq_rows - 1
                    end_decay[h, s] = jnp.exp2(colb[last:last + 1, :])
                dec = jnp.exp2(jnp.where(mask, colb - cum_t[h:h + 1, :], -jnp.inf))
                mix.append((cb * dec * dt_t[h:h + 1, :]).astype(BF16))
            x2 = jnp.concatenate([jnp.where(lo_half, xp, 0.0), jnp.where(lo_half, 0.0, xp)], axis=0).astype(BF16)
            ydiag.append(_dot(jnp.concatenate(mix, axis=1), x2))
        wx_t = wx[:, gcols].T
        yoff_rows = []
        for s in range(n_seq):
            r0 = s * seq_rows
            st = read_state(s, g)
            yoff_rows.append(_dot_nt(cg[r0:r0 + seq_rows, :], st.astype(BF16)))
            scale = jnp.concatenate(
                [jnp.broadcast_to(end_decay[h, s], (HEAD_DIM, D_STATE))
                 for h in range(g * (gw // HEAD_DIM), (g + 1) * (gw // HEAD_DIM))], axis=0)
            wsel = wx_t
            if n_seq > 1:
                wsel = jnp.where(jnp.logical_and(col >= r0, col < r0 + seq_rows), wx_t, 0.0)
            write_state(s, g, st * scale + _dot(wsel.astype(BF16), bg))
        yoff = yoff_rows[0] if n_seq == 1 else jnp.concatenate(yoff_rows, axis=0)
        y = jnp.concatenate(ydiag, axis=1) + yoff * ecum_e[:, gcols] + xs[:, gcols] * dexp_ref[:, gcols]
        gz = y * sz[:, gcols]
        ms = jnp.sum(gz * gz, axis=-1, keepdims=True) * (1.0 / gw)
        write_out(g, gz * lax.rsqrt(ms + RMS_EPS) * normw_ref[:, gcols])


def _ssd_prompt_kernel(*refs, gw, ways):
    ins, prm = refs[:3 * ways], refs[3 * ways:3 * ways + 5]
    outs, h_ref = refs[3 * ways + 5:5 * ways + 5], refs[5 * ways + 5]
    c = pl.program_id(1)

    @pl.when(c == 0)
    def _():
        h_ref[...] = jnp.zeros(h_ref.shape, F32)

    for k in range(ways):
        sz_ref, xact_ref, dt_ref = ins[3 * k:3 * k + 3]
        y_ref = outs[2 * k]

        def read_state(s, g, k=k):
            return h_ref[k, g * gw:(g + 1) * gw, :]

        def write_state(s, g, v, k=k):
            h_ref[k, g * gw:(g + 1) * gw, :] = v

        def write_out(g, v, y_ref=y_ref):
            y_ref[:, g * gw:(g + 1) * gw] = v.astype(y_ref.dtype)

        _ssd_chunk(xact_ref[...], sz_ref[...], dt_ref[...], prm, sz_ref.shape[0], read_state, write_state, write_out)

    @pl.when(c == pl.num_programs(1) - 1)
    def _():
        for k in range(ways):
            outs[2 * k + 1][0] = h_ref[k]


def _ssd_sample_kernel(sz_ref, ext_ref, dt_ref, hin_ref, convw_ref, convb_ref,
                       cums_ref, spread_ref, alog_ref, dexp_ref, normw_ref, *rest, gw, n_new):
    y_ref, hout_ref = rest[-2:]
    q = sz_ref.shape[0]
    n_seq = q // SAMPLE_ROWS
    first = CONV_LEAD - (CONV_W - 1)

    def window(k):
        return jnp.concatenate(
            [ext_ref[s * SAMPLE_EXT_ROWS + first + k:s * SAMPLE_EXT_ROWS + first + k + SAMPLE_ROWS, :]
             for s in range(n_seq)], axis=0)

    xact = _conv_silu(window, convw_ref, convb_ref)
    row = lax.broadcasted_iota(jnp.int32, (q, LANES), 0)
    dt = jnp.where((row % SAMPLE_ROWS) < n_new, dt_ref[...], 0.0)

    def read_state(s, g):
        return hin_ref[s, g * gw:(g + 1) * gw, :]

    def write_state(s, g, v):
        hout_ref[s, g * gw:(g + 1) * gw, :] = v

    prm = (cums_ref, spread_ref, alog_ref, dexp_ref, normw_ref)

    def write_out(g, v):
        y_ref[:, g * gw:(g + 1) * gw] = v.astype(y_ref.dtype)

    _ssd_chunk(xact, sz_ref[...], dt, prm, SAMPLE_ROWS, read_state, write_state, write_out)


def _ssd_consts(seq_rows, n_heads):
    q = CHUNK
    i = jnp.arange(q)[:, None]
    j = jnp.arange(q)[None, :]
    same = (i // seq_rows) == (j // seq_rows)
    tri = jnp.logical_and(same, j <= i).astype(BF16)
    tris = jnp.logical_and(same, j > i).astype(BF16)
    cums = jnp.tile(jnp.concatenate([tri, tris], axis=0), (1, 3))
    hrow = jnp.arange(LANES)[:, None]
    sel64 = (hrow == (jnp.arange(n_heads * HEAD_DIM)[None, :] // HEAD_DIM)).astype(BF16)
    spread = jnp.tile(sel64, (2, 1))
    return cums, spread


def _ssd_param_specs(prm_arrays):
    return [_const_spec(a.shape) for a in prm_arrays]


def _ssd_prompt(sz, xact, dt, n_seq, seq_len, lp):
    d_ssm = sz.shape[1]
    d_xbc = xact.shape[1]
    q = CHUNK
    nc = seq_len // q
    gw = d_ssm // N_GROUPS
    prm = (*lp["ssd_consts_prompt"], lp["a_log"], lp["d_exp"], lp["norm_w"])
    ways = SSD_WAYS if n_seq % SSD_WAYS == 0 else 1
    per = n_seq // ways
    in_specs, out_specs, out_shape = [], [], []
    for k in range(ways):
        rows = lambda b, c, k=k: ((k * per + b) * nc + c, 0)
        in_specs += [pl.BlockSpec((q, d_ssm), rows), pl.BlockSpec((q, d_xbc), rows), pl.BlockSpec((q, LANES), rows)]
        out_specs += [pl.BlockSpec((q, d_ssm), lambda b, c: (b * nc + c, 0)),
                      pl.BlockSpec((1, d_ssm, D_STATE), lambda b, c: (b, 0, 0))]
        out_shape += [jax.ShapeDtypeStruct((per * seq_len, d_ssm), BF16),
                      jax.ShapeDtypeStruct((per, d_ssm, D_STATE), F32)]
    res = pl.pallas_call(
        functools.partial(_ssd_prompt_kernel, gw=gw, ways=ways),
        grid=(per, nc),
        in_specs=in_specs + _ssd_param_specs(prm),
        out_specs=out_specs,
        out_shape=out_shape,
        scratch_shapes=[pltpu.VMEM((ways, d_ssm, D_STATE), F32)],
        compiler_params=pltpu.CompilerParams(dimension_semantics=("arbitrary", "arbitrary"),
                                             vmem_limit_bytes=_vmem_limit(48 * 1024 * 1024)),
        name="ssd_prompt",
    )(*([sz, xact, dt] * ways), *prm)
    return list(res[0::2]), list(res[1::2])


def _ssd_sample(z16, ext, dtr16, h_all, h_new, layer, lp, n_new):
    d_ssm = z16.shape[1]
    d_xbc = ext.shape[1]
    q = CHUNK
    spc = q // SAMPLE_ROWS
    bs = h_all.shape[1]
    gw = d_ssm // N_GROUPS
    prm = (lp["conv_w"], lp["conv_b"], *lp["ssd_consts_sample"], lp["a_log"], lp["d_exp"], lp["norm_w"])
    rows = lambda i: (i, 0)
    slab =pl.BlockSpec((None, spc, d_ssm, D_STATE), lambda i: (layer, i, 0, 0))
    state_bytes = spc * d_ssm * D_STATE * 4
    prev = [] if h_new is None else [h_new]
    n_in = 4 + len(prm)
    return pl.pallas_call(
        functools.partial(_ssd_sample_kernel, gw=gw, n_new=n_new),
        grid=(bs // spc,),
        in_specs=[pl.BlockSpec((q, d_ssm), rows), pl.BlockSpec((spc * SAMPLE_EXT_ROWS, d_xbc), rows),
                  pl.BlockSpec((q, LANES), rows), slab] + _ssd_param_specs(prm)
                 + [pl.BlockSpec(memory_space=pl.ANY) for _ in prev],
        out_specs=[pl.BlockSpec((q, d_ssm), rows), slab],
        out_shape=[jax.ShapeDtypeStruct((bs * SAMPLE_ROWS, d_ssm), BF16),
                   jax.ShapeDtypeStruct(h_all.shape, F32)],
        input_output_aliases={n_in: 1} if prev else {},
        compiler_params=pltpu.CompilerParams(dimension_semantics=("arbitrary",),
                                             vmem_limit_bytes=_vmem_limit(4 * state_bytes + 20 * 1024 * 1024)),
        name="ssd_sample",
    )(z16, ext, dtr16, h_all, *prm, *prev)


def _route(h, rw_ref, rb_ref, tril_ref, cnt_ref, n_experts):
    xh = h.astype(BF16)
    xl = (h - xh.astype(F32)).astype(BF16)
    w = rw_ref[...]
    wh = w.astype(BF16)
    wl = (w - wh.astype(F32)).astype(BF16)
    hi = _dot(xh, jnp.concatenate([wh, wl], axis=1))
    logits = hi[:, :LANES] + (hi[:, LANES:] + _dot(xl, wh)) + rb_ref[...]
    lane = lax.broadcasted_iota(jnp.int32, logits.shape, 1)
    lane_f = lane.astype(F32)
    logits = jnp.where(lane < n_experts, logits, -jnp.inf)
    m1 = jnp.max(logits, axis=-1, keepdims=True)
    i1 = jnp.min(jnp.where(logits == m1, lane_f, float(LANES)), axis=-1, keepdims=True)
    rest = jnp.where(lane_f == i1, -jnp.inf, logits)
    m2 = jnp.max(rest, axis=-1, keepdims=True)
    i2 = jnp.min(jnp.where(rest == m2, lane_f, float(LANES)), axis=-1, keepdims=True)
    e2 = jnp.exp(m2 - m1)
    den = 1.0 + e2
    oh1 = jnp.where(lane_f == i1, 1.0, 0.0)
    oh2 = jnp.where(lane_f == i2, 1.0, 0.0)
    before = _dot(tril_ref[...], jnp.concatenate([oh1, oh2], axis=1).astype(BF16))
    before1, before2 = before[:, :LANES], before[:, LANES:]
    c1 = jnp.sum(oh1, axis=0, keepdims=True)
    c2 = jnp.sum(oh2, axis=0, keepdims=True)
    base = cnt_ref[...]
    r1 = jnp.sum(oh1 * (before1 + base), axis=-1, keepdims=True)
    r2 = jnp.sum(oh2 * (before2 + (base + c1)), axis=-1, keepdims=True)
    cnt_ref[...] = base + (c1 + c2)
    gate = jnp.where(lane == 0, 1.0 / den, jnp.where(lane == 1, e2 / den, 0.0))
    plan = jnp.where(lane == 0, i1, jnp.where(lane == 1, i2, jnp.where(lane == 2, r1, jnp.where(lane == 3, r2, 0.0))))
    return gate, plan


def _outproj_ln_kernel(*refs, alpha, n_seg, offs, route, n_experts):
    it = iter(refs)
    take = lambda n: [next(it) for _ in range(n)]
    x_refs, p_refs, s_refs = take(n_seg[0]), take(n_seg[1]), take(n_seg[2])
    wp_ref, ws_ref, g_ref, b_ref = take(4)
    x = _seg_read(x_refs, offs[0])
    mixed = _dot(_seg_read(p_refs, offs[1]), wp_ref[...]) + _dot(_seg_read(s_refs, offs[2]), ws_ref[...])
    h = _layer_norm(alpha * x + mixed, g_ref[...], b_ref[...])
    if not route:
        (o_ref,) = take(1)
        o_ref[...] = h
        return
    rw_ref, rb_ref, tril_ref = take(3)
    xt_ref, gate_ref, plan_ref, cnt_ref = take(4)

    @pl.when(pl.program_id(0) == 0)
    def _():
        cnt_ref[...] = jnp.zeros(cnt_ref.shape, F32)

    tm, d = h.shape
    rpt = d // LANES
    for k in range(rpt):
        xt_ref[pl.ds(k, tm, stride=rpt), :] = h[:, k * LANES:(k + 1) * LANES]
    gate, plan = _route(h, rw_ref, rb_ref, tril_ref, cnt_ref, n_experts)
    gate_ref[...] = gate
    plan_ref[...] = plan.T[:plan_ref.shape[0], :]


def _outproj_ln(x_segs, pool_segs, ssd_segs, w_pool, w_ssd, g, b, alpha, router=None):
    d = x_segs[0].shape[1]
    rows = [a.shape[0] for a in x_segs + pool_segs + ssd_segs]
    tm = _common_tile(rows, (512, 256, 128, 64))
    groups = (x_segs, pool_segs, ssd_segs)
    tiles = [_seg_tiles(s, tm) for s in groups]
    t = sum(tiles[0]) * tm
    assert all(sum(ts) * tm == t for ts in tiles)
    in_specs = [sp for s in groups for sp in _seg_specs(s, tm)]
    consts = [w_pool, w_ssd, g, b]
    row = lambda w, dt: (pl.BlockSpec((tm, w), lambda i: (i, 0)), jax.ShapeDtypeStruct((t, w), dt))
    if router is None:
        outs = [row(d, F32)]
        n_experts = 0
    else:
        rw_pad, rb_pad, n_experts = router
        tril = (jnp.arange(tm)[:, None] > jnp.arange(tm)[None, :]).astype(BF16)
        consts += [rw_pad, rb_pad, tril]
        rpt = d // LANES
        outs = [(pl.BlockSpec((tm * rpt, LANES), lambda i: (i, 0)), jax.ShapeDtypeStruct((t * rpt, LANES), F32)),
                row(LANES, F32),
                (pl.BlockSpec((SUBLANES, tm), lambda i: (0, i)), jax.ShapeDtypeStruct((SUBLANES, t), F32)),
                (_const_spec((1, LANES)), jax.ShapeDtypeStruct((1, LANES), F32))]
    in_specs += [_const_spec(c.shape) for c in consts]
    res = pl.pallas_call(
        functools.partial(_outproj_ln_kernel, alpha=alpha, n_seg=[len(s) for s in groups],
                          offs=[_seg_offsets(ts) for ts in tiles], route=router is not None, n_experts=n_experts),
        grid=(t // tm,),
        in_specs=in_specs,
        out_specs=[o[0] for o in outs],
        out_shape=[o[1] for o in outs],
        compiler_params=pltpu.CompilerParams(dimension_semantics=("arbitrary",),
                                             vmem_limit_bytes=_vmem_limit(40 * 1024 * 1024)),
        name="outproj_ln",
    )(*x_segs, *pool_segs, *ssd_segs, *consts)
    return res[0] if router is None else res


def _swiglu(xb, wg_ref, wu_ref, wd_ref):
    act = (_silu(_dot(xb, wg_ref[0])) * _dot(xb, wu_ref[0])).astype(BF16)
    return _dot(act, wd_ref[0])


def _dense_ffn_ln_kernel(*refs, alpha, offs, tiles, n_side):
    x_ref, wg_ref, wu_ref, wd_ref, g_ref, b_ref = refs[:6]
    side_in = refs[6:6 + n_side]
    out_refs = refs[6 + n_side:6 + n_side + len(tiles)]
    side_out = refs[6 + n_side + len(tiles):]
    x = x_ref[...]
    f = _swiglu(x.astype(BF16), wg_ref, wu_ref, wd_ref)
    _seg_write(out_refs, offs, tiles, _layer_norm(alpha * x + f, g_ref[...], b_ref[...]))
    _side_cast(side_in, side_out)


def _dense_ffn_ln(x, wg, wu, wd, g, b, alpha, out_rows, side_casts=()):
    t, d = x.shape
    ff = wg.shape[2]
    tm = _common_tile(out_rows, (256, 128, 64))
    tiles = [n // tm for n in out_rows]
    offs = _seg_offsets(tiles)
    steps = t // tm
    side2d, side_specs, side_bytes = _side_cast_plan(side_casts, steps)
    vmem = 3 * d * ff * 2 + 4 * tm * d * 4 + tm * ff * 12 + side_bytes
    res = pl.pallas_call(
        functools.partial(_dense_ffn_ln_kernel, alpha=alpha, offs=offs, tiles=tiles, n_side=len(side2d)),
        grid=(steps,),
        in_specs=[pl.BlockSpec((tm, d), lambda i: (i, 0)), _const_spec(wg.shape, single=True),
                  _const_spec(wu.shape, single=True), _const_spec(wd.shape, single=True),
                  _const_spec(g.shape), _const_spec(b.shape)] + side_specs,
        out_specs=[pl.BlockSpec((tm, d), lambda i, o=o, n=n: (jnp.clip(i - o, 0, n - 1), 0))
                   for o, n in zip(offs, tiles)] + side_specs,
        out_shape=[jax.ShapeDtypeStruct((n, d), F32) for n in out_rows]
                  + [jax.ShapeDtypeStruct(a.shape, BF16) for a in side2d],
        compiler_params=pltpu.CompilerParams(dimension_semantics=("arbitrary",),
                                             vmem_limit_bytes=_vmem_limit(vmem)),
        name="dense_ffn_ln",
    )(x, wg, wu, wd, g, b, *side2d)
    return list(res[:len(tiles)]), [r.reshape(a.shape) for r, a in zip(res[len(tiles):], side_casts)]


def _dispatch_kernel(ends_ref, slot_ref, x_ref, out_hbm, zeros_ref, sem, zsem, *, tokens, rpt, tile, n_experts):
    i = pl.program_id(0)

    def tail_copy(e):
        start = pl.multiple_of((ends_ref[e] - tile) * rpt, tile * rpt)
        return pltpu.make_async_copy(zeros_ref, out_hbm.at[pl.ds(start, tile * rpt)], zsem)

    def nonempty(e):
        return ends_ref[e] > (ends_ref[e - 1] if e > 0 else 0)

    def unused_copy(j):
        start = pl.multiple_of((ends_ref[n_experts - 1] + j * tile) * rpt, tile * rpt)
        return pltpu.make_async_copy(zeros_ref, out_hbm.at[pl.ds(start, tile * rpt)], zsem)

    def unused(j):
        return ends_ref[n_experts - 1] + (j + 1) * tile <= out_hbm.shape[0] // rpt

    @pl.when(i == 0)
    def _():
        zeros_ref[...] = jnp.zeros(zeros_ref.shape, zeros_ref.dtype)
        for e in range(n_experts):
            @pl.when(nonempty(e))
            def _(e=e):
                tail_copy(e).start()

            @pl.when(unused(e))
            def _(e=e):
                unused_copy(e).start()
        for e in range(n_experts):
            @pl.when(nonempty(e))
            def _(e=e):
                tail_copy(e).wait()

            @pl.when(unused(e))
            def _(e=e):
                unused_copy(e).wait()

    def issue(r, carry):
        src = x_ref.at[pl.ds(pl.multiple_of(r * rpt, rpt), rpt)]
        for k in range(TOP_K):
            dst = out_hbm.at[pl.ds(pl.multiple_of(slot_ref[0, 0, k * tokens + r] * rpt, rpt), rpt)]
            pltpu.make_async_copy(src, dst, sem).start()
        return carry

    lax.fori_loop(0, tokens, issue, 0)
    for k in range(TOP_K):
        pltpu.make_async_copy(x_ref, out_hbm.at[pl.ds(0, tokens * rpt)], sem).wait()


def _dispatch(xt, slots, ends, n_slots, tokens, rpt, n_experts):
    steps = slots.shape[0]
    return pl.pallas_call(
        functools.partial(_dispatch_kernel, tokens=tokens, rpt=rpt, tile=MOE_TILE, n_experts=n_experts),
        grid_spec=pltpu.PrefetchScalarGridSpec(
            num_scalar_prefetch=1, grid=(steps,),
            in_specs=[pl.BlockSpec((1, 1, TOP_K * tokens), lambda i, e: (i, 0, 0), memory_space=pltpu.SMEM),
                      pl.BlockSpec((tokens * rpt, LANES), lambda i, e: (i, 0))],
            out_specs=pl.BlockSpec(memory_space=pl.ANY),
            scratch_shapes=[pltpu.VMEM((MOE_TILE * rpt, LANES), xt.dtype),
                            pltpu.SemaphoreType.DMA(()), pltpu.SemaphoreType.DMA(())]),
        out_shape=jax.ShapeDtypeStruct((n_slots * rpt, LANES), xt.dtype),
        compiler_params=pltpu.CompilerParams(dimension_semantics=("arbitrary",)),
        name="moe_dispatch",
    )(ends, slots, xt)


def _moe_ffn_kernel(te_ref, na_ref, x_ref, wg_ref, wu_ref, wd_ref, o_ref, *, tm, rpt):
    active = pl.program_id(0) < na_ref[0]

    @pl.when(active)
    def _():
        xb = jnp.concatenate([x_ref[pl.ds(k, tm, stride=rpt), :] for k in range(rpt)], axis=1).astype(BF16)
        f = _swiglu(xb, wg_ref, wu_ref, wd_ref)
        for k in range(rpt):
            o_ref[pl.ds(k, tm, stride=rpt), :] = f[:, k * LANES:(k + 1) * LANES]

    @pl.when(jnp.logical_not(active))
    def _():
        o_ref[...] = jnp.zeros(o_ref.shape, o_ref.dtype)


def _moe_ffn(xs, wg, wu, wd, tile_expert, n_active, rpt):
    tm = MOE_TILE
    d, ff = wg.shape[1], wg.shape[2]
    n_tiles = xs.shape[0] // (tm * rpt)
    w_idx = lambda i, te, na: (te[i], 0, 0)
    vmem = 2 * 3 * d * ff * 2 + 4 * tm * d * 4 + tm * ff * 12 + 2 * tm * d * 4
    return pl.pallas_call(
        functools.partial(_moe_ffn_kernel, tm=tm, rpt=rpt),
        grid_spec=pltpu.PrefetchScalarGridSpec(
            num_scalar_prefetch=2, grid=(n_tiles,),
            in_specs=[pl.BlockSpec((tm * rpt, LANES), lambda i, te, na: (jnp.minimum(i, na[0] - 1), 0)),
                      pl.BlockSpec((1, d, ff), w_idx), pl.BlockSpec((1, d, ff), w_idx),
                      pl.BlockSpec((1, ff, d), w_idx)],
            out_specs=pl.BlockSpec((tm * rpt, LANES), lambda i, te, na: (i, 0))),
        out_shape=jax.ShapeDtypeStruct(xs.shape, F32),
        compiler_params=pltpu.CompilerParams(dimension_semantics=("arbitrary",),
                                             vmem_limit_bytes=_vmem_limit(vmem)),
        name="moe_ffn",
    )(tile_expert, n_active, xs, wg, wu, wd)


def _combine_ln_kernel(slot_ref, next_ref, x_ref, gate_ref, y_hbm, g_ref, b_ref, *rest, alpha, tm, rpt, offs, tiles):
    out_refs, (buf, sem) = rest[:len(tiles)], rest[len(tiles):]
    i = pl.program_id(0)
    n_rows = TOP_K * tm * rpt

    def issue(s_ref, slot):
        def body(r, carry):
            src = y_hbm.at[pl.ds(pl.multiple_of(s_ref[0, 0, r] * rpt, rpt), rpt)]
            pltpu.make_async_copy(src, buf.at[slot, pl.ds(pl.multiple_of(r * rpt, rpt), rpt)], sem.at[slot]).start()
            return carry
        lax.fori_loop(0, TOP_K * tm, body, 0)

    @pl.when(i == 0)
    def _():
        issue(slot_ref, 0)

    @pl.when(i + 1 < pl.num_programs(0))
    def _():
        issue(next_ref, (i + 1) % 2)

    cur = i % 2
    pltpu.make_async_copy(y_hbm.at[pl.ds(0, n_rows)], buf.at[cur], sem.at[cur]).wait()
    g1 = gate_ref[:, 0:1]
    g2 = gate_ref[:, 1:2]
    cols = []
    for k in range(rpt):
        xk = x_ref[pl.ds(k, tm, stride=rpt), :]
        ya = buf[cur, pl.ds(k, tm, stride=rpt), :]
        yb = buf[cur, pl.ds(tm * rpt + k, tm, stride=rpt), :]
        cols.append(alpha * xk + (g1 * ya + g2 * yb))
    h = jnp.concatenate(cols, axis=1)
    _seg_write(out_refs, offs, tiles, _layer_norm(h, g_ref[...], b_ref[...]))


def _combine_ln(xt, gate, ys, slots, g, b, alpha, tm, rpt, out_rows):
    d = g.shape[1]
    tiles = [n // tm for n in out_rows]
    offs = _seg_offsets(tiles)
    steps = slots.shape[0]
    slot_spec = lambda f: pl.BlockSpec((1, 1, TOP_K * tm), f, memory_space=pltpu.SMEM)
    res = pl.pallas_call(
        functools.partial(_combine_ln_kernel, alpha=alpha, tm=tm, rpt=rpt, offs=offs, tiles=tiles),
        grid=(steps,),
        in_specs=[slot_spec(lambda i: (i, 0, 0)), slot_spec(lambda i: (jnp.minimum(i + 1, steps - 1), 0, 0)),
                  pl.BlockSpec((tm * rpt, LANES), lambda i: (i, 0)), pl.BlockSpec((tm, LANES), lambda i: (i, 0)),
                  pl.BlockSpec(memory_space=pl.ANY), _const_spec(g.shape), _const_spec(b.shape)],
        out_specs=[pl.BlockSpec((tm, d), lambda i, o=o, n=n: (jnp.clip(i - o, 0, n - 1), 0))
                   for o, n in zip(offs, tiles)],
        out_shape=[jax.ShapeDtypeStruct((n, d), F32) for n in out_rows],
        scratch_shapes=[pltpu.VMEM((2, TOP_K * tm * rpt, LANES), F32), pltpu.SemaphoreType.DMA((2,))],
        compiler_params=pltpu.CompilerParams(dimension_semantics=("arbitrary",),
                                             vmem_limit_bytes=_vmem_limit(32 * 1024 * 1024)),
        name="moe_combine_ln",
    )(slots, slots, xt, gate, ys, g, b)
    return list(res)


def _moe_ffn_ln(xt, gate, plan, counts, wg, wu, wd, g, b, alpha, tm, out_rows):
    n_experts = wg.shape[0]
    d = wg.shape[1]
    rpt = d // LANES
    t = gate.shape[0]
    tile = MOE_TILE
    cnt = counts[0, :n_experts].astype(jnp.int32)
    padded = ((cnt + tile - 1) // tile) * tile
    ends = jnp.cumsum(padded).astype(jnp.int32)
    starts = ends - padded
    plan = plan.astype(jnp.int32)
    slot = plan[TOP_K:2 * TOP_K]
    for e in range(n_experts):
        slot = slot + jnp.where(plan[:TOP_K] == e, starts[e], 0)
    n_slots = -(-(TOP_K * t + n_experts * (tile - 1)) // tile) * tile
    n_tiles = n_slots // tile
    n_active = ends[-1:] // tile
    tile_start = jnp.minimum(jnp.arange(n_tiles, dtype=jnp.int32), n_active[0] - 1) * tile
    tile_expert = jnp.sum((tile_start[:, None] >= ends[None, :]).astype(jnp.int32), axis=1)
    slots = jnp.swapaxes(slot.reshape(TOP_K, t // tm, tm), 0, 1).reshape(t // tm, 1, TOP_K * tm)
    xs = _dispatch(xt, slots, ends, n_slots, tm, rpt, n_experts)
    ys = _moe_ffn(xs, wg, wu, wd, tile_expert, n_active, rpt)
    return _combine_ln(xt, gate, ys, slots, g, b, alpha, tm, rpt, out_rows)


def kernel(x_prompt, x_sample, state_pool, state_conv, state_ssm, w_in, conv_w, conv_b, dt_bias, A_log, D_skip,
           ssm_norm_w, pool_w, pool_scale, w_out, ln1_g, ln1_b, ln2_g, ln2_b, ffn_w_gate, ffn_w_up, ffn_w_down,
           router_w, router_b, moe_w_gate, moe_w_up, moe_w_down):
    bp, seq, d = x_prompt.shape
    bs, n_new, _ = x_sample.shape
    depth = w_in.shape[0]
    d_pool = pool_scale.shape[1]
    d_ssm = ssm_norm_w.shape[1]
    d_xbc = conv_w.shape[2]
    n_heads = dt_bias.shape[1]
    tp, ts = bp * seq, bs * n_new
    alpha = (2.0 * depth) ** 0.25
    assert d_pool == POOL_GROUP * len(POOL_WINDOWS) and d_ssm == n_heads * HEAD_DIM
    assert d_xbc == d_ssm + 2 * N_GROUPS * D_STATE and seq % CHUNK == 0 and n_new <= CONV_W
    assert n_heads <= LANES and bs % (CHUNK // SAMPLE_ROWS) == 0 and d % LANES == 0

    consts_prompt = _ssd_consts(CHUNK, n_heads)
    consts_sample = _ssd_consts(SAMPLE_ROWS, n_heads)
    x_segs = [x_prompt.reshape(tp, d), x_sample.reshape(ts, d)]
    new_pool_p, new_conv_p, new_ssm_p, new_pool_s, new_conv_s = [], [], [], [], []
    ssm_s = None
    next_w_in = None
    for l in range(depth):
        out_rows = [tp, ts]
        n_in = w_in.shape[2]
        w_in_bf16 = w_in[l].astype(BF16) if next_w_in is None else next_w_in
        w_in_pad = jnp.pad(w_in_bf16, ((0, 0), (0, d_pool + d_ssm + d_xbc + LANES - n_in)))
        lp = dict(
            conv_w=conv_w[l], conv_b=conv_b[l][None, :],
            dt_bias=jnp.zeros((1, LANES), F32).at[0, :n_heads].set(dt_bias[l]),
            a_log=jnp.zeros((1, LANES), F32).at[0, :n_heads].set(A_log[l]),
            d_exp=jnp.repeat(D_skip[l], HEAD_DIM)[None, :], norm_w=ssm_norm_w[l][None, :],
            ssd_consts_prompt=consts_prompt, ssd_consts_sample=consts_sample)
        pw = pool_w[l].astype(BF16)
        ps = pool_scale[l][None, :]

        proj = (w_in_pad, lp["conv_w"], lp["conv_b"], lp["dt_bias"], d_pool, d_ssm, d_xbc)

        dense = l % 2 == 0
        side = ((ffn_w_gate[l // 2], ffn_w_up[l // 2], ffn_w_down[l // 2]) if dense else ()) \
            + ((w_in[l + 1],) if l + 1 < depth else ())
        (u_p, sz_p, xact_p, dt_p, c_p), cast = _in_proj(x_segs[0], *proj, seq_len=seq, side_casts=side)
        ffn_bf16 = [c[None] for c in cast[:3]] if dense else None
        next_w_in = cast[-1] if l + 1 < depth else None
        pool_p, buf_p = _pool_prompt(u_p, tp, seq, pw, ps)
        y_p, h_p = _ssd_prompt(sz_p, xact_p, dt_p, bp, seq, lp)
        new_pool_p.append(buf_p)
        new_conv_p.append(c_p)
        new_ssm_p.append(jnp.concatenate(h_p, axis=0).reshape(bp, n_heads, HEAD_DIM, D_STATE))

        u_s, sz_s, xbc_s, dt_s = _in_proj_sample(x_segs[1], w_in_pad, lp["dt_bias"], d_pool, d_ssm, d_xbc)
        u_s = u_s.reshape(bs, n_new, d_pool)
        xbc_s = xbc_s.reshape(bs, n_new, d_xbc)
        pool_ext = jnp.concatenate([state_pool[l], u_s], axis=1)
        pool_s = _pool_sample(jnp.swapaxes(pool_ext, 0, 1), pw, ps, n_new, PAST_LEN)
        pool_s = jnp.swapaxes(pool_s, 0, 1).reshape(ts, d_pool)
        conv_ext = jnp.concatenate([state_conv[l], xbc_s], axis=1)
        lead = CONV_LEAD - (CONV_W - 1)
        ext = jnp.pad(conv_ext, ((0, 0), (lead, SAMPLE_EXT_ROWS - lead - conv_ext.shape[1]), (0, 0)))
        pad_rows = lambda a: jnp.pad(a.reshape(bs, n_new, -1), ((0, 0), (0, SAMPLE_ROWS - n_new), (0, 0))
                                     ).reshape(bs * SAMPLE_ROWS, -1)
        y_s16, ssm_s = _ssd_sample(pad_rows(sz_s), ext.reshape(bs * SAMPLE_EXT_ROWS, d_xbc), pad_rows(dt_s),
                                   state_ssm.reshape(depth, bs, d_ssm, D_STATE), ssm_s, l, lp, n_new)
        y_s = y_s16.reshape(bs, SAMPLE_ROWS, d_ssm)[:, :n_new].reshape(ts, d_ssm)
        new_pool_s.append(pool_ext[:, n_new:])
        new_conv_s.append(conv_ext[:, n_new:])

        w_o = w_out[l].astype(BF16)
        ln1 = (ln1_g[l][None, :], ln1_b[l][None, :])
        g2, b2 = ln2_g[l][None, :], ln2_b[l][None, :]
        j = l // 2
        if l % 2 == 0:
            x1 = _outproj_ln(x_segs, [pool_p, pool_s], [*y_p, y_s], w_o[:d_pool], w_o[d_pool:], *ln1, alpha)
            side = (moe_w_gate[j], moe_w_up[j], moe_w_down[j]) if l + 1 < depth else ()
            x_segs, moe_bf16 = _dense_ffn_ln(x1, *ffn_bf16, g2, b2, alpha, out_rows, side)
        else:
            n_experts = router_w.shape[2]
            rw_pad = jnp.zeros((d, LANES), F32).at[:, :n_experts].set(router_w[j])
            rb_pad = jnp.zeros((1, LANES), F32).at[0, :n_experts].set(router_b[j])
            xt, gate, plan, counts = _outproj_ln(x_segs, [pool_p, pool_s], [*y_p, y_s], w_o[:d_pool], w_o[d_pool:],
                                                 *ln1, alpha, router=(rw_pad, rb_pad, n_experts))
            tm = _common_tile([tp, ts], (512, 256, 128, 64))
            x_segs = _moe_ffn_ln(xt, gate, plan, counts, *moe_bf16, g2, b2, alpha, tm, out_rows)

    y_prompt, y_sample = x_segs
    return (y_prompt.reshape(bp, seq, d), y_sample.reshape(bs, n_new, d),
            jnp.stack(new_pool_p), jnp.stack(new_conv_p), jnp.stack(new_ssm_p),
            jnp.stack(new_pool_s), jnp.stack(new_conv_s),
            ssm_s.reshape(depth, bs, n_heads, HEAD_DIM, D_STATE))
```

```python
import functools

import jax
import jax.numpy as jnp
from jax import lax
from jax.experimental import pallas as pl
from jax.experimental.pallas import tpu as pltpu

F32 = jnp.float32
BF16 = jnp.bfloat16

PAST_LEN = 16384
POOL_WINDOWS = (2, 4, 8, 16)
POOL_GROUP = 128
POOL_BUF = max(POOL_WINDOWS) - 1
HEAD_DIM = 64
N_GROUPS = 4
D_STATE = 128
CONV_W = 4
CHUNK = 128
TOP_K = 2
LN_EPS = 1e-5
RMS_EPS = 1e-6
LOG2_E = 1.4426950408889634

LANES = 128
SUBLANES = 8
V7X_VMEM_BYTES = 64 * 1024 * 1024
VMEM_CAP = V7X_VMEM_BYTES - 8 * 1024 * 1024

SAMPLE_ROWS = 16
SAMPLE_EXT_ROWS = 24
CONV_LEAD = 8
MOE_TILE = 256
SSD_WAYS = 2


def _vmem_limit(nbytes):
    return int(min(VMEM_CAP, nbytes + 6 * 1024 * 1024))


def _dot(a, b):
    return jnp.dot(a, b, preferred_element_type=F32)


def _dot_nt(a, b):
    return lax.dot_general(a, b, (((1,), (1,)), ((), ())), preferred_element_type=F32)


def _split3(v):
    hi = v.astype(BF16)
    r = v - hi.astype(F32)
    mid = r.astype(BF16)
    lo = (r - mid.astype(F32)).astype(BF16)
    return hi, mid, lo


def _sel_right(parts, m):
    out = _dot(parts[0], m)
    for p in parts[1:]:
        out = out + _dot(p, m)
    return out


def _sel_left(m, parts):
    out = _dot(m, parts[0])
    for p in parts[1:]:
        out = out + _dot(m, p)
    return out


def _silu(x):
    return x / (1.0 + jnp.exp2(x * (-LOG2_E)))


def _layer_norm(h, g, b):
    mu = jnp.mean(h, axis=-1, keepdims=True)
    d = h - mu
    var = jnp.mean(d * d, axis=-1, keepdims=True)
    return d * lax.rsqrt(var + LN_EPS) * g + b


def _pick_tile(n, candidates):
    for c in candidates:
        if n % c == 0:
            return c
    raise ValueError(f"no tile in {candidates} divides {n}")


def _const_spec(shape, single=False):
    nd = len(shape)
    if single:
        return pl.BlockSpec(shape, lambda *_: (0,) * nd, pipeline_mode=pl.Buffered(1))
    return pl.BlockSpec(shape, lambda *_: (0,) * nd)


def _seg_tiles(segs, tm, rows_per_token=1):
    return [a.shape[0] // (tm * rows_per_token) for a in segs]


def _seg_offsets(tiles):
    offs, off = [], 0
    for n in tiles:
        offs.append(off)
        off += n
    return offs


def _seg_specs(segs, tm, rows_per_token=1):
    tiles = _seg_tiles(segs, tm, rows_per_token)
    return [pl.BlockSpec((tm * rows_per_token, a.shape[1]),
                         lambda i, *_, o=o, n=n: (jnp.clip(i - o, 0, n - 1), 0))
            for a, o, n in zip(segs, _seg_offsets(tiles), tiles)]


def _seg_read(refs, offs):
    v = refs[0][...]
    for r, o in zip(refs[1:], offs[1:]):
        v = jnp.where(pl.program_id(0) >= o, r[...], v)
    return v


def _seg_write(refs, offs, tiles, v):
    i = pl.program_id(0)
    for r, o, n in zip(refs, offs, tiles):
        @pl.when(jnp.logical_and(i >= o, i < o + n))
        def _(r=r):
            r[...] = v.astype(r.dtype)


def _common_tile(row_counts, candidates):
    for c in candidates:
        if all(n % c == 0 for n in row_counts):
            return c
    raise ValueError(f"no tile in {candidates} divides all of {row_counts}")


INPROJ_CHUNK = 512


def _row_blocks(rows, steps, align):
    for n in range(min(steps, rows), 0, -1):
        if rows % n == 0 and (rows // n) % align == 0:
            return n
    raise ValueError(f"cannot split {rows} rows into at most {steps} blocks of a multiple of {align} rows")


def _side_cast_plan(entries, steps):
    side2d, in_specs, out_specs, out_shapes = [], [], [], []
    for stacked, layer in entries:
        rows = 1
        for n in stacked.shape[1:-1]:
            rows *= n
        cols = stacked.shape[-1]
        nb = _row_blocks(rows, steps, 2 * SUBLANES)
        side2d.append(stacked.reshape(-1, cols))
        in_specs.append(pl.BlockSpec((rows // nb, cols), lambda i, nb=nb, layer=layer: (layer * nb + jnp.minimum(i, nb - 1), 0)))
        out_specs.append(pl.BlockSpec((rows // nb, cols), lambda i, nb=nb: (jnp.minimum(i, nb - 1), 0)))
        out_shapes.append(jax.ShapeDtypeStruct((rows, cols), BF16))
    vmem = sum(2 * (4 + 2) * s.block_shape[0] * s.block_shape[1] for s in in_specs)
    return side2d, in_specs, out_specs, out_shapes, vmem


def _side_cast(side_in, side_out):
    for src, dst in zip(side_in, side_out):
        dst[...] = src[...].astype(dst.dtype)


def _inproj_kernel(*refs, splits, tiles_per_seq, n_side):
    x_ref, w_ref, convw_ref, convb_ref, dtb_ref = refs[:5]
    side_in = refs[5:5 + n_side]
    u_ref, sz_ref, xo_ref, dt_ref, cstate_ref = refs[5 + n_side:10 + n_side]
    side_out, ext_ref = refs[10 + n_side:10 + 2 * n_side], refs[10 + 2 * n_side]
    (u0, u1), (z0, z1), (c0, c1), (d0, d1) = splits
    tm = x_ref.shape[0]
    xb = x_ref[...].astype(BF16)
    u_ref[...] = _dot(xb, w_ref[:, u0:u1])
    dt_ref[...] = _softplus(_dot(xb, w_ref[:, d0:d1]) + dtb_ref[...])
    for lo in range(z0, z1, INPROJ_CHUNK):
        sz_ref[:, lo - z0:lo - z0 + INPROJ_CHUNK] = _silu(_dot(xb, w_ref[:, lo:lo + INPROJ_CHUNK]))
    _side_cast(side_in, side_out)

    pos = pl.program_id(0) % tiles_per_seq

    @pl.when(pos == 0)
    def _():
        ext_ref[0:CONV_LEAD, :] = jnp.zeros((CONV_LEAD, ext_ref.shape[1]), F32)

    @pl.when(pos > 0)
    def _():
        ext_ref[0:CONV_LEAD, :] = ext_ref[tm:tm + CONV_LEAD, :]

    first = CONV_LEAD - (CONV_W - 1)
    for lo in range(c0, c1, INPROJ_CHUNK):
        cols = slice(lo - c0, lo - c0 + INPROJ_CHUNK)
        ext_ref[CONV_LEAD:CONV_LEAD + tm, cols] = _dot(xb, w_ref[:, lo:lo + INPROJ_CHUNK])
        rows = ext_ref[:, cols]
        acc = convb_ref[:, cols] + (pltpu.roll(rows, CONV_W - 1, axis=0)[CONV_LEAD:] * convw_ref[0:1, cols])
        for k in range(1, CONV_W - 1):
            acc = acc + pltpu.roll(rows, CONV_W - 1 - k, axis=0)[CONV_LEAD:] * convw_ref[k:k + 1, cols]
        acc = acc + rows[CONV_LEAD:] * convw_ref[CONV_W - 1:CONV_W, cols]
        xo_ref[:, cols] = _silu(acc)

    @pl.when(pos == tiles_per_seq - 1)
    def _():
        cstate_ref[0] = ext_ref[CONV_LEAD + tm - (CONV_W - 1):CONV_LEAD + tm, :]


def _inproj_splits(w_pad, d_pool, d_ssm, d_xbc):
    assert d_ssm % INPROJ_CHUNK == 0 and d_xbc % INPROJ_CHUNK == 0
    return ((0, d_pool), (d_pool, d_pool + d_ssm), (d_pool + d_ssm, d_pool + d_ssm + d_xbc),
            (d_pool + d_ssm + d_xbc, w_pad.shape[1]))


def _in_proj(x, w_pad, conv_w, conv_b, dt_bias, d_pool, d_ssm, d_xbc, seq_len, side_casts=()):
    t, d = x.shape
    n = w_pad.shape[1]
    tm = _pick_tile(seq_len, (512, 256, 128, 64))
    splits = _inproj_splits(w_pad, d_pool, d_ssm, d_xbc)
    widths = [hi - lo for lo, hi in splits]
    tiles_per_seq = seq_len // tm
    side2d, side_in, side_out, side_shapes, side_bytes = _side_cast_plan(side_casts, t // tm)
    vmem = (2 * tm * d * 4 + d * n * 2 + 2 * tm * n * 4 + 2 * tm * d_xbc * 4 + 4 * tm * INPROJ_CHUNK * 4
            + side_bytes)
    res = pl.pallas_call(
        functools.partial(_inproj_kernel, splits=splits, tiles_per_seq=tiles_per_seq, n_side=len(side2d)),
        grid=(t // tm,),
        in_specs=[pl.BlockSpec((tm, d), lambda i: (i, 0)), _const_spec((d, n), single=True),
                  _const_spec(conv_w.shape), _const_spec(conv_b.shape), _const_spec(dt_bias.shape)] + side_in,
        out_specs=[pl.BlockSpec((tm, w), lambda i: (i, 0)) for w in widths]
                  + [pl.BlockSpec((1, CONV_W - 1, d_xbc), lambda i: (i // tiles_per_seq, 0, 0))] + side_out,
        out_shape=[jax.ShapeDtypeStruct((t, w), F32) for w in widths]
                  + [jax.ShapeDtypeStruct((t // seq_len, CONV_W - 1, d_xbc), F32)] + side_shapes,
        scratch_shapes=[pltpu.VMEM((tm + CONV_LEAD, d_xbc), F32)],
        compiler_params=pltpu.CompilerParams(dimension_semantics=("arbitrary",),
                                             vmem_limit_bytes=_vmem_limit(vmem)),
        name="in_proj",
    )(x, w_pad, conv_w, conv_b, dt_bias, *side2d)
    return list(res[:5]), [r.reshape(a.shape[1:]) for r, (a, _) in zip(res[5:], side_casts)]


def _inproj_sample_kernel(x_ref, w_ref, dtb_ref, u_ref, sz_ref, xbc_ref, dt_ref, *, splits):
    (u0, u1), (z0, z1), (c0, c1), (d0, d1) = splits
    xb = x_ref[...].astype(BF16)
    u_ref[...] = _dot(xb, w_ref[:, u0:u1])
    sz_ref[...] = _silu(_dot(xb, w_ref[:, z0:z1]))
    xbc_ref[...] = _dot(xb, w_ref[:, c0:c1])
    dt_ref[...] = _softplus(_dot(xb, w_ref[:, d0:d1]) + dtb_ref[...])


def _in_proj_sample(x, w_pad, dt_bias, d_pool, d_ssm, d_xbc):
    t, d = x.shape
    n = w_pad.shape[1]
    tm = _pick_tile(t, (512, 256, 128, 64))
    splits = _inproj_splits(w_pad, d_pool, d_ssm, d_xbc)
    widths = [hi - lo for lo, hi in splits]
    vmem = 2 * tm * d * 4 + 2 * d * n * 2 + 4 * tm * n * 4
    return pl.pallas_call(
        functools.partial(_inproj_sample_kernel, splits=splits),
        grid=(t // tm,),
        in_specs=[pl.BlockSpec((tm, d), lambda i: (i, 0)), _const_spec((d, n)), _const_spec(dt_bias.shape)],
        out_specs=[pl.BlockSpec((tm, w), lambda i: (i, 0)) for w in widths],
        out_shape=[jax.ShapeDtypeStruct((t, w), F32) for w in widths],
        compiler_params=pltpu.CompilerParams(dimension_semantics=("arbitrary",),
                                             vmem_limit_bytes=_vmem_limit(vmem)),
        name="in_proj_sample",
    )(x, w_pad, dt_bias)


def _pool_prompt_kernel(u_ref, w_ref, scale_ref, o_ref, buf_ref, ext_ref, *, tl):
    j = pl.program_id(1)

    @pl.when(j == pl.num_programs(1) - 1)
    def _():
        buf_ref[0] = u_ref[tl - POOL_BUF:tl, :]

    @pl.when(j == 0)
    def _():
        ext_ref[0:16, :] = jnp.zeros((16, ext_ref.shape[1]), F32)

    @pl.when(j > 0)
    def _():
        ext_ref[0:16, :] = ext_ref[tl:tl + 16, :]

    ext_ref[16:16 + tl, :] = u_ref[...]
    pos = j * tl + lax.broadcasted_iota(jnp.int32, (tl, POOL_GROUP), 0)
    for g, win in enumerate(POOL_WINDOWS):
        lanes = slice(g * POOL_GROUP, (g + 1) * POOL_GROUP)
        rows = ext_ref[:, lanes]
        cur = rows[16:]
        span = 1
        while span < win:
            rows = rows + pltpu.roll(rows, span, axis=0)
            span *= 2
        acc = rows[16:]
        cnt = jnp.minimum(pos + 1, win).astype(F32)
        diff = (acc / cnt - cur).astype(BF16)
        out = _dot(diff, w_ref[g]) * scale_ref[:, lanes]
        o_ref[:, lanes] = out.astype(o_ref.dtype)


def _pool_prompt(u, n_rows, seq_len, pool_w, pool_scale):
    d_pool = u.shape[1]
    tl = _pick_tile(seq_len, (512, 256, 128))
    nj = seq_len // tl
    nb = n_rows // seq_len
    return pl.pallas_call(
        functools.partial(_pool_prompt_kernel, tl=tl),
        grid=(nb, nj),
        in_specs=[pl.BlockSpec((tl, d_pool), lambda b, j: (b * nj + j, 0)),
                  _const_spec(pool_w.shape), _const_spec(pool_scale.shape)],
        out_specs=[pl.BlockSpec((tl, d_pool), lambda b, j: (b * nj + j, 0)),
                   pl.BlockSpec((1, POOL_BUF, d_pool), lambda b, j: (b, 0, 0))],
        out_shape=[jax.ShapeDtypeStruct((n_rows, d_pool), BF16),
                   jax.ShapeDtypeStruct((nb, POOL_BUF, d_pool), F32)],
        scratch_shapes=[pltpu.VMEM((tl + 16, d_pool), F32)],
        compiler_params=pltpu.CompilerParams(dimension_semantics=("arbitrary", "arbitrary")),
        name="pool_prompt",
    )(u, pool_w, pool_scale)


def _pool_sample_kernel(ext_ref, w_ref, scale_ref, o_ref, *, n_new, start):
    for t in range(n_new):
        for g, win in enumerate(POOL_WINDOWS):
            lanes = slice(g * POOL_GROUP, (g + 1) * POOL_GROUP)
            cur = ext_ref[POOL_BUF + t, :, lanes]
            acc = cur
            for k in range(1, win):
                acc = acc + ext_ref[POOL_BUF + t - k, :, lanes]
            cnt = float(min(start + t + 1, win))
            diff = (acc / cnt - cur).astype(BF16)
            out = _dot(diff, w_ref[g]) * scale_ref[:, lanes]
            o_ref[t, :, lanes] = out.astype(o_ref.dtype)


def _pool_sample(ext_t, pool_w, pool_scale, n_new, start):
    rows, bs, d_pool = ext_t.shape
    return pl.pallas_call(
        functools.partial(_pool_sample_kernel, n_new=n_new, start=start),
        grid=(1,),
        in_specs=[_const_spec(ext_t.shape), _const_spec(pool_w.shape), _const_spec(pool_scale.shape)],
        out_specs=_const_spec((n_new, bs, d_pool)),
        out_shape=jax.ShapeDtypeStruct((n_new, bs, d_pool), BF16),
        compiler_params=pltpu.CompilerParams(dimension_semantics=("arbitrary",)),
        name="pool_sample",
    )(ext_t, pool_w, pool_scale)


def _conv_silu(window, w_ref, b_ref):
    acc = b_ref[...] + window(0) * w_ref[0:1, :]
    for k in range(1, CONV_W):
        acc = acc + window(k) * w_ref[k:k + 1, :]
    return _silu(acc)


def _softplus(x):
    return jnp.maximum(x, 0.0) + jnp.log1p(jnp.exp(-jnp.abs(x)))


def _ssd_chunk(xact, sz, dt, prm, seq_rows, read_state, write_state, write_out):
    (cums_ref, spread_ref, alog_ref, dexp_ref, normw_ref) = prm
    q = xact.shape[0]
    d_ssm = sz.shape[1]
    gw = d_ssm // N_GROUPS
    n_seq = q // seq_rows
    xs = xact[:, :d_ssm]
    bm = xact[:, d_ssm:d_ssm + N_GROUPS * D_STATE]
    cm = xact[:, d_ssm + N_GROUPS * D_STATE:]

    a = dt * (-jnp.exp(alog_ref[...]) * LOG2_E)
    sums = _dot(cums_ref[...], jnp.concatenate(_split3(a), axis=0))
    cum, rcum = sums[:q], sums[q:]
    w = jnp.exp2(rcum) * dt
    cum_t = cum.T
    dt_t = dt.T
    two_terms = lambda v: jnp.concatenate(_split3(v)[:2], axis=1)
    spread = _dot(jnp.concatenate([two_terms(w), two_terms(jnp.exp2(cum))], axis=0), spread_ref[...])
    wx = xs * spread[:q]
    ecum_e = spread[q:]

    ii = lax.broadcasted_iota(jnp.int32, (q, q), 0)
    jj = lax.broadcasted_iota(jnp.int32, (q, q), 1)
    mask = ii >= jj
    if n_seq > 1:
        mask = jnp.logical_and(mask, (ii // seq_rows) == (jj // seq_rows))
    lo_half = lax.broadcasted_iota(jnp.int32, (q, LANES), 1) < HEAD_DIM
    col = lax.broadcasted_iota(jnp.int32, (gw, q), 1)

    for g in range(N_GROUPS):
        gcols = slice(g * gw, (g + 1) * gw)
        bg = bm[:, g * D_STATE:(g + 1) * D_STATE].astype(BF16)
        cg = cm[:, g * D_STATE:(g + 1) * D_STATE].astype(BF16)
        cb = _dot_nt(cg, bg)
        ydiag = []
        end_decay = {}
        for pr in range(gw // LANES):
            blk = g * (gw // LANES) + pr
            xp = xs[:, blk * LANES:(blk + 1) * LANES]
            mix = []
            for half in range(2):
                h = 2 * blk + half
                colb = jnp.broadcast_to(cum[:, h:h + 1], (q, q))
                for s in range(n_seq):
                    last = (s + 1) * seq_rows - 1
                    end_decay[h, s] = jnp.exp2(colb[last:last + 1, :])
                dec = jnp.exp2(jnp.where(mask, colb - cum_t[h:h + 1, :], -jnp.inf))
                mix.append((cb * dec * dt_t[h:h + 1, :]).astype(BF16))
            x2 = jnp.concatenate([jnp.where(lo_half, xp, 0.0), jnp.where(lo_half, 0.0, xp)], axis=0).astype(BF16)
            ydiag.append(_dot(jnp.concatenate(mix, axis=1), x2))
        wx_t = wx[:, gcols].T
        yoff_rows = []
        for s in range(n_seq):
            r0 = s * seq_rows
            st = read_state(s, g)
            yoff_rows.append(_dot_nt(cg[r0:r0 + seq_rows, :], st.astype(BF16)))
            scale = jnp.concatenate(
                [jnp.broadcast_to(end_decay[h, s], (HEAD_DIM, D_STATE))
                 for h in range(g * (gw // HEAD_DIM), (g + 1) * (gw // HEAD_DIM))], axis=0)
            wsel = wx_t
            if n_seq > 1:
                wsel = jnp.where(jnp.logical_and(col >= r0, col < r0 + seq_rows), wx_t, 0.0)
            write_state(s, g, st * scale + _dot(wsel.astype(BF16), bg))
        yoff = yoff_rows[0] if n_seq == 1 else jnp.concatenate(yoff_rows, axis=0)
        y = jnp.concatenate(ydiag, axis=1) + yoff * ecum_e[:, gcols] + xs[:, gcols] * dexp_ref[:, gcols]
        gz = y * sz[:, gcols]
        ms = jnp.sum(gz * gz, axis=-1, keepdims=True) * (1.0 / gw)
        write_out(g, gz * lax.rsqrt(ms + RMS_EPS) * normw_ref[:, gcols])


def _ssd_prompt_kernel(*refs, gw, ways):
    ins, prm = refs[:3 * ways], refs[3 * ways:3 * ways + 5]
    outs, h_ref = refs[3 * ways + 5:5 * ways + 5], refs[5 * ways + 5]
    c = pl.program_id(1)

    @pl.when(c == 0)
    def _():
        h_ref[...] = jnp.zeros(h_ref.shape, F32)

    for k in range(ways):
        sz_ref, xact_ref, dt_ref = ins[3 * k:3 * k + 3]
        y_ref = outs[2 * k]

        def read_state(s, g, k=k):
            return h_ref[k, g * gw:(g + 1) * gw, :]

        def write_state(s, g, v, k=k):
            h_ref[k, g * gw:(g + 1) * gw, :] = v

        def write_out(g, v, y_ref=y_ref):
            y_ref[:, g * gw:(g + 1) * gw] = v.astype(y_ref.dtype)

        _ssd_chunk(xact_ref[...], sz_ref[...], dt_ref[...], prm, sz_ref.shape[0], read_state, write_state, write_out)

    @pl.when(c == pl.num_programs(1) - 1)
    def _():
        for k in range(ways):
            outs[2 * k + 1][0] = h_ref[k]


def _ssd_sample_kernel(sz_ref, ext_ref, dt_ref, hin_ref, convw_ref, convb_ref,
                       cums_ref, spread_ref, alog_ref, dexp_ref, normw_ref, *rest, gw, n_new):
    y_ref, hout_ref = rest[-2:]
    q = sz_ref.shape[0]
    n_seq = q // SAMPLE_ROWS
    first = CONV_LEAD - (CONV_W - 1)

    def window(k):
        return jnp.concatenate(
            [ext_ref[s * SAMPLE_EXT_ROWS + first + k:s * SAMPLE_EXT_ROWS + first + k + SAMPLE_ROWS, :]
             for s in range(n_seq)], axis=0)

    xact = _conv_silu(window, convw_ref, convb_ref)
    row = lax.broadcasted_iota(jnp.int32, (q, LANES), 0)
    dt = jnp.where((row % SAMPLE_ROWS) < n_new, dt_ref[...], 0.0)

    def read_state(s, g):
        return hin_ref[s, g * gw:(g + 1) * gw, :]

    def write_state(s, g, v):
        hout_ref[s, g * gw:(g + 1) * gw, :] = v

    prm = (cums_ref, spread_ref, alog_ref, dexp_ref, normw_ref)

    def write_out(g, v):
        y_ref[:, g * gw:(g + 1) * gw] = v.astype(y_ref.dtype)

    _ssd_chunk(xact, sz_ref[...], dt, prm, SAMPLE_ROWS, read_state, write_state, write_out)


def _ssd_consts(seq_rows, n_heads):
    q = CHUNK
    i = jnp.arange(q)[:, None]
    j = jnp.arange(q)[None, :]
    same = (i // seq_rows) == (j // seq_rows)
    tri = jnp.logical_and(same, j <= i).astype(BF16)
    tris = jnp.logical_and(same, j > i).astype(BF16)
    cums = jnp.tile(jnp.concatenate([tri, tris], axis=0), (1, 3))
    hrow = jnp.arange(LANES)[:, None]
    sel64 = (hrow == (jnp.arange(n_heads * HEAD_DIM)[None, :] // HEAD_DIM)).astype(BF16)
    spread = jnp.tile(sel64, (2, 1))
    return cums, spread


def _ssd_param_specs(prm_arrays):
    return [_const_spec(a.shape) for a in prm_arrays]


def _ssd_prompt(sz, xact, dt, n_seq, seq_len, lp):
    d_ssm = sz.shape[1]
    d_xbc = xact.shape[1]
    q = CHUNK
    nc = seq_len // q
    gw = d_ssm // N_GROUPS
    prm = (*lp["ssd_consts_prompt"], lp["a_log"], lp["d_exp"], lp["norm_w"])
    ways = SSD_WAYS if n_seq % SSD_WAYS == 0 else 1
    per = n_seq // ways
    in_specs, out_specs, out_shape = [], [], []
    for k in range(ways):
        rows = lambda b, c, k=k: ((k * per + b) * nc + c, 0)
        in_specs += [pl.BlockSpec((q, d_ssm), rows), pl.BlockSpec((q, d_xbc), rows), pl.BlockSpec((q, LANES), rows)]
        out_specs += [pl.BlockSpec((q, d_ssm), lambda b, c: (b * nc + c, 0)),
                      pl.BlockSpec((1, d_ssm, D_STATE), lambda b, c: (b, 0, 0))]
        out_shape += [jax.ShapeDtypeStruct((per * seq_len, d_ssm), BF16),
                      jax.ShapeDtypeStruct((per, d_ssm, D_STATE), F32)]
    res = pl.pallas_call(
        functools.partial(_ssd_prompt_kernel, gw=gw, ways=ways),
        grid=(per, nc),
        in_specs=in_specs + _ssd_param_specs(prm),
        out_specs=out_specs,
        out_shape=out_shape,
        scratch_shapes=[pltpu.VMEM((ways, d_ssm, D_STATE), F32)],
        compiler_params=pltpu.CompilerParams(dimension_semantics=("arbitrary", "arbitrary"),
                                             vmem_limit_bytes=_vmem_limit(48 * 1024 * 1024)),
        name="ssd_prompt",
    )(*([sz, xact, dt] * ways), *prm)
    return list(res[0::2]), list(res[1::2])


def _ssd_sample(z16, ext, dtr16, h_all, h_new, layer, lp, n_new):
    d_ssm = z16.shape[1]
    d_xbc = ext.shape[1]
    q = CHUNK
    spc = q // SAMPLE_ROWS
    bs = h_all.shape[1]
    gw = d_ssm // N_GROUPS
    prm = (lp["conv_w"], lp["conv_b"], *lp["ssd_consts_sample"], lp["a_log"], lp["d_exp"], lp["norm_w"])
    rows = lambda i: (i, 0)
    slab =pl.BlockSpec((None, spc, d_ssm, D_STATE), lambda i: (layer, i, 0, 0))
    state_bytes = spc * d_ssm * D_STATE * 4
    prev = [] if h_new is None else [h_new]
    n_in = 4 + len(prm)
    return pl.pallas_call(
        functools.partial(_ssd_sample_kernel, gw=gw, n_new=n_new),
        grid=(bs // spc,),
        in_specs=[pl.BlockSpec((q, d_ssm), rows), pl.BlockSpec((spc * SAMPLE_EXT_ROWS, d_xbc), rows),
                  pl.BlockSpec((q, LANES), rows), slab] + _ssd_param_specs(prm)
                 + [pl.BlockSpec(memory_space=pl.ANY) for _ in prev],
        out_specs=[pl.BlockSpec((q, d_ssm), rows), slab],
        out_shape=[jax.ShapeDtypeStruct((bs * SAMPLE_ROWS, d_ssm), BF16),
                   jax.ShapeDtypeStruct(h_all.shape, F32)],
        input_output_aliases={n_in: 1} if prev else {},
        compiler_params=pltpu.CompilerParams(dimension_semantics=("arbitrary",),
                                             vmem_limit_bytes=_vmem_limit(4 * state_bytes + 20 * 1024 * 1024)),
        name="ssd_sample",
    )(z16, ext, dtr16, h_all, *prm, *prev)


def _route(h, rw_ref, rb_ref, tril_ref, cnt_ref, n_experts):
    xh = h.astype(BF16)
    xl = (h - xh.astype(F32)).astype(BF16)
    w = rw_ref[...]
    wh = w.astype(BF16)
    wl = (w - wh.astype(F32)).astype(BF16)
    hi = _dot(xh, jnp.concatenate([wh, wl], axis=1))
    logits = hi[:, :LANES] + (hi[:, LANES:] + _dot(xl, wh)) + rb_ref[...]
    lane = lax.broadcasted_iota(jnp.int32, logits.shape, 1)
    lane_f = lane.astype(F32)
    logits = jnp.where(lane < n_experts, logits, -jnp.inf)
    m1 = jnp.max(logits, axis=-1, keepdims=True)
    i1 = jnp.min(jnp.where(logits == m1, lane_f, float(LANES)), axis=-1, keepdims=True)
    rest = jnp.where(lane_f == i1, -jnp.inf, logits)
    m2 = jnp.max(rest, axis=-1, keepdims=True)
    i2 = jnp.min(jnp.where(rest == m2, lane_f, float(LANES)), axis=-1, keepdims=True)
    e2 = jnp.exp(m2 - m1)
    den = 1.0 + e2
    oh1 = jnp.where(lane_f == i1, 1.0, 0.0)
    oh2 = jnp.where(lane_f == i2, 1.0, 0.0)
    before = _dot(tril_ref[...], jnp.concatenate([oh1, oh2], axis=1).astype(BF16))
    before1, before2 = before[:, :LANES], before[:, LANES:]
    c1 = jnp.sum(oh1, axis=0, keepdims=True)
    c2 = jnp.sum(oh2, axis=0, keepdims=True)
    base = cnt_ref[...]
    r1 = jnp.sum(oh1 * (before1 + base), axis=-1, keepdims=True)
    r2 = jnp.sum(oh2 * (before2 + (base + c1)), axis=-1, keepdims=True)
    cnt_ref[...] = base + (c1 + c2)
    gate = jnp.where(lane == 0, 1.0 / den, jnp.where(lane == 1, e2 / den, 0.0))
    plan = jnp.where(lane == 0, i1, jnp.where(lane == 1, i2, jnp.where(lane == 2, r1, jnp.where(lane == 3, r2, 0.0))))
    return gate, plan


def _outproj_ln_kernel(*refs, alpha, n_seg, offs, route, n_experts):
    it = iter(refs)
    take = lambda n: [next(it) for _ in range(n)]
    x_refs, p_refs, s_refs = take(n_seg[0]), take(n_seg[1]), take(n_seg[2])
    wp_ref, ws_ref, g_ref, b_ref = take(4)
    x = _seg_read(x_refs, offs[0])
    mixed = _dot(_seg_read(p_refs, offs[1]), wp_ref[...]) + _dot(_seg_read(s_refs, offs[2]), ws_ref[...])
    h = _layer_norm(alpha * x + mixed, g_ref[...], b_ref[...])
    if not route:
        (o_ref,) = take(1)
        o_ref[...] = h
        return
    rw_ref, rb_ref, tril_ref = take(3)
    xt_ref, gate_ref, plan_ref, cnt_ref = take(4)

    @pl.when(pl.program_id(0) == 0)
    def _():
        cnt_ref[...] = jnp.zeros(cnt_ref.shape, F32)

    tm, d = h.shape
    rpt = d // LANES
    for k in range(rpt):
        xt_ref[pl.ds(k, tm, stride=rpt), :] = h[:, k * LANES:(k + 1) * LANES]
    gate, plan = _route(h, rw_ref, rb_ref, tril_ref, cnt_ref, n_experts)
    gate_ref[...] = gate
    plan_ref[...] = plan.T[:plan_ref.shape[0], :]


def _outproj_ln(x_segs, pool_segs, ssd_segs, w_pool, w_ssd, g, b, alpha, router=None):
    d = x_segs[0].shape[1]
    rows = [a.shape[0] for a in x_segs + pool_segs + ssd_segs]
    tm = _common_tile(rows, (512, 256, 128, 64))
    groups = (x_segs, pool_segs, ssd_segs)
    tiles = [_seg_tiles(s, tm) for s in groups]
    t = sum(tiles[0]) * tm
    assert all(sum(ts) * tm == t for ts in tiles)
    in_specs = [sp for s in groups for sp in _seg_specs(s, tm)]
    consts = [w_pool, w_ssd, g, b]
    row = lambda w, dt: (pl.BlockSpec((tm, w), lambda i: (i, 0)), jax.ShapeDtypeStruct((t, w), dt))
    if router is None:
        outs = [row(d, F32)]
        n_experts = 0
    else:
        rw_pad, rb_pad, n_experts = router
        tril = (jnp.arange(tm)[:, None] > jnp.arange(tm)[None, :]).astype(BF16)
        consts += [rw_pad, rb_pad, tril]
        rpt = d // LANES
        outs = [(pl.BlockSpec((tm * rpt, LANES), lambda i: (i, 0)), jax.ShapeDtypeStruct((t * rpt, LANES), F32)),
                row(LANES, F32),
                (pl.BlockSpec((SUBLANES, tm), lambda i: (0, i)), jax.ShapeDtypeStruct((SUBLANES, t), F32)),
                (_const_spec((1, LANES)), jax.ShapeDtypeStruct((1, LANES), F32))]
    in_specs += [_const_spec(c.shape) for c in consts]
    res = pl.pallas_call(
        functools.partial(_outproj_ln_kernel, alpha=alpha, n_seg=[len(s) for s in groups],
                          offs=[_seg_offsets(ts) for ts in tiles], route=router is not None, n_experts=n_experts),
        grid=(t // tm,),
        in_specs=in_specs,
        out_specs=[o[0] for o in outs],
        out_shape=[o[1] for o in outs],
        compiler_params=pltpu.CompilerParams(dimension_semantics=("arbitrary",),
                                             vmem_limit_bytes=_vmem_limit(40 * 1024 * 1024)),
        name="outproj_ln",
    )(*x_segs, *pool_segs, *ssd_segs, *consts)
    return res[0] if router is None else res


def _swiglu(xb, wg_ref, wu_ref, wd_ref):
    act = (_silu(_dot(xb, wg_ref[0])) * _dot(xb, wu_ref[0])).astype(BF16)
    return _dot(act, wd_ref[0])


def _dense_ffn_ln_kernel(*refs, alpha, offs, tiles, n_side):
    x_ref, wg_ref, wu_ref, wd_ref, g_ref, b_ref = refs[:6]
    side_in = refs[6:6 + n_side]
    out_refs = refs[6 + n_side:6 + n_side + len(tiles)]
    side_out = refs[6 + n_side + len(tiles):]
    x = x_ref[...]
    f = _swiglu(x.astype(BF16), wg_ref, wu_ref, wd_ref)
    _seg_write(out_refs, offs, tiles, _layer_norm(alpha * x + f, g_ref[...], b_ref[...]))
    _side_cast(side_in, side_out)


def _dense_ffn_ln(x, wg, wu, wd, g, b, alpha, out_rows, side_casts=()):
    t, d = x.shape
    ff = wg.shape[2]
    tm = _common_tile(out_rows, (256, 128, 64))
    tiles = [n // tm for n in out_rows]
    offs = _seg_offsets(tiles)
    steps = t // tm
    side2d, side_in, side_out, side_shapes, side_bytes = _side_cast_plan(side_casts, steps)
    vmem = 3 * d * ff * 2 + 4 * tm * d * 4 + tm * ff * 12 + side_bytes
    res = pl.pallas_call(
        functools.partial(_dense_ffn_ln_kernel, alpha=alpha, offs=offs, tiles=tiles, n_side=len(side2d)),
        grid=(steps,),
        in_specs=[pl.BlockSpec((tm, d), lambda i: (i, 0)), _const_spec(wg.shape, single=True),
                  _const_spec(wu.shape, single=True), _const_spec(wd.shape, single=True),
                  _const_spec(g.shape), _const_spec(b.shape)] + side_in,
        out_specs=[pl.BlockSpec((tm, d), lambda i, o=o, n=n: (jnp.clip(i - o, 0, n - 1), 0))
                   for o, n in zip(offs, tiles)] + side_out,
        out_shape=[jax.ShapeDtypeStruct((n, d), F32) for n in out_rows] + side_shapes,
        compiler_params=pltpu.CompilerParams(dimension_semantics=("arbitrary",),
                                             vmem_limit_bytes=_vmem_limit(vmem)),
        name="dense_ffn_ln",
    )(x, wg, wu, wd, g, b, *side2d)
    return list(res[:len(tiles)]), [r.reshape(a.shape[1:]) for r, (a, _) in zip(res[len(tiles):], side_casts)]


def _dispatch_kernel(ends_ref, slot_ref, x_ref, out_hbm, zeros_ref, sem, zsem, *, tokens, rpt, tile, n_experts):
    i = pl.program_id(0)

    def tail_copy(e):
        start = pl.multiple_of((ends_ref[e] - tile) * rpt, tile * rpt)
        return pltpu.make_async_copy(zeros_ref, out_hbm.at[pl.ds(start, tile * rpt)], zsem)

    def nonempty(e):
        return ends_ref[e] > (ends_ref[e - 1] if e > 0 else 0)

    def unused_copy(j):
        start = pl.multiple_of((ends_ref[n_experts - 1] + j * tile) * rpt, tile * rpt)
        return pltpu.make_async_copy(zeros_ref, out_hbm.at[pl.ds(start, tile * rpt)], zsem)

    def unused(j):
        return ends_ref[n_experts - 1] + (j + 1) * tile <= out_hbm.shape[0] // rpt

    @pl.when(i == 0)
    def _():
        zeros_ref[...] = jnp.zeros(zeros_ref.shape, zeros_ref.dtype)
        for e in range(n_experts):
            @pl.when(nonempty(e))
            def _(e=e):
                tail_copy(e).start()

            @pl.when(unused(e))
            def _(e=e):
                unused_copy(e).start()
        for e in range(n_experts):
            @pl.when(nonempty(e))
            def _(e=e):
                tail_copy(e).wait()

            @pl.when(unused(e))
            def _(e=e):
                unused_copy(e).wait()

    def issue(r, carry):
        src = x_ref.at[pl.ds(pl.multiple_of(r * rpt, rpt), rpt)]
        for k in range(TOP_K):
            dst = out_hbm.at[pl.ds(pl.multiple_of(slot_ref[0, 0, k * tokens + r] * rpt, rpt), rpt)]
            pltpu.make_async_copy(src, dst, sem).start()
        return carry

    lax.fori_loop(0, tokens, issue, 0)
    for k in range(TOP_K):
        pltpu.make_async_copy(x_ref, out_hbm.at[pl.ds(0, tokens * rpt)], sem).wait()


def _dispatch(xt, slots, ends, n_slots, tokens, rpt, n_experts):
    steps = slots.shape[0]
    return pl.pallas_call(
        functools.partial(_dispatch_kernel, tokens=tokens, rpt=rpt, tile=MOE_TILE, n_experts=n_experts),
        grid_spec=pltpu.PrefetchScalarGridSpec(
            num_scalar_prefetch=1, grid=(steps,),
            in_specs=[pl.BlockSpec((1, 1, TOP_K * tokens), lambda i, e: (i, 0, 0), memory_space=pltpu.SMEM),
                      pl.BlockSpec((tokens * rpt, LANES), lambda i, e: (i, 0))],
            out_specs=pl.BlockSpec(memory_space=pl.ANY),
            scratch_shapes=[pltpu.VMEM((MOE_TILE * rpt, LANES), xt.dtype),
                            pltpu.SemaphoreType.DMA(()), pltpu.SemaphoreType.DMA(())]),
        out_shape=jax.ShapeDtypeStruct((n_slots * rpt, LANES), xt.dtype),
        compiler_params=pltpu.CompilerParams(dimension_semantics=("arbitrary",)),
        name="moe_dispatch",
    )(ends, slots, xt)


def _moe_ffn_kernel(te_ref, na_ref, x_ref, wg_ref, wu_ref, wd_ref, o_ref, *, tm, rpt):
    active = pl.program_id(0) < na_ref[0]

    @pl.when(active)
    def _():
        xb = jnp.concatenate([x_ref[pl.ds(k, tm, stride=rpt), :] for k in range(rpt)], axis=1).astype(BF16)
        f = _swiglu(xb, wg_ref, wu_ref, wd_ref)
        for k in range(rpt):
            o_ref[pl.ds(k, tm, stride=rpt), :] = f[:, k * LANES:(k + 1) * LANES]

    @pl.when(jnp.logical_not(active))
    def _():
        o_ref[...] = jnp.zeros(o_ref.shape, o_ref.dtype)


def _moe_ffn(xs, wg, wu, wd, tile_expert, n_active, rpt):
    tm = MOE_TILE
    d, ff = wg.shape[1], wg.shape[2]
    n_tiles = xs.shape[0] // (tm * rpt)
    w_idx = lambda i, te, na: (te[i], 0, 0)
    vmem = 2 * 3 * d * ff * 2 + 4 * tm * d * 4 + tm * ff * 12 + 2 * tm * d * 4
    return pl.pallas_call(
        functools.partial(_moe_ffn_kernel, tm=tm, rpt=rpt),
        grid_spec=pltpu.PrefetchScalarGridSpec(
            num_scalar_prefetch=2, grid=(n_tiles,),
            in_specs=[pl.BlockSpec((tm * rpt, LANES), lambda i, te, na: (jnp.minimum(i, na[0] - 1), 0)),
                      pl.BlockSpec((1, d, ff), w_idx), pl.BlockSpec((1, d, ff), w_idx),
                      pl.BlockSpec((1, ff, d), w_idx)],
            out_specs=pl.BlockSpec((tm * rpt, LANES), lambda i, te, na: (i, 0))),
        out_shape=jax.ShapeDtypeStruct(xs.shape, F32),
        compiler_params=pltpu.CompilerParams(dimension_semantics=("arbitrary",),
                                             vmem_limit_bytes=_vmem_limit(vmem)),
        name="moe_ffn",
    )(tile_expert, n_active, xs, wg, wu, wd)


def _combine_ln_kernel(slot_ref, next_ref, x_ref, gate_ref, y_hbm, g_ref, b_ref, *rest, alpha, tm, rpt, offs, tiles):
    out_refs, (buf, sem) = rest[:len(tiles)], rest[len(tiles):]
    i = pl.program_id(0)
    n_rows = TOP_K * tm * rpt

    def issue(s_ref, slot):
        def body(r, carry):
            src = y_hbm.at[pl.ds(pl.multiple_of(s_ref[0, 0, r] * rpt, rpt), rpt)]
            pltpu.make_async_copy(src, buf.at[slot, pl.ds(pl.multiple_of(r * rpt, rpt), rpt)], sem.at[slot]).start()
            return carry
        lax.fori_loop(0, TOP_K * tm, body, 0)

    @pl.when(i == 0)
    def _():
        issue(slot_ref, 0)

    @pl.when(i + 1 < pl.num_programs(0))
    def _():
        issue(next_ref, (i + 1) % 2)

    cur = i % 2
    pltpu.make_async_copy(y_hbm.at[pl.ds(0, n_rows)], buf.at[cur], sem.at[cur]).wait()
    g1 = gate_ref[:, 0:1]
    g2 = gate_ref[:, 1:2]
    cols = []
    for k in range(rpt):
        xk = x_ref[pl.ds(k, tm, stride=rpt), :]
        ya = buf[cur, pl.ds(k, tm, stride=rpt), :]
        yb = buf[cur, pl.ds(tm * rpt + k, tm, stride=rpt), :]
        cols.append(alpha * xk + (g1 * ya + g2 * yb))
    h = jnp.concatenate(cols, axis=1)
    _seg_write(out_refs, offs, tiles, _layer_norm(h, g_ref[...], b_ref[...]))


def _combine_ln(xt, gate, ys, slots, g, b, alpha, tm, rpt, out_rows):
    d = g.shape[1]
    tiles = [n // tm for n in out_rows]
    offs = _seg_offsets(tiles)
    steps = slots.shape[0]
    slot_spec = lambda f: pl.BlockSpec((1, 1, TOP_K * tm), f, memory_space=pltpu.SMEM)
    res = pl.pallas_call(
        functools.partial(_combine_ln_kernel, alpha=alpha, tm=tm, rpt=rpt, offs=offs, tiles=tiles),
        grid=(steps,),
        in_specs=[slot_spec(lambda i: (i, 0, 0)), slot_spec(lambda i: (jnp.minimum(i + 1, steps - 1), 0, 0)),
                  pl.BlockSpec((tm * rpt, LANES), lambda i: (i, 0)), pl.BlockSpec((tm, LANES), lambda i: (i, 0)),
                  pl.BlockSpec(memory_space=pl.ANY), _const_spec(g.shape), _const_spec(b.shape)],
        out_specs=[pl.BlockSpec((tm, d), lambda i, o=o, n=n: (jnp.clip(i - o, 0, n - 1), 0))
                   for o, n in zip(offs, tiles)],
        out_shape=[jax.ShapeDtypeStruct((n, d), F32) for n in out_rows],
        scratch_shapes=[pltpu.VMEM((2, TOP_K * tm * rpt, LANES), F32), pltpu.SemaphoreType.DMA((2,))],
        compiler_params=pltpu.CompilerParams(dimension_semantics=("arbitrary",),
                                             vmem_limit_bytes=_vmem_limit(32 * 1024 * 1024)),
        name="moe_combine_ln",
    )(slots, slots, xt, gate, ys, g, b)
    return list(res)


def _moe_ffn_ln(xt, gate, plan, counts, wg, wu, wd, g, b, alpha, tm, out_rows):
    n_experts = wg.shape[0]
    d = wg.shape[1]
    rpt = d // LANES
    t = gate.shape[0]
    tile = MOE_TILE
    cnt = counts[0, :n_experts].astype(jnp.int32)
    padded = ((cnt + tile - 1) // tile) * tile
    ends = jnp.cumsum(padded).astype(jnp.int32)
    starts = ends - padded
    plan = plan.astype(jnp.int32)
    slot = plan[TOP_K:2 * TOP_K]
    for e in range(n_experts):
        slot = slot + jnp.where(plan[:TOP_K] == e, starts[e], 0)
    n_slots = -(-(TOP_K * t + n_experts * (tile - 1)) // tile) * tile
    n_tiles = n_slots // tile
    n_active = ends[-1:] // tile
    tile_start = jnp.minimum(jnp.arange(n_tiles, dtype=jnp.int32), n_active[0] - 1) * tile
    tile_expert = jnp.sum((tile_start[:, None] >= ends[None, :]).astype(jnp.int32), axis=1)
    slots = jnp.swapaxes(slot.reshape(TOP_K, t // tm, tm), 0, 1).reshape(t // tm, 1, TOP_K * tm)
    xs = _dispatch(xt, slots, ends, n_slots, tm, rpt, n_experts)
    ys = _moe_ffn(xs, wg, wu, wd, tile_expert, n_active, rpt)
    return _combine_ln(xt, gate, ys, slots, g, b, alpha, tm, rpt, out_rows)


def kernel(x_prompt, x_sample, state_pool, state_conv, state_ssm, w_in, conv_w, conv_b, dt_bias, A_log, D_skip,
           ssm_norm_w, pool_w, pool_scale, w_out, ln1_g, ln1_b, ln2_g, ln2_b, ffn_w_gate, ffn_w_up, ffn_w_down,
           router_w, router_b, moe_w_gate, moe_w_up, moe_w_down):
    bp, seq, d = x_prompt.shape
    bs, n_new, _ = x_sample.shape
    depth = w_in.shape[0]
    d_pool = pool_scale.shape[1]
    d_ssm = ssm_norm_w.shape[1]
    d_xbc = conv_w.shape[2]
    n_heads = dt_bias.shape[1]
    tp, ts = bp * seq, bs * n_new
    alpha = (2.0 * depth) ** 0.25
    assert d_pool == POOL_GROUP * len(POOL_WINDOWS) and d_ssm == n_heads * HEAD_DIM
    assert d_xbc == d_ssm + 2 * N_GROUPS * D_STATE and seq % CHUNK == 0 and n_new <= CONV_W
    assert n_heads <= LANES and bs % (CHUNK // SAMPLE_ROWS) == 0 and d % LANES == 0

    consts_prompt = _ssd_consts(CHUNK, n_heads)
    consts_sample = _ssd_consts(SAMPLE_ROWS, n_heads)
    x_segs = [x_prompt.reshape(tp, d), x_sample.reshape(ts, d)]
    new_pool_p, new_conv_p, new_ssm_p, new_pool_s, new_conv_s = [], [], [], [], []
    ssm_s = None
    next_w_in = None
    for l in range(depth):
        out_rows = [tp, ts]
        n_in = w_in.shape[2]
        w_in_bf16 = w_in[l].astype(BF16) if next_w_in is None else next_w_in
        w_in_pad = jnp.pad(w_in_bf16, ((0, 0), (0, d_pool + d_ssm + d_xbc + LANES - n_in)))
        lp = dict(
            conv_w=conv_w[l], conv_b=conv_b[l][None, :],
            dt_bias=jnp.zeros((1, LANES), F32).at[0, :n_heads].set(dt_bias[l]),
            a_log=jnp.zeros((1, LANES), F32).at[0, :n_heads].set(A_log[l]),
            d_exp=jnp.repeat(D_skip[l], HEAD_DIM)[None, :], norm_w=ssm_norm_w[l][None, :],
            ssd_consts_prompt=consts_prompt, ssd_consts_sample=consts_sample)
        pw = pool_w[l].astype(BF16)
        ps = pool_scale[l][None, :]

        proj = (w_in_pad, lp["conv_w"], lp["conv_b"], lp["dt_bias"], d_pool, d_ssm, d_xbc)

        dense = l % 2 == 0
        side = (((ffn_w_gate, l // 2), (ffn_w_up, l // 2), (ffn_w_down, l // 2)) if dense else ()) \
            + (((w_in, l + 1),) if l + 1 < depth else ())
        (u_p, sz_p, xact_p, dt_p, c_p), cast = _in_proj(x_segs[0], *proj, seq_len=seq, side_casts=side)
        ffn_bf16 = [c[None] for c in cast[:3]] if dense else None
        next_w_in = cast[-1] if l + 1 < depth else None
        pool_p, buf_p = _pool_prompt(u_p, tp, seq, pw, ps)
        y_p, h_p = _ssd_prompt(sz_p, xact_p, dt_p, bp, seq, lp)
        new_pool_p.append(buf_p)
        new_conv_p.append(c_p)
        new_ssm_p.append(jnp.concatenate(h_p, axis=0).reshape(bp, n_heads, HEAD_DIM, D_STATE))

        u_s, sz_s, xbc_s, dt_s = _in_proj_sample(x_segs[1], w_in_pad, lp["dt_bias"], d_pool, d_ssm, d_xbc)
        u_s = u_s.reshape(bs, n_new, d_pool)
        xbc_s = xbc_s.reshape(bs, n_new, d_xbc)
        pool_ext = jnp.concatenate([state_pool[l], u_s], axis=1)
        pool_s = _pool_sample(jnp.swapaxes(pool_ext, 0, 1), pw, ps, n_new, PAST_LEN)
        pool_s = jnp.swapaxes(pool_s, 0, 1).reshape(ts, d_pool)
        conv_ext = jnp.concatenate([state_conv[l], xbc_s], axis=1)
        lead = CONV_LEAD - (CONV_W - 1)
        ext = jnp.pad(conv_ext, ((0, 0), (lead, SAMPLE_EXT_ROWS - lead - conv_ext.shape[1]), (0, 0)))
        pad_rows = lambda a: jnp.pad(a.reshape(bs, n_new, -1), ((0, 0), (0, SAMPLE_ROWS - n_new), (0, 0))
                                     ).reshape(bs * SAMPLE_ROWS, -1)
        y_s16, ssm_s = _ssd_sample(pad_rows(sz_s), ext.reshape(bs * SAMPLE_EXT_ROWS, d_xbc), pad_rows(dt_s),
                                   state_ssm.reshape(depth, bs, d_ssm, D_STATE), ssm_s, l, lp, n_new)
        y_s = y_s16.reshape(bs, SAMPLE_ROWS, d_ssm)[:, :n_new].reshape(ts, d_ssm)
        new_pool_s.append(pool_ext[:, n_new:])
        new_conv_s.append(conv_ext[:, n_new:])

        w_o = w_out[l].astype(BF16)
        ln1 = (ln1_g[l][None, :], ln1_b[l][None, :])
        g2, b2 = ln2_g[l][None, :], ln2_b[l][None, :]
        j = l // 2
        if l % 2 == 0:
            x1 = _outproj_ln(x_segs, [pool_p, pool_s], [*y_p, y_s], w_o[:d_pool], w_o[d_pool:], *ln1, alpha)
            side = ((moe_w_gate, j), (moe_w_up, j), (moe_w_down, j)) if l + 1 < depth else ()
            x_segs, moe_bf16 = _dense_ffn_ln(x1, *ffn_bf16, g2, b2, alpha, out_rows, side)
        else:
            n_experts = router_w.shape[2]
            rw_pad = jnp.zeros((d, LANES), F32).at[:, :n_experts].set(router_w[j])
            rb_pad = jnp.zeros((1, LANES), F32).at[0, :n_experts].set(router_b[j])
            xt, gate, plan, counts = _outproj_ln(x_segs, [pool_p, pool_s], [*y_p, y_s], w_o[:d_pool], w_o[d_pool:],
                                                 *ln1, alpha, router=(rw_pad, rb_pad, n_experts))
            tm = _common_tile([tp, ts], (512, 256, 128, 64))
            x_segs = _moe_ffn_ln(xt, gate, plan, counts, *moe_bf16, g2, b2, alpha, tm, out_rows)

    y_prompt, y_sample = x_segs
    return (y_prompt.reshape(bp, seq, d), y_sample.reshape(bs, n_new, d),
            jnp.stack(new_pool_p), jnp.stack(new_conv_p), jnp.stack(new_ssm_p),
            jnp.stack(new_pool_s), jnp.stack(new_conv_s),
            ssm_s.reshape(depth, bs, n_heads, HEAD_DIM, D_STATE))
```

```python
import functools

import jax
import jax.numpy as jnp
from jax import lax
from jax.experimental import pallas as pl
from jax.experimental.pallas import tpu as pltpu

F32 = jnp.float32
BF16 = jnp.bfloat16

PAST_LEN = 16384
POOL_WINDOWS = (2, 4, 8, 16)
POOL_GROUP = 128
POOL_BUF = max(POOL_WINDOWS) - 1
HEAD_DIM = 64
N_GROUPS = 4
D_STATE = 128
CONV_W = 4
CHUNK = 128
TOP_K = 2
LN_EPS = 1e-5
RMS_EPS = 1e-6
LOG2_E = 1.4426950408889634

LANES = 128
SUBLANES = 8
V7X_VMEM_BYTES = 64 * 1024 * 1024
VMEM_CAP = V7X_VMEM_BYTES - 8 * 1024 * 1024

SAMPLE_ROWS = 16
SAMPLE_EXT_ROWS = 24
CONV_LEAD = 8
MOE_TILE = 256
SSD_WAYS = 2


def _vmem_limit(nbytes):
    return int(min(VMEM_CAP, nbytes + 6 * 1024 * 1024))


def _dot(a, b):
    return jnp.dot(a, b, preferred_element_type=F32)


def _dot_nt(a, b):
    return lax.dot_general(a, b, (((1,), (1,)), ((), ())), preferred_element_type=F32)


def _split3(v):
    hi = v.astype(BF16)
    r = v - hi.astype(F32)
    mid = r.astype(BF16)
    lo = (r - mid.astype(F32)).astype(BF16)
    return hi, mid, lo


def _sel_right(parts, m):
    out = _dot(parts[0], m)
    for p in parts[1:]:
        out = out + _dot(p, m)
    return out


def _sel_left(m, parts):
    out = _dot(m, parts[0])
    for p in parts[1:]:
        out = out + _dot(m, p)
    return out


def _silu(x):
    return x / (1.0 + jnp.exp2(x * (-LOG2_E)))


def _layer_norm(h, g, b):
    mu = jnp.mean(h, axis=-1, keepdims=True)
    d = h - mu
    var = jnp.mean(d * d, axis=-1, keepdims=True)
    return d * lax.rsqrt(var + LN_EPS) * g + b


def _pick_tile(n, candidates):
    for c in candidates:
        if n % c == 0:
            return c
    raise ValueError(f"no tile in {candidates} divides {n}")


def _const_spec(shape, single=False):
    nd = len(shape)
    if single:
        return pl.BlockSpec(shape, lambda *_: (0,) * nd, pipeline_mode=pl.Buffered(1))
    return pl.BlockSpec(shape, lambda *_: (0,) * nd)


def _seg_tiles(segs, tm, rows_per_token=1):
    return [a.shape[0] // (tm * rows_per_token) for a in segs]


def _seg_offsets(tiles):
    offs, off = [], 0
    for n in tiles:
        offs.append(off)
        off += n
    return offs


def _seg_specs(segs, tm, rows_per_token=1):
    tiles = _seg_tiles(segs, tm, rows_per_token)
    return [pl.BlockSpec((tm * rows_per_token, a.shape[1]),
                         lambda i, *_, o=o, n=n: (jnp.clip(i - o, 0, n - 1), 0))
            for a, o, n in zip(segs, _seg_offsets(tiles), tiles)]


def _seg_read(refs, offs):
    v = refs[0][...]
    for r, o in zip(refs[1:], offs[1:]):
        v = jnp.where(pl.program_id(0) >= o, r[...], v)
    return v


def _seg_write(refs, offs, tiles, v):
    i = pl.program_id(0)
    for r, o, n in zip(refs, offs, tiles):
        @pl.when(jnp.logical_and(i >= o, i < o + n))
        def _(r=r):
            r[...] = v.astype(r.dtype)


def _common_tile(row_counts, candidates):
    for c in candidates:
        if all(n % c == 0 for n in row_counts):
            return c
    raise ValueError(f"no tile in {candidates} divides all of {row_counts}")


INPROJ_CHUNK = 512


def _row_blocks(rows, steps, align):
    for n in range(min(steps, rows), 0, -1):
        if rows % n == 0 and (rows // n) % align == 0:
            return n
    raise ValueError(f"cannot split {rows} rows into at most {steps} blocks of a multiple of {align} rows")


def _side_cast_plan(entries, steps):
    side2d, in_specs, out_specs, out_shapes = [], [], [], []
    for stacked, layer in entries:
        rows = 1
        for n in stacked.shape[1:-1]:
            rows *= n
        cols = stacked.shape[-1]
        nb = _row_blocks(rows, steps, 2 * SUBLANES)
        side2d.append(stacked.reshape(stacked.shape[0], rows, cols))
        in_specs.append(pl.BlockSpec((None, rows // nb, cols),
                                     lambda i, nb=nb, layer=layer: (layer, jnp.minimum(i, nb - 1), 0)))
        out_specs.append(pl.BlockSpec((rows // nb, cols), lambda i, nb=nb: (jnp.minimum(i, nb - 1), 0)))
        out_shapes.append(jax.ShapeDtypeStruct((rows, cols), BF16))
    vmem = sum(2 * (4 + 2) * s.block_shape[0] * s.block_shape[1] for s in out_specs)
    return side2d, in_specs, out_specs, out_shapes, vmem


def _side_cast(side_in, side_out):
    for src, dst in zip(side_in, side_out):
        dst[...] = src[...].astype(dst.dtype)


def _inproj_kernel(*refs, splits, tiles_per_seq, n_side):
    x_ref, w_ref, convw_ref, convb_ref, dtb_ref = refs[:5]
    side_in = refs[5:5 + n_side]
    u_ref, sz_ref, xo_ref, dt_ref, cstate_ref = refs[5 + n_side:10 + n_side]
    side_out, ext_ref = refs[10 + n_side:10 + 2 * n_side], refs[10 + 2 * n_side]
    (u0, u1), (z0, z1), (c0, c1), (d0, d1) = splits
    tm = x_ref.shape[0]
    xb = x_ref[...].astype(BF16)
    u_ref[...] = _dot(xb, w_ref[:, u0:u1])
    dt_ref[...] = _softplus(_dot(xb, w_ref[:, d0:d1]) + dtb_ref[...])
    for lo in range(z0, z1, INPROJ_CHUNK):
        sz_ref[:, lo - z0:lo - z0 + INPROJ_CHUNK] = _silu(_dot(xb, w_ref[:, lo:lo + INPROJ_CHUNK]))
    _side_cast(side_in, side_out)

    pos = pl.program_id(0) % tiles_per_seq

    @pl.when(pos == 0)
    def _():
        ext_ref[0:CONV_LEAD, :] = jnp.zeros((CONV_LEAD, ext_ref.shape[1]), F32)

    @pl.when(pos > 0)
    def _():
        ext_ref[0:CONV_LEAD, :] = ext_ref[tm:tm + CONV_LEAD, :]

    first = CONV_LEAD - (CONV_W - 1)
    for lo in range(c0, c1, INPROJ_CHUNK):
        cols = slice(lo - c0, lo - c0 + INPROJ_CHUNK)
        ext_ref[CONV_LEAD:CONV_LEAD + tm, cols] = _dot(xb, w_ref[:, lo:lo + INPROJ_CHUNK])
        rows = ext_ref[:, cols]
        acc = convb_ref[:, cols] + (pltpu.roll(rows, CONV_W - 1, axis=0)[CONV_LEAD:] * convw_ref[0:1, cols])
        for k in range(1, CONV_W - 1):
            acc = acc + pltpu.roll(rows, CONV_W - 1 - k, axis=0)[CONV_LEAD:] * convw_ref[k:k + 1, cols]
        acc = acc + rows[CONV_LEAD:] * convw_ref[CONV_W - 1:CONV_W, cols]
        xo_ref[:, cols] = _silu(acc)

    @pl.when(pos == tiles_per_seq - 1)
    def _():
        cstate_ref[0] = ext_ref[CONV_LEAD + tm - (CONV_W - 1):CONV_LEAD + tm, :]


def _inproj_splits(w_pad, d_pool, d_ssm, d_xbc):
    assert d_ssm % INPROJ_CHUNK == 0 and d_xbc % INPROJ_CHUNK == 0
    return ((0, d_pool), (d_pool, d_pool + d_ssm), (d_pool + d_ssm, d_pool + d_ssm + d_xbc),
            (d_pool + d_ssm + d_xbc, w_pad.shape[1]))


def _in_proj(x, w_pad, conv_w, conv_b, dt_bias, d_pool, d_ssm, d_xbc, seq_len, side_casts=()):
    t, d = x.shape
    n = w_pad.shape[1]
    tm = _pick_tile(seq_len, (512, 256, 128, 64))
    splits = _inproj_splits(w_pad, d_pool, d_ssm, d_xbc)
    widths = [hi - lo for lo, hi in splits]
    tiles_per_seq = seq_len // tm
    side2d, side_in, side_out, side_shapes, side_bytes = _side_cast_plan(side_casts, t // tm)
    vmem = (2 * tm * d * 4 + d * n * 2 + 2 * tm * n * 4 + 2 * tm * d_xbc * 4 + 4 * tm * INPROJ_CHUNK * 4
            + side_bytes)
    res = pl.pallas_call(
        functools.partial(_inproj_kernel, splits=splits, tiles_per_seq=tiles_per_seq, n_side=len(side2d)),
        grid=(t // tm,),
        in_specs=[pl.BlockSpec((tm, d), lambda i: (i, 0)), _const_spec((d, n), single=True),
                  _const_spec(conv_w.shape), _const_spec(conv_b.shape), _const_spec(dt_bias.shape)] + side_in,
        out_specs=[pl.BlockSpec((tm, w), lambda i: (i, 0)) for w in widths]
                  + [pl.BlockSpec((1, CONV_W - 1, d_xbc), lambda i: (i // tiles_per_seq, 0, 0))] + side_out,
        out_shape=[jax.ShapeDtypeStruct((t, w), F32) for w in widths]
                  + [jax.ShapeDtypeStruct((t // seq_len, CONV_W - 1, d_xbc), F32)] + side_shapes,
        scratch_shapes=[pltpu.VMEM((tm + CONV_LEAD, d_xbc), F32)],
        compiler_params=pltpu.CompilerParams(dimension_semantics=("arbitrary",),
                                             vmem_limit_bytes=_vmem_limit(vmem)),
        name="in_proj",
    )(x, w_pad, conv_w, conv_b, dt_bias, *side2d)
    return list(res[:5]), [r.reshape(a.shape[1:]) for r, (a, _) in zip(res[5:], side_casts)]


def _inproj_sample_kernel(x_ref, w_ref, dtb_ref, u_ref, sz_ref, xbc_ref, dt_ref, *, splits):
    (u0, u1), (z0, z1), (c0, c1), (d0, d1) = splits
    xb = x_ref[...].astype(BF16)
    u_ref[...] = _dot(xb, w_ref[:, u0:u1])
    sz_ref[...] = _silu(_dot(xb, w_ref[:, z0:z1]))
    xbc_ref[...] = _dot(xb, w_ref[:, c0:c1])
    dt_ref[...] = _softplus(_dot(xb, w_ref[:, d0:d1]) + dtb_ref[...])


def _in_proj_sample(x, w_pad, dt_bias, d_pool, d_ssm, d_xbc):
    t, d = x.shape
    n = w_pad.shape[1]
    tm = _pick_tile(t, (512, 256, 128, 64))
    splits = _inproj_splits(w_pad, d_pool, d_ssm, d_xbc)
    widths = [hi - lo for lo, hi in splits]
    vmem = 2 * tm * d * 4 + 2 * d * n * 2 + 4 * tm * n * 4
    return pl.pallas_call(
        functools.partial(_inproj_sample_kernel, splits=splits),
        grid=(t // tm,),
        in_specs=[pl.BlockSpec((tm, d), lambda i: (i, 0)), _const_spec((d, n)), _const_spec(dt_bias.shape)],
        out_specs=[pl.BlockSpec((tm, w), lambda i: (i, 0)) for w in widths],
        out_shape=[jax.ShapeDtypeStruct((t, w), F32) for w in widths],
        compiler_params=pltpu.CompilerParams(dimension_semantics=("arbitrary",),
                                             vmem_limit_bytes=_vmem_limit(vmem)),
        name="in_proj_sample",
    )(x, w_pad, dt_bias)


def _pool_prompt_kernel(u_ref, w_ref, scale_ref, o_ref, buf_ref, ext_ref, *, tl):
    j = pl.program_id(1)

    @pl.when(j == pl.num_programs(1) - 1)
    def _():
        buf_ref[0] = u_ref[tl - POOL_BUF:tl, :]

    @pl.when(j == 0)
    def _():
        ext_ref[0:16, :] = jnp.zeros((16, ext_ref.shape[1]), F32)

    @pl.when(j > 0)
    def _():
        ext_ref[0:16, :] = ext_ref[tl:tl + 16, :]

    ext_ref[16:16 + tl, :] = u_ref[...]
    pos = j * tl + lax.broadcasted_iota(jnp.int32, (tl, POOL_GROUP), 0)
    for g, win in enumerate(POOL_WINDOWS):
        lanes = slice(g * POOL_GROUP, (g + 1) * POOL_GROUP)
        rows = ext_ref[:, lanes]
        cur = rows[16:]
        span = 1
        while span < win:
            rows = rows + pltpu.roll(rows, span, axis=0)
            span *= 2
        acc = rows[16:]
        cnt = jnp.minimum(pos + 1, win).astype(F32)
        diff = (acc / cnt - cur).astype(BF16)
        out = _dot(diff, w_ref[g]) * scale_ref[:, lanes]
        o_ref[:, lanes] = out.astype(o_ref.dtype)


def _pool_prompt(u, n_rows, seq_len, pool_w, pool_scale):
    d_pool = u.shape[1]
    tl = _pick_tile(seq_len, (512, 256, 128))
    nj = seq_len // tl
    nb = n_rows // seq_len
    return pl.pallas_call(
        functools.partial(_pool_prompt_kernel, tl=tl),
        grid=(nb, nj),
        in_specs=[pl.BlockSpec((tl, d_pool), lambda b, j: (b * nj + j, 0)),
                  _const_spec(pool_w.shape), _const_spec(pool_scale.shape)],
        out_specs=[pl.BlockSpec((tl, d_pool), lambda b, j: (b * nj + j, 0)),
                   pl.BlockSpec((1, POOL_BUF, d_pool), lambda b, j: (b, 0, 0))],
        out_shape=[jax.ShapeDtypeStruct((n_rows, d_pool), BF16),
                   jax.ShapeDtypeStruct((nb, POOL_BUF, d_pool), F32)],
        scratch_shapes=[pltpu.VMEM((tl + 16, d_pool), F32)],
        compiler_params=pltpu.CompilerParams(dimension_semantics=("arbitrary", "arbitrary")),
        name="pool_prompt",
    )(u, pool_w, pool_scale)


def _pool_sample_kernel(ext_ref, w_ref, scale_ref, o_ref, *, n_new, start):
    for t in range(n_new):
        for g, win in enumerate(POOL_WINDOWS):
            lanes = slice(g * POOL_GROUP, (g + 1) * POOL_GROUP)
            cur = ext_ref[POOL_BUF + t, :, lanes]
            acc = cur
            for k in range(1, win):
                acc = acc + ext_ref[POOL_BUF + t - k, :, lanes]
            cnt = float(min(start + t + 1, win))
            diff = (acc / cnt - cur).astype(BF16)
            out = _dot(diff, w_ref[g]) * scale_ref[:, lanes]
            o_ref[t, :, lanes] = out.astype(o_ref.dtype)


def _pool_sample(ext_t, pool_w, pool_scale, n_new, start):
    rows, bs, d_pool = ext_t.shape
    return pl.pallas_call(
        functools.partial(_pool_sample_kernel, n_new=n_new, start=start),
        grid=(1,),
        in_specs=[_const_spec(ext_t.shape), _const_spec(pool_w.shape), _const_spec(pool_scale.shape)],
        out_specs=_const_spec((n_new, bs, d_pool)),
        out_shape=jax.ShapeDtypeStruct((n_new, bs, d_pool), BF16),
        compiler_params=pltpu.CompilerParams(dimension_semantics=("arbitrary",)),
        name="pool_sample",
    )(ext_t, pool_w, pool_scale)


def _conv_silu(window, w_ref, b_ref):
    acc = b_ref[...] + window(0) * w_ref[0:1, :]
    for k in range(1, CONV_W):
        acc = acc + window(k) * w_ref[k:k + 1, :]
    return _silu(acc)


def _softplus(x):
    return jnp.maximum(x, 0.0) + jnp.log1p(jnp.exp(-jnp.abs(x)))


def _ssd_chunk(xact, sz, dt, prm, seq_rows, read_state, write_state, write_out):
    (cums_ref, spread_ref, alog_ref, dexp_ref, normw_ref) = prm
    q = xact.shape[0]
    d_ssm = sz.shape[1]
    gw = d_ssm // N_GROUPS
    n_seq = q // seq_rows
    xs = xact[:, :d_ssm]
    bm = xact[:, d_ssm:d_ssm + N_GROUPS * D_STATE]
    cm = xact[:, d_ssm + N_GROUPS * D_STATE:]

    a = dt * (-jnp.exp(alog_ref[...]) * LOG2_E)
    sums = _dot(cums_ref[...], jnp.concatenate(_split3(a), axis=0))
    cum, rcum = sums[:q], sums[q:]
    w = jnp.exp2(rcum) * dt
    cum_t = cum.T
    dt_t = dt.T
    two_terms = lambda v: jnp.concatenate(_split3(v)[:2], axis=1)
    spread = _dot(jnp.concatenate([two_terms(w), two_terms(jnp.exp2(cum))], axis=0), spread_ref[...])
    wx = xs * spread[:q]
    ecum_e = spread[q:]

    ii = lax.broadcasted_iota(jnp.int32, (q, q), 0)
    jj = lax.broadcasted_iota(jnp.int32, (q, q), 1)
    mask = ii >= jj
    if n_seq > 1:
        mask = jnp.logical_and(mask, (ii // seq_rows) == (jj // seq_rows))
    lo_half = lax.broadcasted_iota(jnp.int32, (q, LANES), 1) < HEAD_DIM
    col = lax.broadcasted_iota(jnp.int32, (gw, q), 1)

    for g in range(N_GROUPS):
        gcols = slice(g * gw, (g + 1) * gw)
        bg = bm[:, g * D_STATE:(g + 1) * D_STATE].astype(BF16)
        cg = cm[:, g * D_STATE:(g + 1) * D_STATE].astype(BF16)
        cb = _dot_nt(cg, bg)
        ydiag = []
        end_decay = {}
        for pr in range(gw // LANES):
            blk = g * (gw // LANES) + pr
            xp = xs[:, blk * LANES:(blk + 1) * LANES]
            mix = []
            for half in range(2):
                h = 2 * blk + half
                colb = jnp.broadcast_to(cum[:, h:h + 1], (q, q))
                for s in range(n_seq):
                    last = (s + 1) * seq_rows - 1
                    end_decay[h, s] = jnp.exp2(colb[last:last + 1, :])
                dec = jnp.exp2(jnp.where(mask, colb - cum_t[h:h + 1, :], -jnp.inf))
                mix.append((cb * dec * dt_t[h:h + 1, :]).astype(BF16))
            x2 = jnp.concatenate([jnp.where(lo_half, xp, 0.0), jnp.where(lo_half, 0.0, xp)], axis=0).astype(BF16)
            ydiag.append(_dot(jnp.concatenate(mix, axis=1), x2))
        wx_t = wx[:, gcols].T
        yoff_rows = []
        for s in range(n_seq):
            r0 = s * seq_rows
            st = read_state(s, g)
            yoff_rows.append(_dot_nt(cg[r0:r0 + seq_rows, :], st.astype(BF16)))
            scale = jnp.concatenate(
                [jnp.broadcast_to(end_decay[h, s], (HEAD_DIM, D_STATE))
                 for h in range(g * (gw // HEAD_DIM), (g + 1) * (gw // HEAD_DIM))], axis=0)
            wsel = wx_t
            if n_seq > 1:
                wsel = jnp.where(jnp.logical_and(col >= r0, col < r0 + seq_rows), wx_t, 0.0)
            write_state(s, g, st * scale + _dot(wsel.astype(BF16), bg))
        yoff = yoff_rows[0] if n_seq == 1 else jnp.concatenate(yoff_rows, axis=0)
        y = jnp.concatenate(ydiag, axis=1) + yoff * ecum_e[:, gcols] + xs[:, gcols] * dexp_ref[:, gcols]
        gz = y * sz[:, gcols]
        ms = jnp.sum(gz * gz, axis=-1, keepdims=True) * (1.0 / gw)
        write_out(g, gz * lax.rsqrt(ms + RMS_EPS) * normw_ref[:, gcols])


def _ssd_prompt_kernel(*refs, gw, ways):
    ins, prm = refs[:3 * ways], refs[3 * ways:3 * ways + 5]
    outs, h_ref = refs[3 * ways + 5:5 * ways + 5], refs[5 * ways + 5]
    c = pl.program_id(1)

    @pl.when(c == 0)
    def _():
        h_ref[...] = jnp.zeros(h_ref.shape, F32)

    for k in range(ways):
        sz_ref, xact_ref, dt_ref = ins[3 * k:3 * k + 3]
        y_ref = outs[2 * k]

        def read_state(s, g, k=k):
            return h_ref[k, g * gw:(g + 1) * gw, :]

        def write_state(s, g, v, k=k):
            h_ref[k, g * gw:(g + 1) * gw, :] = v

        def write_out(g, v, y_ref=y_ref):
            y_ref[:, g * gw:(g + 1) * gw] = v.astype(y_ref.dtype)

        _ssd_chunk(xact_ref[...], sz_ref[...], dt_ref[...], prm, sz_ref.shape[0], read_state, write_state, write_out)

    @pl.when(c == pl.num_programs(1) - 1)
    def _():
        for k in range(ways):
            outs[2 * k + 1][0] = h_ref[k]


def _ssd_sample_kernel(sz_ref, ext_ref, dt_ref, hin_ref, convw_ref, convb_ref,
                       cums_ref, spread_ref, alog_ref, dexp_ref, normw_ref, *rest, gw, n_new):
    y_ref, hout_ref = rest[-2:]
    q = sz_ref.shape[0]
    n_seq = q // SAMPLE_ROWS
    first = CONV_LEAD - (CONV_W - 1)

    def window(k):
        return jnp.concatenate(
            [ext_ref[s * SAMPLE_EXT_ROWS + first + k:s * SAMPLE_EXT_ROWS + first + k + SAMPLE_ROWS, :]
             for s in range(n_seq)], axis=0)

    xact = _conv_silu(window, convw_ref, convb_ref)
    row = lax.broadcasted_iota(jnp.int32, (q, LANES), 0)
    dt = jnp.where((row % SAMPLE_ROWS) < n_new, dt_ref[...], 0.0)

    def read_state(s, g):
        return hin_ref[s, g * gw:(g + 1) * gw, :]

    def write_state(s, g, v):
        hout_ref[s, g * gw:(g + 1) * gw, :] = v

    prm = (cums_ref, spread_ref, alog_ref, dexp_ref, normw_ref)

    def write_out(g, v):
        y_ref[:, g * gw:(g + 1) * gw] = v.astype(y_ref.dtype)

    _ssd_chunk(xact, sz_ref[...], dt, prm, SAMPLE_ROWS, read_state, write_state, write_out)


def _ssd_consts(seq_rows, n_heads):
    q = CHUNK
    i = jnp.arange(q)[:, None]
    j = jnp.arange(q)[None, :]
    same = (i // seq_rows) == (j // seq_rows)
    tri = jnp.logical_and(same, j <= i).astype(BF16)
    tris = jnp.logical_and(same, j > i).astype(BF16)
    cums = jnp.tile(jnp.concatenate([tri, tris], axis=0), (1, 3))
    hrow = jnp.arange(LANES)[:, None]
    sel64 = (hrow == (jnp.arange(n_heads * HEAD_DIM)[None, :] // HEAD_DIM)).astype(BF16)
    spread = jnp.tile(sel64, (2, 1))
    return cums, spread


def _ssd_param_specs(prm_arrays):
    return [_const_spec(a.shape) for a in prm_arrays]


def _ssd_prompt(sz, xact, dt, n_seq, seq_len, lp):
    d_ssm = sz.shape[1]
    d_xbc = xact.shape[1]
    q = CHUNK
    nc = seq_len // q
    gw = d_ssm // N_GROUPS
    prm = (*lp["ssd_consts_prompt"], lp["a_log"], lp["d_exp"], lp["norm_w"])
    ways = SSD_WAYS if n_seq % SSD_WAYS == 0 else 1
    per = n_seq // ways
    in_specs, out_specs, out_shape = [], [], []
    for k in range(ways):
        rows = lambda b, c, k=k: ((k * per + b) * nc + c, 0)
        in_specs += [pl.BlockSpec((q, d_ssm), rows), pl.BlockSpec((q, d_xbc), rows), pl.BlockSpec((q, LANES), rows)]
        out_specs += [pl.BlockSpec((q, d_ssm), lambda b, c: (b * nc + c, 0)),
                      pl.BlockSpec((1, d_ssm, D_STATE), lambda b, c: (b, 0, 0))]
        out_shape += [jax.ShapeDtypeStruct((per * seq_len, d_ssm), BF16),
                      jax.ShapeDtypeStruct((per, d_ssm, D_STATE), F32)]
    res = pl.pallas_call(
        functools.partial(_ssd_prompt_kernel, gw=gw, ways=ways),
        grid=(per, nc),
        in_specs=in_specs + _ssd_param_specs(prm),
        out_specs=out_specs,
        out_shape=out_shape,
        scratch_shapes=[pltpu.VMEM((ways, d_ssm, D_STATE), F32)],
        compiler_params=pltpu.CompilerParams(dimension_semantics=("arbitrary", "arbitrary"),
                                             vmem_limit_bytes=_vmem_limit(48 * 1024 * 1024)),
        name="ssd_prompt",
    )(*([sz, xact, dt] * ways), *prm)
    return list(res[0::2]), list(res[1::2])


def _ssd_sample(z16, ext, dtr16, h_all, h_new, layer, lp, n_new):
    d_ssm = z16.shape[1]
    d_xbc = ext.shape[1]
    q = CHUNK
    spc = q // SAMPLE_ROWS
    bs = h_all.shape[1]
    gw = d_ssm // N_GROUPS
    prm = (lp["conv_w"], lp["conv_b"], *lp["ssd_consts_sample"], lp["a_log"], lp["d_exp"], lp["norm_w"])
    rows = lambda i: (i, 0)
    slab =pl.BlockSpec((None, spc, d_ssm, D_STATE), lambda i: (layer, i, 0, 0))
    state_bytes = spc * d_ssm * D_STATE * 4
    prev = [] if h_new is None else [h_new]
    n_in = 4 + len(prm)
    return pl.pallas_call(
        functools.partial(_ssd_sample_kernel, gw=gw, n_new=n_new),
        grid=(bs // spc,),
        in_specs=[pl.BlockSpec((q, d_ssm), rows), pl.BlockSpec((spc * SAMPLE_EXT_ROWS, d_xbc), rows),
                  pl.BlockSpec((q, LANES), rows), slab] + _ssd_param_specs(prm)
                 + [pl.BlockSpec(memory_space=pl.ANY) for _ in prev],
        out_specs=[pl.BlockSpec((q, d_ssm), rows), slab],
        out_shape=[jax.ShapeDtypeStruct((bs * SAMPLE_ROWS, d_ssm), BF16),
                   jax.ShapeDtypeStruct(h_all.shape, F32)],
        input_output_aliases={n_in: 1} if prev else {},
        compiler_params=pltpu.CompilerParams(dimension_semantics=("arbitrary",),
                                             vmem_limit_bytes=_vmem_limit(4 * state_bytes + 20 * 1024 * 1024)),
        name="ssd_sample",
    )(z16, ext, dtr16, h_all, *prm, *prev)


def _route(h, rw_ref, rb_ref, tril_ref, cnt_ref, n_experts):
    xh = h.astype(BF16)
    xl = (h - xh.astype(F32)).astype(BF16)
    w = rw_ref[...]
    wh = w.astype(BF16)
    wl = (w - wh.astype(F32)).astype(BF16)
    hi = _dot(xh, jnp.concatenate([wh, wl], axis=1))
    logits = hi[:, :LANES] + (hi[:, LANES:] + _dot(xl, wh)) + rb_ref[...]
    lane = lax.broadcasted_iota(jnp.int32, logits.shape, 1)
    lane_f = lane.astype(F32)
    logits = jnp.where(lane < n_experts, logits, -jnp.inf)
    m1 = jnp.max(logits, axis=-1, keepdims=True)
    i1 = jnp.min(jnp.where(logits == m1, lane_f, float(LANES)), axis=-1, keepdims=True)
    rest = jnp.where(lane_f == i1, -jnp.inf, logits)
    m2 = jnp.max(rest, axis=-1, keepdims=True)
    i2 = jnp.min(jnp.where(rest == m2, lane_f, float(LANES)), axis=-1, keepdims=True)
    e2 = jnp.exp(m2 - m1)
    den = 1.0 + e2
    oh1 = jnp.where(lane_f == i1, 1.0, 0.0)
    oh2 = jnp.where(lane_f == i2, 1.0, 0.0)
    before = _dot(tril_ref[...], jnp.concatenate([oh1, oh2], axis=1).astype(BF16))
    before1, before2 = before[:, :LANES], before[:, LANES:]
    c1 = jnp.sum(oh1, axis=0, keepdims=True)
    c2 = jnp.sum(oh2, axis=0, keepdims=True)
    base = cnt_ref[...]
    r1 = jnp.sum(oh1 * (before1 + base), axis=-1, keepdims=True)
    r2 = jnp.sum(oh2 * (before2 + (base + c1)), axis=-1, keepdims=True)
    cnt_ref[...] = base + (c1 + c2)
    gate = jnp.where(lane == 0, 1.0 / den, jnp.where(lane == 1, e2 / den, 0.0))
    plan = jnp.where(lane == 0, i1, jnp.where(lane == 1, i2, jnp.where(lane == 2, r1, jnp.where(lane == 3, r2, 0.0))))
    return gate, plan


def _outproj_ln_kernel(*refs, alpha, n_seg, offs, route, n_experts):
    it = iter(refs)
    take = lambda n: [next(it) for _ in range(n)]
    x_refs, p_refs, s_refs = take(n_seg[0]), take(n_seg[1]), take(n_seg[2])
    wp_ref, ws_ref, g_ref, b_ref = take(4)
    x = _seg_read(x_refs, offs[0])
    mixed = _dot(_seg_read(p_refs, offs[1]), wp_ref[...]) + _dot(_seg_read(s_refs, offs[2]), ws_ref[...])
    h = _layer_norm(alpha * x + mixed, g_ref[...], b_ref[...])
    if not route:
        (o_ref,) = take(1)
        o_ref[...] = h
        return
    rw_ref, rb_ref, tril_ref = take(3)
    xt_ref, gate_ref, plan_ref, cnt_ref = take(4)

    @pl.when(pl.program_id(0) == 0)
    def _():
        cnt_ref[...] = jnp.zeros(cnt_ref.shape, F32)

    tm, d = h.shape
    rpt = d // LANES
    for k in range(rpt):
        xt_ref[pl.ds(k, tm, stride=rpt), :] = h[:, k * LANES:(k + 1) * LANES]
    gate, plan = _route(h, rw_ref, rb_ref, tril_ref, cnt_ref, n_experts)
    gate_ref[...] = gate
    plan_ref[...] = plan.T[:plan_ref.shape[0], :]


def _outproj_ln(x_segs, pool_segs, ssd_segs, w_pool, w_ssd, g, b, alpha, router=None):
    d = x_segs[0].shape[1]
    rows = [a.shape[0] for a in x_segs + pool_segs + ssd_segs]
    tm = _common_tile(rows, (512, 256, 128, 64))
    groups = (x_segs, pool_segs, ssd_segs)
    tiles = [_seg_tiles(s, tm) for s in groups]
    t = sum(tiles[0]) * tm
    assert all(sum(ts) * tm == t for ts in tiles)
    in_specs = [sp for s in groups for sp in _seg_specs(s, tm)]
    consts = [w_pool, w_ssd, g, b]
    row = lambda w, dt: (pl.BlockSpec((tm, w), lambda i: (i, 0)), jax.ShapeDtypeStruct((t, w), dt))
    if router is None:
        outs = [row(d, F32)]
        n_experts = 0
    else:
        rw_pad, rb_pad, n_experts = router
        tril = (jnp.arange(tm)[:, None] > jnp.arange(tm)[None, :]).astype(BF16)
        consts += [rw_pad, rb_pad, tril]
        rpt = d // LANES
        outs = [(pl.BlockSpec((tm * rpt, LANES), lambda i: (i, 0)), jax.ShapeDtypeStruct((t * rpt, LANES), F32)),
                row(LANES, F32),
                (pl.BlockSpec((SUBLANES, tm), lambda i: (0, i)), jax.ShapeDtypeStruct((SUBLANES, t), F32)),
                (_const_spec((1, LANES)), jax.ShapeDtypeStruct((1, LANES), F32))]
    in_specs += [_const_spec(c.shape) for c in consts]
    res = pl.pallas_call(
        functools.partial(_outproj_ln_kernel, alpha=alpha, n_seg=[len(s) for s in groups],
                          offs=[_seg_offsets(ts) for ts in tiles], route=router is not None, n_experts=n_experts),
        grid=(t // tm,),
        in_specs=in_specs,
        out_specs=[o[0] for o in outs],
        out_shape=[o[1] for o in outs],
        compiler_params=pltpu.CompilerParams(dimension_semantics=("arbitrary",),
                                             vmem_limit_bytes=_vmem_limit(40 * 1024 * 1024)),
        name="outproj_ln",
    )(*x_segs, *pool_segs, *ssd_segs, *consts)
    return res[0] if router is None else res


def _swiglu(xb, wg_ref, wu_ref, wd_ref):
    act = (_silu(_dot(xb, wg_ref[0])) * _dot(xb, wu_ref[0])).astype(BF16)
    return _dot(act, wd_ref[0])


def _dense_ffn_ln_kernel(*refs, alpha, offs, tiles, n_side):
    x_ref, wg_ref, wu_ref, wd_ref, g_ref, b_ref = refs[:6]
    side_in = refs[6:6 + n_side]
    out_refs = refs[6 + n_side:6 + n_side + len(tiles)]
    side_out = refs[6 + n_side + len(tiles):]
    x = x_ref[...]
    f = _swiglu(x.astype(BF16), wg_ref, wu_ref, wd_ref)
    _seg_write(out_refs, offs, tiles, _layer_norm(alpha * x + f, g_ref[...], b_ref[...]))
    _side_cast(side_in, side_out)


def _dense_ffn_ln(x, wg, wu, wd, g, b, alpha, out_rows, side_casts=()):
    t, d = x.shape
    ff = wg.shape[2]
    tm = _common_tile(out_rows, (256, 128, 64))
    tiles = [n // tm for n in out_rows]
    offs = _seg_offsets(tiles)
    steps = t // tm
    side2d, side_in, side_out, side_shapes, side_bytes = _side_cast_plan(side_casts, steps)
    vmem = 3 * d * ff * 2 + 4 * tm * d * 4 + tm * ff * 12 + side_bytes
    res = pl.pallas_call(
        functools.partial(_dense_ffn_ln_kernel, alpha=alpha, offs=offs, tiles=tiles, n_side=len(side2d)),
        grid=(steps,),
        in_specs=[pl.BlockSpec((tm, d), lambda i: (i, 0)), _const_spec(wg.shape, single=True),
                  _const_spec(wu.shape, single=True), _const_spec(wd.shape, single=True),
                  _const_spec(g.shape), _const_spec(b.shape)] + side_in,
        out_specs=[pl.BlockSpec((tm, d), lambda i, o=o, n=n: (jnp.clip(i - o, 0, n - 1), 0))
                   for o, n in zip(offs, tiles)] + side_out,
        out_shape=[jax.ShapeDtypeStruct((n, d), F32) for n in out_rows] + side_shapes,
        compiler_params=pltpu.CompilerParams(dimension_semantics=("arbitrary",),
                                             vmem_limit_bytes=_vmem_limit(vmem)),
        name="dense_ffn_ln",
    )(x, wg, wu, wd, g, b, *side2d)
    return list(res[:len(tiles)]), [r.reshape(a.shape[1:]) for r, (a, _) in zip(res[len(tiles):], side_casts)]


def _dispatch_kernel(ends_ref, slot_ref, x_ref, out_hbm, zeros_ref, sem, zsem, *, tokens, rpt, tile, n_experts):
    i = pl.program_id(0)

    def tail_copy(e):
        start = pl.multiple_of((ends_ref[e] - tile) * rpt, tile * rpt)
        return pltpu.make_async_copy(zeros_ref, out_hbm.at[pl.ds(start, tile * rpt)], zsem)

    def nonempty(e):
        return ends_ref[e] > (ends_ref[e - 1] if e > 0 else 0)

    def unused_copy(j):
        start = pl.multiple_of((ends_ref[n_experts - 1] + j * tile) * rpt, tile * rpt)
        return pltpu.make_async_copy(zeros_ref, out_hbm.at[pl.ds(start, tile * rpt)], zsem)

    def unused(j):
        return ends_ref[n_experts - 1] + (j + 1) * tile <= out_hbm.shape[0] // rpt

    @pl.when(i == 0)
    def _():
        zeros_ref[...] = jnp.zeros(zeros_ref.shape, zeros_ref.dtype)
        for e in range(n_experts):
            @pl.when(nonempty(e))
            def _(e=e):
                tail_copy(e).start()

            @pl.when(unused(e))
            def _(e=e):
                unused_copy(e).start()
        for e in range(n_experts):
            @pl.when(nonempty(e))
            def _(e=e):
                tail_copy(e).wait()

            @pl.when(unused(e))
            def _(e=e):
                unused_copy(e).wait()

    def issue(r, carry):
        src = x_ref.at[pl.ds(pl.multiple_of(r * rpt, rpt), rpt)]
        for k in range(TOP_K):
            dst = out_hbm.at[pl.ds(pl.multiple_of(slot_ref[0, 0, k * tokens + r] * rpt, rpt), rpt)]
            pltpu.make_async_copy(src, dst, sem).start()
        return carry

    lax.fori_loop(0, tokens, issue, 0)
    for k in range(TOP_K):
        pltpu.make_async_copy(x_ref, out_hbm.at[pl.ds(0, tokens * rpt)], sem).wait()


def _dispatch(xt, slots, ends, n_slots, tokens, rpt, n_experts):
    steps = slots.shape[0]
    return pl.pallas_call(
        functools.partial(_dispatch_kernel, tokens=tokens, rpt=rpt, tile=MOE_TILE, n_experts=n_experts),
        grid_spec=pltpu.PrefetchScalarGridSpec(
            num_scalar_prefetch=1, grid=(steps,),
            in_specs=[pl.BlockSpec((1, 1, TOP_K * tokens), lambda i, e: (i, 0, 0), memory_space=pltpu.SMEM),
                      pl.BlockSpec((tokens * rpt, LANES), lambda i, e: (i, 0))],
            out_specs=pl.BlockSpec(memory_space=pl.ANY),
            scratch_shapes=[pltpu.VMEM((MOE_TILE * rpt, LANES), xt.dtype),
                            pltpu.SemaphoreType.DMA(()), pltpu.SemaphoreType.DMA(())]),
        out_shape=jax.ShapeDtypeStruct((n_slots * rpt, LANES), xt.dtype),
        compiler_params=pltpu.CompilerParams(dimension_semantics=("arbitrary",)),
        name="moe_dispatch",
    )(ends, slots, xt)


def _moe_ffn_kernel(te_ref, na_ref, x_ref, wg_ref, wu_ref, wd_ref, o_ref, *, tm, rpt):
    active = pl.program_id(0) < na_ref[0]

    @pl.when(active)
    def _():
        xb = jnp.concatenate([x_ref[pl.ds(k, tm, stride=rpt), :] for k in range(rpt)], axis=1).astype(BF16)
        f = _swiglu(xb, wg_ref, wu_ref, wd_ref)
        for k in range(rpt):
            o_ref[pl.ds(k, tm, stride=rpt), :] = f[:, k * LANES:(k + 1) * LANES]

    @pl.when(jnp.logical_not(active))
    def _():
        o_ref[...] = jnp.zeros(o_ref.shape, o_ref.dtype)


def _moe_ffn(xs, wg, wu, wd, tile_expert, n_active, rpt):
    tm = MOE_TILE
    d, ff = wg.shape[1], wg.shape[2]
    n_tiles = xs.shape[0] // (tm * rpt)
    w_idx = lambda i, te, na: (te[i], 0, 0)
    vmem = 2 * 3 * d * ff * 2 + 4 * tm * d * 4 + tm * ff * 12 + 2 * tm * d * 4
    return pl.pallas_call(
        functools.partial(_moe_ffn_kernel, tm=tm, rpt=rpt),
        grid_spec=pltpu.PrefetchScalarGridSpec(
            num_scalar_prefetch=2, grid=(n_tiles,),
            in_specs=[pl.BlockSpec((tm * rpt, LANES), lambda i, te, na: (jnp.minimum(i, na[0] - 1), 0)),
                      pl.BlockSpec((1, d, ff), w_idx), pl.BlockSpec((1, d, ff), w_idx),
                      pl.BlockSpec((1, ff, d), w_idx)],
            out_specs=pl.BlockSpec((tm * rpt, LANES), lambda i, te, na: (i, 0))),
        out_shape=jax.ShapeDtypeStruct(xs.shape, F32),
        compiler_params=pltpu.CompilerParams(dimension_semantics=("arbitrary",),
                                             vmem_limit_bytes=_vmem_limit(vmem)),
        name="moe_ffn",
    )(tile_expert, n_active, xs, wg, wu, wd)


def _combine_ln_kernel(slot_ref, next_ref, x_ref, gate_ref, y_hbm, g_ref, b_ref, *rest, alpha, tm, rpt, offs, tiles):
    out_refs, (buf, sem) = rest[:len(tiles)], rest[len(tiles):]
    i = pl.program_id(0)
    n_rows = TOP_K * tm * rpt

    def issue(s_ref, slot):
        def body(r, carry):
            src = y_hbm.at[pl.ds(pl.multiple_of(s_ref[0, 0, r] * rpt, rpt), rpt)]
            pltpu.make_async_copy(src, buf.at[slot, pl.ds(pl.multiple_of(r * rpt, rpt), rpt)], sem.at[slot]).start()
            return carry
        lax.fori_loop(0, TOP_K * tm, body, 0)

    @pl.when(i == 0)
    def _():
        issue(slot_ref, 0)

    @pl.when(i + 1 < pl.num_programs(0))
    def _():
        issue(next_ref, (i + 1) % 2)

    cur = i % 2
    pltpu.make_async_copy(y_hbm.at[pl.ds(0, n_rows)], buf.at[cur], sem.at[cur]).wait()
    g1 = gate_ref[:, 0:1]
    g2 = gate_ref[:, 1:2]
    cols = []
    for k in range(rpt):
        xk = x_ref[pl.ds(k, tm, stride=rpt), :]
        ya = buf[cur, pl.ds(k, tm, stride=rpt), :]
        yb = buf[cur, pl.ds(tm * rpt + k, tm, stride=rpt), :]
        cols.append(alpha * xk + (g1 * ya + g2 * yb))
    h = jnp.concatenate(cols, axis=1)
    _seg_write(out_refs, offs, tiles, _layer_norm(h, g_ref[...], b_ref[...]))


def _combine_ln(xt, gate, ys, slots, g, b, alpha, tm, rpt, out_rows):
    d = g.shape[1]
    tiles = [n // tm for n in out_rows]
    offs = _seg_offsets(tiles)
    steps = slots.shape[0]
    slot_spec = lambda f: pl.BlockSpec((1, 1, TOP_K * tm), f, memory_space=pltpu.SMEM)
    res = pl.pallas_call(
        functools.partial(_combine_ln_kernel, alpha=alpha, tm=tm, rpt=rpt, offs=offs, tiles=tiles),
        grid=(steps,),
        in_specs=[slot_spec(lambda i: (i, 0, 0)), slot_spec(lambda i: (jnp.minimum(i + 1, steps - 1), 0, 0)),
                  pl.BlockSpec((tm * rpt, LANES), lambda i: (i, 0)), pl.BlockSpec((tm, LANES), lambda i: (i, 0)),
                  pl.BlockSpec(memory_space=pl.ANY), _const_spec(g.shape), _const_spec(b.shape)],
        out_specs=[pl.BlockSpec((tm, d), lambda i, o=o, n=n: (jnp.clip(i - o, 0, n - 1), 0))
                   for o, n in zip(offs, tiles)],
        out_shape=[jax.ShapeDtypeStruct((n, d), F32) for n in out_rows],
        scratch_shapes=[pltpu.VMEM((2, TOP_K * tm * rpt, LANES), F32), pltpu.SemaphoreType.DMA((2,))],
        compiler_params=pltpu.CompilerParams(dimension_semantics=("arbitrary",),
                                             vmem_limit_bytes=_vmem_limit(32 * 1024 * 1024)),
        name="moe_combine_ln",
    )(slots, slots, xt, gate, ys, g, b)
    return list(res)


def _moe_ffn_ln(xt, gate, plan, counts, wg, wu, wd, g, b, alpha, tm, out_rows):
    n_experts = wg.shape[0]
    d = wg.shape[1]
    rpt = d // LANES
    t = gate.shape[0]
    tile = MOE_TILE
    cnt = counts[0, :n_experts].astype(jnp.int32)
    padded = ((cnt + tile - 1) // tile) * tile
    ends = jnp.cumsum(padded).astype(jnp.int32)
    starts = ends - padded
    plan = plan.astype(jnp.int32)
    slot = plan[TOP_K:2 * TOP_K]
    for e in range(n_experts):
        slot = slot + jnp.where(plan[:TOP_K] == e, starts[e], 0)
    n_slots = -(-(TOP_K * t + n_experts * (tile - 1)) // tile) * tile
    n_tiles = n_slots // tile
    n_active = ends[-1:] // tile
    tile_start = jnp.minimum(jnp.arange(n_tiles, dtype=jnp.int32), n_active[0] - 1) * tile
    tile_expert = jnp.sum((tile_start[:, None] >= ends[None, :]).astype(jnp.int32), axis=1)
    slots = jnp.swapaxes(slot.reshape(TOP_K, t // tm, tm), 0, 1).reshape(t // tm, 1, TOP_K * tm)
    xs = _dispatch(xt, slots, ends, n_slots, tm, rpt, n_experts)
    ys = _moe_ffn(xs, wg, wu, wd, tile_expert, n_active, rpt)
    return _combine_ln(xt, gate, ys, slots, g, b, alpha, tm, rpt, out_rows)


def kernel(x_prompt, x_sample, state_pool, state_conv, state_ssm, w_in, conv_w, conv_b, dt_bias, A_log, D_skip,
           ssm_norm_w, pool_w, pool_scale, w_out, ln1_g, ln1_b, ln2_g, ln2_b, ffn_w_gate, ffn_w_up, ffn_w_down,
           router_w, router_b, moe_w_gate, moe_w_up, moe_w_down):
    bp, seq, d = x_prompt.shape
    bs, n_new, _ = x_sample.shape
    depth = w_in.shape[0]
    d_pool = pool_scale.shape[1]
    d_ssm = ssm_norm_w.shape[1]
    d_xbc = conv_w.shape[2]
    n_heads = dt_bias.shape[1]
    tp, ts = bp * seq, bs * n_new
    alpha = (2.0 * depth) ** 0.25
    assert d_pool == POOL_GROUP * len(POOL_WINDOWS) and d_ssm == n_heads * HEAD_DIM
    assert d_xbc == d_ssm + 2 * N_GROUPS * D_STATE and seq % CHUNK == 0 and n_new <= CONV_W
    assert n_heads <= LANES and bs % (CHUNK // SAMPLE_ROWS) == 0 and d % LANES == 0

    consts_prompt = _ssd_consts(CHUNK, n_heads)
    consts_sample = _ssd_consts(SAMPLE_ROWS, n_heads)
    x_segs = [x_prompt.reshape(tp, d), x_sample.reshape(ts, d)]
    new_pool_p, new_conv_p, new_ssm_p, new_pool_s, new_conv_s = [], [], [], [], []
    ssm_s = None
    n_in = w_in.shape[2]
    w_in_pad_all = jnp.pad(w_in.astype(BF16), ((0, 0), (0, 0), (0, d_pool + d_ssm + d_xbc + LANES - n_in)))
    for l in range(depth):
        out_rows = [tp, ts]
        w_in_pad = w_in_pad_all[l]
        lp = dict(
            conv_w=conv_w[l], conv_b=conv_b[l][None, :],
            dt_bias=jnp.zeros((1, LANES), F32).at[0, :n_heads].set(dt_bias[l]),
            a_log=jnp.zeros((1, LANES), F32).at[0, :n_heads].set(A_log[l]),
            d_exp=jnp.repeat(D_skip[l], HEAD_DIM)[None, :], norm_w=ssm_norm_w[l][None, :],
            ssd_consts_prompt=consts_prompt, ssd_consts_sample=consts_sample)
        pw = pool_w[l].astype(BF16)
        ps = pool_scale[l][None, :]

        proj = (w_in_pad, lp["conv_w"], lp["conv_b"], lp["dt_bias"], d_pool, d_ssm, d_xbc)

        dense = l % 2 == 0
        side = ((ffn_w_gate, l // 2), (ffn_w_up, l // 2), (ffn_w_down, l // 2)) if dense else ()
        (u_p, sz_p, xact_p, dt_p, c_p), cast = _in_proj(x_segs[0], *proj, seq_len=seq, side_casts=side)
        ffn_bf16 = [c[None] for c in cast] if dense else None
        pool_p, buf_p = _pool_prompt(u_p, tp, seq, pw, ps)
        y_p, h_p = _ssd_prompt(sz_p, xact_p, dt_p, bp, seq, lp)
        new_pool_p.append(buf_p)
        new_conv_p.append(c_p)
        new_ssm_p.append(jnp.concatenate(h_p, axis=0).reshape(bp, n_heads, HEAD_DIM, D_STATE))

        u_s, sz_s, xbc_s, dt_s = _in_proj_sample(x_segs[1], w_in_pad, lp["dt_bias"], d_pool, d_ssm, d_xbc)
        u_s = u_s.reshape(bs, n_new, d_pool)
        xbc_s = xbc_s.reshape(bs, n_new, d_xbc)
        pool_ext = jnp.concatenate([state_pool[l], u_s], axis=1)
        pool_s = _pool_sample(jnp.swapaxes(pool_ext, 0, 1), pw, ps, n_new, PAST_LEN)
        pool_s = jnp.swapaxes(pool_s, 0, 1).reshape(ts, d_pool)
        conv_ext = jnp.concatenate([state_conv[l], xbc_s], axis=1)
        lead = CONV_LEAD - (CONV_W - 1)
        ext = jnp.pad(conv_ext, ((0, 0), (lead, SAMPLE_EXT_ROWS - lead - conv_ext.shape[1]), (0, 0)))
        pad_rows = lambda a: jnp.pad(a.reshape(bs, n_new, -1), ((0, 0), (0, SAMPLE_ROWS - n_new), (0, 0))
                                     ).reshape(bs * SAMPLE_ROWS, -1)
        y_s16, ssm_s = _ssd_sample(pad_rows(sz_s), ext.reshape(bs * SAMPLE_EXT_ROWS, d_xbc), pad_rows(dt_s),
                                   state_ssm.reshape(depth, bs, d_ssm, D_STATE), ssm_s, l, lp, n_new)
        y_s = y_s16.reshape(bs, SAMPLE_ROWS, d_ssm)[:, :n_new].reshape(ts, d_ssm)
        new_pool_s.append(pool_ext[:, n_new:])
        new_conv_s.append(conv_ext[:, n_new:])

        w_o = w_out[l].astype(BF16)
        ln1 = (ln1_g[l][None, :], ln1_b[l][None, :])
        g2, b2 = ln2_g[l][None, :], ln2_b[l][None, :]
        j = l // 2
        if l % 2 == 0:
            x1 = _outproj_ln(x_segs, [pool_p, pool_s], [*y_p, y_s], w_o[:d_pool], w_o[d_pool:], *ln1, alpha)
            side = ((moe_w_gate, j), (moe_w_up, j), (moe_w_down, j)) if l + 1 < depth else ()
            x_segs, moe_bf16 = _dense_ffn_ln(x1, *ffn_bf16, g2, b2, alpha, out_rows, side)
        else:
            n_experts = router_w.shape[2]
            rw_pad = jnp.zeros((d, LANES), F32).at[:, :n_experts].set(router_w[j])
            rb_pad = jnp.zeros((1, LANES), F32).at[0, :n_experts].set(router_b[j])
            xt, gate, plan, counts = _outproj_ln(x_segs, [pool_p, pool_s], [*y_p, y_s], w_o[:d_pool], w_o[d_pool:],
                                                 *ln1, alpha, router=(rw_pad, rb_pad, n_experts))
            tm = _common_tile([tp, ts], (512, 256, 128, 64))
            x_segs = _moe_ffn_ln(xt, gate, plan, counts, *moe_bf16, g2, b2, alpha, tm, out_rows)

    y_prompt, y_sample = x_segs
    return (y_prompt.reshape(bp, seq, d), y_sample.reshape(bs, n_new, d),
            jnp.stack(new_pool_p), jnp.stack(new_conv_p), jnp.stack(new_ssm_p),
            jnp.stack(new_pool_s), jnp.stack(new_conv_s),
            ssm_s.reshape(depth, bs, n_heads, HEAD_DIM, D_STATE))
```

```python
import functools

import jax
import jax.numpy as jnp
from jax import lax
from jax.experimental import pallas as pl
from jax.experimental.pallas import tpu as pltpu

F32 = jnp.float32
BF16 = jnp.bfloat16

PAST_LEN = 16384
POOL_WINDOWS = (2, 4, 8, 16)
POOL_GROUP = 128
POOL_BUF = max(POOL_WINDOWS) - 1
HEAD_DIM = 64
N_GROUPS = 4
D_STATE = 128
CONV_W = 4
CHUNK = 128
TOP_K = 2
LN_EPS = 1e-5
RMS_EPS = 1e-6
LOG2_E = 1.4426950408889634

LANES = 128
SUBLANES = 8
V7X_VMEM_BYTES = 64 * 1024 * 1024
VMEM_CAP = V7X_VMEM_BYTES - 8 * 1024 * 1024

SAMPLE_ROWS = 16
SAMPLE_EXT_ROWS = 24
CONV_LEAD = 8
MOE_TILE = 256
SSD_WAYS = 2


def _vmem_limit(nbytes):
    return int(min(VMEM_CAP, nbytes + 6 * 1024 * 1024))


def _dot(a, b):
    return jnp.dot(a, b, preferred_element_type=F32)


def _dot_nt(a, b):
    return lax.dot_general(a, b, (((1,), (1,)), ((), ())), preferred_element_type=F32)


def _split3(v):
    hi = v.astype(BF16)
    r = v - hi.astype(F32)
    mid = r.astype(BF16)
    lo = (r - mid.astype(F32)).astype(BF16)
    return hi, mid, lo


def _sel_right(parts, m):
    out = _dot(parts[0], m)
    for p in parts[1:]:
        out = out + _dot(p, m)
    return out


def _sel_left(m, parts):
    out = _dot(m, parts[0])
    for p in parts[1:]:
        out = out + _dot(m, p)
    return out


def _silu(x):
    return x / (1.0 + jnp.exp2(x * (-LOG2_E)))


def _layer_norm(h, g, b):
    mu = jnp.mean(h, axis=-1, keepdims=True)
    d = h - mu
    var = jnp.mean(d * d, axis=-1, keepdims=True)
    return d * lax.rsqrt(var + LN_EPS) * g + b


def _pick_tile(n, candidates):
    for c in candidates:
        if n % c == 0:
            return c
    raise ValueError(f"no tile in {candidates} divides {n}")


def _const_spec(shape, single=False):
    nd = len(shape)
    if single:
        return pl.BlockSpec(shape, lambda *_: (0,) * nd, pipeline_mode=pl.Buffered(1))
    return pl.BlockSpec(shape, lambda *_: (0,) * nd)


def _seg_tiles(segs, tm, rows_per_token=1):
    return [a.shape[0] // (tm * rows_per_token) for a in segs]


def _seg_offsets(tiles):
    offs, off = [], 0
    for n in tiles:
        offs.append(off)
        off += n
    return offs


def _seg_specs(segs, tm, rows_per_token=1):
    tiles = _seg_tiles(segs, tm, rows_per_token)
    return [pl.BlockSpec((tm * rows_per_token, a.shape[1]),
                         lambda i, *_, o=o, n=n: (jnp.clip(i - o, 0, n - 1), 0))
            for a, o, n in zip(segs, _seg_offsets(tiles), tiles)]


def _seg_read(refs, offs):
    v = refs[0][...]
    for r, o in zip(refs[1:], offs[1:]):
        v = jnp.where(pl.program_id(0) >= o, r[...], v)
    return v


def _seg_write(refs, offs, tiles, v):
    i = pl.program_id(0)
    for r, o, n in zip(refs, offs, tiles):
        @pl.when(jnp.logical_and(i >= o, i < o + n))
        def _(r=r):
            r[...] = v.astype(r.dtype)


def _common_tile(row_counts, candidates):
    for c in candidates:
        if all(n % c == 0 for n in row_counts):
            return c
    raise ValueError(f"no tile in {candidates} divides all of {row_counts}")


INPROJ_CHUNK = 512


def _row_blocks(rows, steps, align):
    for n in range(min(steps, rows), 0, -1):
        if rows % n == 0 and (rows // n) % align == 0:
            return n
    raise ValueError(f"cannot split {rows} rows into at most {steps} blocks of a multiple of {align} rows")


def _side_cast_plan(entries, steps):
    side2d, in_specs, out_specs, out_shapes = [], [], [], []
    for stacked, layer in entries:
        rows = 1
        for n in stacked.shape[1:-1]:
            rows *= n
        cols = stacked.shape[-1]
        nb = _row_blocks(rows, steps, 2 * SUBLANES)
        side2d.append(stacked.reshape(stacked.shape[0], rows, cols))
        in_specs.append(pl.BlockSpec((None, rows // nb, cols),
                                     lambda i, nb=nb, layer=layer: (layer, jnp.minimum(i, nb - 1), 0)))
        out_specs.append(pl.BlockSpec((rows // nb, cols), lambda i, nb=nb: (jnp.minimum(i, nb - 1), 0)))
        out_shapes.append(jax.ShapeDtypeStruct((rows, cols), BF16))
    vmem = sum(2 * (4 + 2) * s.block_shape[0] * s.block_shape[1] for s in out_specs)
    return side2d, in_specs, out_specs, out_shapes, vmem


def _side_cast(side_in, side_out):
    for src, dst in zip(side_in, side_out):
        dst[...] = src[...].astype(dst.dtype)


def _inproj_kernel(*refs, splits, tiles_per_seq, n_side):
    x_ref, w_ref, convw_ref, convb_ref, dtb_ref = refs[:5]
    side_in = refs[5:5 + n_side]
    u_ref, sz_ref, xo_ref, dt_ref, cstate_ref = refs[5 + n_side:10 + n_side]
    side_out, ext_ref = refs[10 + n_side:10 + 2 * n_side], refs[10 + 2 * n_side]
    (u0, u1), (z0, z1), (c0, c1), (d0, d1) = splits
    tm = x_ref.shape[0]
    xb = x_ref[...].astype(BF16)
    u_ref[...] = _dot(xb, w_ref[:, u0:u1])
    dt_ref[...] = _softplus(_dot(xb, w_ref[:, d0:d1]) + dtb_ref[...])
    for lo in range(z0, z1, INPROJ_CHUNK):
        sz_ref[:, lo - z0:lo - z0 + INPROJ_CHUNK] = _silu(_dot(xb, w_ref[:, lo:lo + INPROJ_CHUNK]))
    _side_cast(side_in, side_out)

    pos = pl.program_id(0) % tiles_per_seq

    @pl.when(pos == 0)
    def _():
        ext_ref[0:CONV_LEAD, :] = jnp.zeros((CONV_LEAD, ext_ref.shape[1]), F32)

    @pl.when(pos > 0)
    def _():
        ext_ref[0:CONV_LEAD, :] = ext_ref[tm:tm + CONV_LEAD, :]

    first = CONV_LEAD - (CONV_W - 1)
    for lo in range(c0, c1, INPROJ_CHUNK):
        cols = slice(lo - c0, lo - c0 + INPROJ_CHUNK)
        ext_ref[CONV_LEAD:CONV_LEAD + tm, cols] = _dot(xb, w_ref[:, lo:lo + INPROJ_CHUNK])
        rows = ext_ref[:, cols]
        acc = convb_ref[:, cols] + (pltpu.roll(rows, CONV_W - 1, axis=0)[CONV_LEAD:] * convw_ref[0:1, cols])
        for k in range(1, CONV_W - 1):
            acc = acc + pltpu.roll(rows, CONV_W - 1 - k, axis=0)[CONV_LEAD:] * convw_ref[k:k + 1, cols]
        acc = acc + rows[CONV_LEAD:] * convw_ref[CONV_W - 1:CONV_W, cols]
        xo_ref[:, cols] = _silu(acc)

    @pl.when(pos == tiles_per_seq - 1)
    def _():
        cstate_ref[0] = ext_ref[CONV_LEAD + tm - (CONV_W - 1):CONV_LEAD + tm, :]


def _inproj_splits(w_pad, d_pool, d_ssm, d_xbc):
    assert d_ssm % INPROJ_CHUNK == 0 and d_xbc % INPROJ_CHUNK == 0
    return ((0, d_pool), (d_pool, d_pool + d_ssm), (d_pool + d_ssm, d_pool + d_ssm + d_xbc),
            (d_pool + d_ssm + d_xbc, w_pad.shape[1]))


def _in_proj(x, w_pad, conv_w, conv_b, dt_bias, d_pool, d_ssm, d_xbc, seq_len, side_casts=()):
    t, d = x.shape
    n = w_pad.shape[1]
    tm = _pick_tile(seq_len, (512, 256, 128, 64))
    splits = _inproj_splits(w_pad, d_pool, d_ssm, d_xbc)
    widths = [hi - lo for lo, hi in splits]
    tiles_per_seq = seq_len // tm
    side2d, side_in, side_out, side_shapes, side_bytes = _side_cast_plan(side_casts, t // tm)
    vmem = (2 * tm * d * 4 + d * n * 2 + 2 * tm * n * 4 + 2 * tm * d_xbc * 4 + 4 * tm * INPROJ_CHUNK * 4
            + side_bytes)
    res = pl.pallas_call(
        functools.partial(_inproj_kernel, splits=splits, tiles_per_seq=tiles_per_seq, n_side=len(side2d)),
        grid=(t // tm,),
        in_specs=[pl.BlockSpec((tm, d), lambda i: (i, 0)), _const_spec((d, n), single=True),
                  _const_spec(conv_w.shape), _const_spec(conv_b.shape), _const_spec(dt_bias.shape)] + side_in,
        out_specs=[pl.BlockSpec((tm, w), lambda i: (i, 0)) for w in widths]
                  + [pl.BlockSpec((1, CONV_W - 1, d_xbc), lambda i: (i // tiles_per_seq, 0, 0))] + side_out,
        out_shape=[jax.ShapeDtypeStruct((t, w), F32) for w in widths]
                  + [jax.ShapeDtypeStruct((t // seq_len, CONV_W - 1, d_xbc), F32)] + side_shapes,
        scratch_shapes=[pltpu.VMEM((tm + CONV_LEAD, d_xbc), F32)],
        compiler_params=pltpu.CompilerParams(dimension_semantics=("arbitrary",),
                                             vmem_limit_bytes=_vmem_limit(vmem)),
        name="in_proj",
    )(x, w_pad, conv_w, conv_b, dt_bias, *side2d)
    return list(res[:5]), [r.reshape(a.shape[1:]) for r, (a, _) in zip(res[5:], side_casts)]


def _inproj_sample_kernel(x_ref, w_ref, dtb_ref, u_ref, sz_ref, xbc_ref, dt_ref, *, splits):
    (u0, u1), (z0, z1), (c0, c1), (d0, d1) = splits
    xb = x_ref[...].astype(BF16)
    u_ref[...] = _dot(xb, w_ref[:, u0:u1])
    sz_ref[...] = _silu(_dot(xb, w_ref[:, z0:z1]))
    xbc_ref[...] = _dot(xb, w_ref[:, c0:c1])
    dt_ref[...] = _softplus(_dot(xb, w_ref[:, d0:d1]) + dtb_ref[...])


def _in_proj_sample(x, w_pad, dt_bias, d_pool, d_ssm, d_xbc):
    t, d = x.shape
    n = w_pad.shape[1]
    tm = _pick_tile(t, (512, 256, 128, 64))
    splits = _inproj_splits(w_pad, d_pool, d_ssm, d_xbc)
    widths = [hi - lo for lo, hi in splits]
    vmem = 2 * tm * d * 4 + 2 * d * n * 2 + 4 * tm * n * 4
    return pl.pallas_call(
        functools.partial(_inproj_sample_kernel, splits=splits),
        grid=(t // tm,),
        in_specs=[pl.BlockSpec((tm, d), lambda i: (i, 0)), _const_spec((d, n)), _const_spec(dt_bias.shape)],
        out_specs=[pl.BlockSpec((tm, w), lambda i: (i, 0)) for w in widths],
        out_shape=[jax.ShapeDtypeStruct((t, w), F32) for w in widths],
        compiler_params=pltpu.CompilerParams(dimension_semantics=("arbitrary",),
                                             vmem_limit_bytes=_vmem_limit(vmem)),
        name="in_proj_sample",
    )(x, w_pad, dt_bias)


def _pool_prompt_kernel(u_ref, w_ref, scale_ref, o_ref, buf_ref, ext_ref, *, tl):
    j = pl.program_id(1)

    @pl.when(j == pl.num_programs(1) - 1)
    def _():
        buf_ref[0] = u_ref[tl - POOL_BUF:tl, :]

    @pl.when(j == 0)
    def _():
        ext_ref[0:16, :] = jnp.zeros((16, ext_ref.shape[1]), F32)

    @pl.when(j > 0)
    def _():
        ext_ref[0:16, :] = ext_ref[tl:tl + 16, :]

    ext_ref[16:16 + tl, :] = u_ref[...]
    pos = j * tl + lax.broadcasted_iota(jnp.int32, (tl, POOL_GROUP), 0)
    for g, win in enumerate(POOL_WINDOWS):
        lanes = slice(g * POOL_GROUP, (g + 1) * POOL_GROUP)
        rows = ext_ref[:, lanes]
        cur = rows[16:]
        span = 1
        while span < win:
            rows = rows + pltpu.roll(rows, span, axis=0)
            span *= 2
        acc = rows[16:]
        cnt = jnp.minimum(pos + 1, win).astype(F32)
        diff = (acc / cnt - cur).astype(BF16)
        out = _dot(diff, w_ref[g]) * scale_ref[:, lanes]
        o_ref[:, lanes] = out.astype(o_ref.dtype)


def _pool_prompt(u, n_rows, seq_len, pool_w, pool_scale):
    d_pool = u.shape[1]
    tl = _pick_tile(seq_len, (512, 256, 128))
    nj = seq_len // tl
    nb = n_rows // seq_len
    return pl.pallas_call(
        functools.partial(_pool_prompt_kernel, tl=tl),
        grid=(nb, nj),
        in_specs=[pl.BlockSpec((tl, d_pool), lambda b, j: (b * nj + j, 0)),
                  _const_spec(pool_w.shape), _const_spec(pool_scale.shape)],
        out_specs=[pl.BlockSpec((tl, d_pool), lambda b, j: (b * nj + j, 0)),
                   pl.BlockSpec((1, POOL_BUF, d_pool), lambda b, j: (b, 0, 0))],
        out_shape=[jax.ShapeDtypeStruct((n_rows, d_pool), BF16),
                   jax.ShapeDtypeStruct((nb, POOL_BUF, d_pool), F32)],
        scratch_shapes=[pltpu.VMEM((tl + 16, d_pool), F32)],
        compiler_params=pltpu.CompilerParams(dimension_semantics=("arbitrary", "arbitrary")),
        name="pool_prompt",
    )(u, pool_w, pool_scale)


def _pool_sample_kernel(ext_ref, w_ref, scale_ref, o_ref, *, n_new, start):
    for t in range(n_new):
        for g, win in enumerate(POOL_WINDOWS):
            lanes = slice(g * POOL_GROUP, (g + 1) * POOL_GROUP)
            cur = ext_ref[POOL_BUF + t, :, lanes]
            acc = cur
            for k in range(1, win):
                acc = acc + ext_ref[POOL_BUF + t - k, :, lanes]
            cnt = float(min(start + t + 1, win))
            diff = (acc / cnt - cur).astype(BF16)
            out = _dot(diff, w_ref[g]) * scale_ref[:, lanes]
            o_ref[t, :, lanes] = out.astype(o_ref.dtype)


def _pool_sample(ext_t, pool_w, pool_scale, n_new, start):
    rows, bs, d_pool = ext_t.shape
    return pl.pallas_call(
        functools.partial(_pool_sample_kernel, n_new=n_new, start=start),
        grid=(1,),
        in_specs=[_const_spec(ext_t.shape), _const_spec(pool_w.shape), _const_spec(pool_scale.shape)],
        out_specs=_const_spec((n_new, bs, d_pool)),
        out_shape=jax.ShapeDtypeStruct((n_new, bs, d_pool), BF16),
        compiler_params=pltpu.CompilerParams(dimension_semantics=("arbitrary",)),
        name="pool_sample",
    )(ext_t, pool_w, pool_scale)


def _conv_silu(window, w_ref, b_ref):
    acc = b_ref[...] + window(0) * w_ref[0:1, :]
    for k in range(1, CONV_W):
        acc = acc + window(k) * w_ref[k:k + 1, :]
    return _silu(acc)


def _softplus(x):
    return jnp.maximum(x, 0.0) + jnp.log1p(jnp.exp(-jnp.abs(x)))


def _ssd_chunk(xact, sz, dt, prm, seq_rows, read_state, write_state, write_out):
    (cums_ref, spread_ref, alog_ref, dexp_ref, normw_ref) = prm
    q = xact.shape[0]
    d_ssm = sz.shape[1]
    gw = d_ssm // N_GROUPS
    n_seq = q // seq_rows
    xs = xact[:, :d_ssm]
    bm = xact[:, d_ssm:d_ssm + N_GROUPS * D_STATE]
    cm = xact[:, d_ssm + N_GROUPS * D_STATE:]

    a = dt * (-jnp.exp(alog_ref[...]) * LOG2_E)
    sums = _dot(cums_ref[...], jnp.concatenate(_split3(a), axis=0))
    cum, rcum = sums[:q], sums[q:]
    w = jnp.exp2(rcum) * dt
    cum_t = cum.T
    dt_t = dt.T
    two_terms = lambda v: jnp.concatenate(_split3(v)[:2], axis=1)
    spread = _dot(jnp.concatenate([two_terms(w), two_terms(jnp.exp2(cum))], axis=0), spread_ref[...])
    wx = xs * spread[:q]
    ecum_e = spread[q:]

    ii = lax.broadcasted_iota(jnp.int32, (q, q), 0)
    jj = lax.broadcasted_iota(jnp.int32, (q, q), 1)
    mask = ii >= jj
    if n_seq > 1:
        mask = jnp.logical_and(mask, (ii // seq_rows) == (jj // seq_rows))
    lo_half = lax.broadcasted_iota(jnp.int32, (q, LANES), 1) < HEAD_DIM
    col = lax.broadcasted_iota(jnp.int32, (gw, q), 1)

    for g in range(N_GROUPS):
        gcols = slice(g * gw, (g + 1) * gw)
        bg = bm[:, g * D_STATE:(g + 1) * D_STATE].astype(BF16)
        cg = cm[:, g * D_STATE:(g + 1) * D_STATE].astype(BF16)
        cb = _dot_nt(cg, bg)
        ydiag = []
        end_decay = {}
        for pr in range(gw // LANES):
            blk = g * (gw // LANES) + pr
            xp = xs[:, blk * LANES:(blk + 1) * LANES]
            mix = []
            for half in range(2):
                h = 2 * blk + half
                colb = jnp.broadcast_to(cum[:, h:h + 1], (q, q))
                for s in range(n_seq):
                    last = (s + 1) * seq_rows - 1
                    end_decay[h, s] = jnp.exp2(colb[last:last + 1, :])
                dec = jnp.exp2(jnp.where(mask, colb - cum_t[h:h + 1, :], -jnp.inf))
                mix.append((cb * dec * dt_t[h:h + 1, :]).astype(BF16))
            x2 = jnp.concatenate([jnp.where(lo_half, xp, 0.0), jnp.where(lo_half, 0.0, xp)], axis=0).astype(BF16)
            ydiag.append(_dot(jnp.concatenate(mix, axis=1), x2))
        wx_t = wx[:, gcols].T
        yoff_rows = []
        for s in range(n_seq):
            r0 = s * seq_rows
            st = read_state(s, g)
            yoff_rows.append(_dot_nt(cg[r0:r0 + seq_rows, :], st.astype(BF16)))
            scale = jnp.concatenate(
                [jnp.broadcast_to(end_decay[h, s], (HEAD_DIM, D_STATE))
                 for h in range(g * (gw // HEAD_DIM), (g + 1) * (gw // HEAD_DIM))], axis=0)
            wsel = wx_t
            if n_seq > 1:
                wsel = jnp.where(jnp.logical_and(col >= r0, col < r0 + seq_rows), wx_t, 0.0)
            write_state(s, g, st * scale + _dot(wsel.astype(BF16), bg))
        yoff = yoff_rows[0] if n_seq == 1 else jnp.concatenate(yoff_rows, axis=0)
        y = jnp.concatenate(ydiag, axis=1) + yoff * ecum_e[:, gcols] + xs[:, gcols] * dexp_ref[:, gcols]
        gz = y * sz[:, gcols]
        ms = jnp.sum(gz * gz, axis=-1, keepdims=True) * (1.0 / gw)
        write_out(g, gz * lax.rsqrt(ms + RMS_EPS) * normw_ref[:, gcols])


def _ssd_prompt_kernel(*refs, gw, ways):
    ins, prm = refs[:3 * ways], refs[3 * ways:3 * ways + 5]
    outs, h_ref = refs[3 * ways + 5:5 * ways + 5], refs[5 * ways + 5]
    c = pl.program_id(1)

    @pl.when(c == 0)
    def _():
        h_ref[...] = jnp.zeros(h_ref.shape, F32)

    for k in range(ways):
        sz_ref, xact_ref, dt_ref = ins[3 * k:3 * k + 3]
        y_ref = outs[2 * k]

        def read_state(s, g, k=k):
            return h_ref[k, g * gw:(g + 1) * gw, :]

        def write_state(s, g, v, k=k):
            h_ref[k, g * gw:(g + 1) * gw, :] = v

        def write_out(g, v, y_ref=y_ref):
            y_ref[:, g * gw:(g + 1) * gw] = v.astype(y_ref.dtype)

        _ssd_chunk(xact_ref[...], sz_ref[...], dt_ref[...], prm, sz_ref.shape[0], read_state, write_state, write_out)

    @pl.when(c == pl.num_programs(1) - 1)
    def _():
        for k in range(ways):
            outs[2 * k + 1][0] = h_ref[k]


def _ssd_sample_kernel(sz_ref, ext_ref, dt_ref, hin_ref, convw_ref, convb_ref,
                       cums_ref, spread_ref, alog_ref, dexp_ref, normw_ref, *rest, gw, n_new):
    y_ref, hout_ref = rest[-2:]
    q = sz_ref.shape[0]
    n_seq = q // SAMPLE_ROWS
    first = CONV_LEAD - (CONV_W - 1)

    def window(k):
        return jnp.concatenate(
            [ext_ref[s * SAMPLE_EXT_ROWS + first + k:s * SAMPLE_EXT_ROWS + first + k + SAMPLE_ROWS, :]
             for s in range(n_seq)], axis=0)

    xact = _conv_silu(window, convw_ref, convb_ref)
    row = lax.broadcasted_iota(jnp.int32, (q, LANES), 0)
    dt = jnp.where((row % SAMPLE_ROWS) < n_new, dt_ref[...], 0.0)

    def read_state(s, g):
        return hin_ref[s, g * gw:(g + 1) * gw, :]

    def write_state(s, g, v):
        hout_ref[s, g * gw:(g + 1) * gw, :] = v

    prm = (cums_ref, spread_ref, alog_ref, dexp_ref, normw_ref)

    def write_out(g, v):
        y_ref[:, g * gw:(g + 1) * gw] = v.astype(y_ref.dtype)

    _ssd_chunk(xact, sz_ref[...], dt, prm, SAMPLE_ROWS, read_state, write_state, write_out)


def _ssd_consts(seq_rows, n_heads):
    q = CHUNK
    i = jnp.arange(q)[:, None]
    j = jnp.arange(q)[None, :]
    same = (i // seq_rows) == (j // seq_rows)
    tri = jnp.logical_and(same, j <= i).astype(BF16)
    tris = jnp.logical_and(same, j > i).astype(BF16)
    cums = jnp.tile(jnp.concatenate([tri, tris], axis=0), (1, 3))
    hrow = jnp.arange(LANES)[:, None]
    sel64 = (hrow == (jnp.arange(n_heads * HEAD_DIM)[None, :] // HEAD_DIM)).astype(BF16)
    spread = jnp.tile(sel64, (2, 1))
    return cums, spread


def _ssd_param_specs(prm_arrays):
    return [_const_spec(a.shape) for a in prm_arrays]


def _ssd_prompt(sz, xact, dt, n_seq, seq_len, lp):
    d_ssm = sz.shape[1]
    d_xbc = xact.shape[1]
    q = CHUNK
    nc = seq_len // q
    gw = d_ssm // N_GROUPS
    prm = (*lp["ssd_consts_prompt"], lp["a_log"], lp["d_exp"], lp["norm_w"])
    ways = SSD_WAYS if n_seq % SSD_WAYS == 0 else 1
    per = n_seq // ways
    in_specs, out_specs, out_shape = [], [], []
    for k in range(ways):
        rows = lambda b, c, k=k: ((k * per + b) * nc + c, 0)
        in_specs += [pl.BlockSpec((q, d_ssm), rows), pl.BlockSpec((q, d_xbc), rows), pl.BlockSpec((q, LANES), rows)]
        out_specs += [pl.BlockSpec((q, d_ssm), lambda b, c: (b * nc + c, 0)),
                      pl.BlockSpec((1, d_ssm, D_STATE), lambda b, c: (b, 0, 0))]
        out_shape += [jax.ShapeDtypeStruct((per * seq_len, d_ssm), BF16),
                      jax.ShapeDtypeStruct((per, d_ssm, D_STATE), F32)]
    res = pl.pallas_call(
        functools.partial(_ssd_prompt_kernel, gw=gw, ways=ways),
        grid=(per, nc),
        in_specs=in_specs + _ssd_param_specs(prm),
        out_specs=out_specs,
        out_shape=out_shape,
        scratch_shapes=[pltpu.VMEM((ways, d_ssm, D_STATE), F32)],
        compiler_params=pltpu.CompilerParams(dimension_semantics=("arbitrary", "arbitrary"),
                                             vmem_limit_bytes=_vmem_limit(48 * 1024 * 1024)),
        name="ssd_prompt",
    )(*([sz, xact, dt] * ways), *prm)
    return list(res[0::2]), list(res[1::2])


def _ssd_sample(z16, ext, dtr16, h_all, h_new, layer, lp, n_new):
    d_ssm = z16.shape[1]
    d_xbc = ext.shape[1]
    q = CHUNK
    spc = q // SAMPLE_ROWS
    bs = h_all.shape[1]
    gw = d_ssm // N_GROUPS
    prm = (lp["conv_w"], lp["conv_b"], *lp["ssd_consts_sample"], lp["a_log"], lp["d_exp"], lp["norm_w"])
    rows = lambda i: (i, 0)
    slab =pl.BlockSpec((None, spc, d_ssm, D_STATE), lambda i: (layer, i, 0, 0))
    state_bytes = spc * d_ssm * D_STATE * 4
    prev = [] if h_new is None else [h_new]
    n_in = 4 + len(prm)
    return pl.pallas_call(
        functools.partial(_ssd_sample_kernel, gw=gw, n_new=n_new),
        grid=(bs // spc,),
        in_specs=[pl.BlockSpec((q, d_ssm), rows), pl.BlockSpec((spc * SAMPLE_EXT_ROWS, d_xbc), rows),
                  pl.BlockSpec((q, LANES), rows), slab] + _ssd_param_specs(prm)
                 + [pl.BlockSpec(memory_space=pl.ANY) for _ in prev],
        out_specs=[pl.BlockSpec((q, d_ssm), rows), slab],
        out_shape=[jax.ShapeDtypeStruct((bs * SAMPLE_ROWS, d_ssm), BF16),
                   jax.ShapeDtypeStruct(h_all.shape, F32)],
        input_output_aliases={n_in: 1} if prev else {},
        compiler_params=pltpu.CompilerParams(dimension_semantics=("arbitrary",),
                                             vmem_limit_bytes=_vmem_limit(4 * state_bytes + 20 * 1024 * 1024)),
        name="ssd_sample",
    )(z16, ext, dtr16, h_all, *prm, *prev)


def _route(h, rw_ref, rb_ref, tril_ref, cnt_ref, n_experts):
    xh = h.astype(BF16)
    xl = (h - xh.astype(F32)).astype(BF16)
    w = rw_ref[...]
    wh = w.astype(BF16)
    wl = (w - wh.astype(F32)).astype(BF16)
    hi = _dot(xh, jnp.concatenate([wh, wl], axis=1))
    logits = hi[:, :LANES] + (hi[:, LANES:] + _dot(xl, wh)) + rb_ref[...]
    lane = lax.broadcasted_iota(jnp.int32, logits.shape, 1)
    lane_f = lane.astype(F32)
    logits = jnp.where(lane < n_experts, logits, -jnp.inf)
    m1 = jnp.max(logits, axis=-1, keepdims=True)
    i1 = jnp.min(jnp.where(logits == m1, lane_f, float(LANES)), axis=-1, keepdims=True)
    rest = jnp.where(lane_f == i1, -jnp.inf, logits)
    m2 = jnp.max(rest, axis=-1, keepdims=True)
    i2 = jnp.min(jnp.where(rest == m2, lane_f, float(LANES)), axis=-1, keepdims=True)
    e2 = jnp.exp(m2 - m1)
    den = 1.0 + e2
    oh1 = jnp.where(lane_f == i1, 1.0, 0.0)
    oh2 = jnp.where(lane_f == i2, 1.0, 0.0)
    before = _dot(tril_ref[...], jnp.concatenate([oh1, oh2], axis=1).astype(BF16))
    before1, before2 = before[:, :LANES], before[:, LANES:]
    c1 = jnp.sum(oh1, axis=0, keepdims=True)
    c2 = jnp.sum(oh2, axis=0, keepdims=True)
    base = cnt_ref[...]
    r1 = jnp.sum(oh1 * (before1 + base), axis=-1, keepdims=True)
    r2 = jnp.sum(oh2 * (before2 + (base + c1)), axis=-1, keepdims=True)
    cnt_ref[...] = base + (c1 + c2)
    gate = jnp.where(lane == 0, 1.0 / den, jnp.where(lane == 1, e2 / den, 0.0))
    plan = jnp.where(lane == 0, i1, jnp.where(lane == 1, i2, jnp.where(lane == 2, r1, jnp.where(lane == 3, r2, 0.0))))
    return gate, plan


def _outproj_ln_kernel(*refs, alpha, n_seg, offs, route, n_experts):
    it = iter(refs)
    take = lambda n: [next(it) for _ in range(n)]
    x_refs, p_refs, s_refs = take(n_seg[0]), take(n_seg[1]), take(n_seg[2])
    wp_ref, ws_ref, g_ref, b_ref = take(4)
    x = _seg_read(x_refs, offs[0])
    mixed = _dot(_seg_read(p_refs, offs[1]), wp_ref[...]) + _dot(_seg_read(s_refs, offs[2]), ws_ref[...])
    h = _layer_norm(alpha * x + mixed, g_ref[...], b_ref[...])
    if not route:
        (o_ref,) = take(1)
        o_ref[...] = h
        return
    rw_ref, rb_ref, tril_ref = take(3)
    xt_ref, gate_ref, plan_ref, cnt_ref = take(4)

    @pl.when(pl.program_id(0) == 0)
    def _():
        cnt_ref[...] = jnp.zeros(cnt_ref.shape, F32)

    tm, d = h.shape
    rpt = d // LANES
    for k in range(rpt):
        xt_ref[pl.ds(k, tm, stride=rpt), :] = h[:, k * LANES:(k + 1) * LANES]
    gate, plan = _route(h, rw_ref, rb_ref, tril_ref, cnt_ref, n_experts)
    gate_ref[...] = gate
    plan_ref[...] = plan.T[:plan_ref.shape[0], :]


def _outproj_ln(x_segs, pool_segs, ssd_segs, w_pool, w_ssd, g, b, alpha, router=None):
    d = x_segs[0].shape[1]
    rows = [a.shape[0] for a in x_segs + pool_segs + ssd_segs]
    tm = _common_tile(rows, (512, 256, 128, 64))
    groups = (x_segs, pool_segs, ssd_segs)
    tiles = [_seg_tiles(s, tm) for s in groups]
    t = sum(tiles[0]) * tm
    assert all(sum(ts) * tm == t for ts in tiles)
    in_specs = [sp for s in groups for sp in _seg_specs(s, tm)]
    consts = [w_pool, w_ssd, g, b]
    row = lambda w, dt: (pl.BlockSpec((tm, w), lambda i: (i, 0)), jax.ShapeDtypeStruct((t, w), dt))
    if router is None:
        outs = [row(d, F32)]
        n_experts = 0
    else:
        rw_pad, rb_pad, n_experts = router
        tril = (jnp.arange(tm)[:, None] > jnp.arange(tm)[None, :]).astype(BF16)
        consts += [rw_pad, rb_pad, tril]
        rpt = d // LANES
        outs = [(pl.BlockSpec((tm * rpt, LANES), lambda i: (i, 0)), jax.ShapeDtypeStruct((t * rpt, LANES), F32)),
                row(LANES, F32),
                (pl.BlockSpec((SUBLANES, tm), lambda i: (0, i)), jax.ShapeDtypeStruct((SUBLANES, t), F32)),
                (_const_spec((1, LANES)), jax.ShapeDtypeStruct((1, LANES), F32))]
    in_specs += [_const_spec(c.shape) for c in consts]
    res = pl.pallas_call(
        functools.partial(_outproj_ln_kernel, alpha=alpha, n_seg=[len(s) for s in groups],
                          offs=[_seg_offsets(ts) for ts in tiles], route=router is not None, n_experts=n_experts),
        grid=(t // tm,),
        in_specs=in_specs,
        out_specs=[o[0] for o in outs],
        out_shape=[o[1] for o in outs],
        compiler_params=pltpu.CompilerParams(dimension_semantics=("arbitrary",),
                                             vmem_limit_bytes=_vmem_limit(40 * 1024 * 1024)),
        name="outproj_ln",
    )(*x_segs, *pool_segs, *ssd_segs, *consts)
    return res[0] if router is None else res


def _swiglu(xb, wg_ref, wu_ref, wd_ref):
    act = (_silu(_dot(xb, wg_ref[0])) * _dot(xb, wu_ref[0])).astype(BF16)
    return _dot(act, wd_ref[0])


def _dense_ffn_ln_kernel(*refs, alpha, offs, tiles, n_side):
    x_ref, wg_ref, wu_ref, wd_ref, g_ref, b_ref = refs[:6]
    side_in = refs[6:6 + n_side]
    out_refs = refs[6 + n_side:6 + n_side + len(tiles)]
    side_out = refs[6 + n_side + len(tiles):]
    x = x_ref[...]
    f = _swiglu(x.astype(BF16), wg_ref, wu_ref, wd_ref)
    _seg_write(out_refs, offs, tiles, _layer_norm(alpha * x + f, g_ref[...], b_ref[...]))
    _side_cast(side_in, side_out)


def _dense_ffn_ln(x, wg, wu, wd, g, b, alpha, out_rows, side_casts=()):
    t, d = x.shape
    ff = wg.shape[2]
    tm = _common_tile(out_rows, (256, 128, 64))
    tiles = [n // tm for n in out_rows]
    offs = _seg_offsets(tiles)
    steps = t // tm
    side2d, side_in, side_out, side_shapes, side_bytes = _side_cast_plan(side_casts, steps)
    vmem = 3 * d * ff * 2 + 4 * tm * d * 4 + tm * ff * 12 + side_bytes
    res = pl.pallas_call(
        functools.partial(_dense_ffn_ln_kernel, alpha=alpha, offs=offs, tiles=tiles, n_side=len(side2d)),
        grid=(steps,),
        in_specs=[pl.BlockSpec((tm, d), lambda i: (i, 0)), _const_spec(wg.shape, single=True),
                  _const_spec(wu.shape, single=True), _const_spec(wd.shape, single=True),
                  _const_spec(g.shape), _const_spec(b.shape)] + side_in,
        out_specs=[pl.BlockSpec((tm, d), lambda i, o=o, n=n: (jnp.clip(i - o, 0, n - 1), 0))
                   for o, n in zip(offs, tiles)] + side_out,
        out_shape=[jax.ShapeDtypeStruct((n, d), F32) for n in out_rows] + side_shapes,
        compiler_params=pltpu.CompilerParams(dimension_semantics=("arbitrary",),
                                             vmem_limit_bytes=_vmem_limit(vmem)),
        name="dense_ffn_ln",
    )(x, wg, wu, wd, g, b, *side2d)
    return list(res[:len(tiles)]), [r.reshape(a.shape[1:]) for r, (a, _) in zip(res[len(tiles):], side_casts)]


def _dispatch_kernel(ends_ref, slot_ref, x_ref, out_hbm, zeros_ref, sem, zsem, *, tokens, rpt, tile, n_experts):
    i = pl.program_id(0)

    def tail_copy(e):
        start = pl.multiple_of((ends_ref[e] - tile) * rpt, tile * rpt)
        return pltpu.make_async_copy(zeros_ref, out_hbm.at[pl.ds(start, tile * rpt)], zsem)

    def nonempty(e):
        return ends_ref[e] > (ends_ref[e - 1] if e > 0 else 0)

    def unused_copy(j):
        start = pl.multiple_of((ends_ref[n_experts - 1] + j * tile) * rpt, tile * rpt)
        return pltpu.make_async_copy(zeros_ref, out_hbm.at[pl.ds(start, tile * rpt)], zsem)

    def unused(j):
        return ends_ref[n_experts - 1] + (j + 1) * tile <= out_hbm.shape[0] // rpt

    @pl.when(i == 0)
    def _():
        zeros_ref[...] = jnp.zeros(zeros_ref.shape, zeros_ref.dtype)
        for e in range(n_experts):
            @pl.when(nonempty(e))
            def _(e=e):
                tail_copy(e).start()

            @pl.when(unused(e))
            def _(e=e):
                unused_copy(e).start()
        for e in range(n_experts):
            @pl.when(nonempty(e))
            def _(e=e):
                tail_copy(e).wait()

            @pl.when(unused(e))
            def _(e=e):
                unused_copy(e).wait()

    def issue(r, carry):
        src = x_ref.at[pl.ds(pl.multiple_of(r * rpt, rpt), rpt)]
        for k in range(TOP_K):
            dst = out_hbm.at[pl.ds(pl.multiple_of(slot_ref[0, 0, k * tokens + r] * rpt, rpt), rpt)]
            pltpu.make_async_copy(src, dst, sem).start(priority=k % 2)
        return carry

    lax.fori_loop(0, tokens, issue, 0)
    for k in range(TOP_K):
        pltpu.make_async_copy(x_ref, out_hbm.at[pl.ds(0, tokens * rpt)], sem).wait()


def _dispatch(xt, slots, ends, n_slots, tokens, rpt, n_experts):
    steps = slots.shape[0]
    return pl.pallas_call(
        functools.partial(_dispatch_kernel, tokens=tokens, rpt=rpt, tile=MOE_TILE, n_experts=n_experts),
        grid_spec=pltpu.PrefetchScalarGridSpec(
            num_scalar_prefetch=1, grid=(steps,),
            in_specs=[pl.BlockSpec((1, 1, TOP_K * tokens), lambda i, e: (i, 0, 0), memory_space=pltpu.SMEM),
                      pl.BlockSpec((tokens * rpt, LANES), lambda i, e: (i, 0))],
            out_specs=pl.BlockSpec(memory_space=pl.ANY),
            scratch_shapes=[pltpu.VMEM((MOE_TILE * rpt, LANES), xt.dtype),
                            pltpu.SemaphoreType.DMA(()), pltpu.SemaphoreType.DMA(())]),
        out_shape=jax.ShapeDtypeStruct((n_slots * rpt, LANES), xt.dtype),
        compiler_params=pltpu.CompilerParams(dimension_semantics=("arbitrary",)),
        name="moe_dispatch",
    )(ends, slots, xt)


def _moe_ffn_kernel(te_ref, na_ref, x_ref, wg_ref, wu_ref, wd_ref, o_ref, *, tm, rpt):
    active = pl.program_id(0) < na_ref[0]

    @pl.when(active)
    def _():
        xb = jnp.concatenate([x_ref[pl.ds(k, tm, stride=rpt), :] for k in range(rpt)], axis=1).astype(BF16)
        f = _swiglu(xb, wg_ref, wu_ref, wd_ref)
        for k in range(rpt):
            o_ref[pl.ds(k, tm, stride=rpt), :] = f[:, k * LANES:(k + 1) * LANES]

    @pl.when(jnp.logical_not(active))
    def _():
        o_ref[...] = jnp.zeros(o_ref.shape, o_ref.dtype)


def _moe_ffn(xs, wg, wu, wd, tile_expert, n_active, rpt):
    tm = MOE_TILE
    d, ff = wg.shape[1], wg.shape[2]
    n_tiles = xs.shape[0] // (tm * rpt)
    w_idx = lambda i, te, na: (te[i], 0, 0)
    vmem = 2 * 3 * d * ff * 2 + 4 * tm * d * 4 + tm * ff * 12 + 2 * tm * d * 4
    return pl.pallas_call(
        functools.partial(_moe_ffn_kernel, tm=tm, rpt=rpt),
        grid_spec=pltpu.PrefetchScalarGridSpec(
            num_scalar_prefetch=2, grid=(n_tiles,),
            in_specs=[pl.BlockSpec((tm * rpt, LANES), lambda i, te, na: (jnp.minimum(i, na[0] - 1), 0)),
                      pl.BlockSpec((1, d, ff), w_idx), pl.BlockSpec((1, d, ff), w_idx),
                      pl.BlockSpec((1, ff, d), w_idx)],
            out_specs=pl.BlockSpec((tm * rpt, LANES), lambda i, te, na: (i, 0))),
        out_shape=jax.ShapeDtypeStruct(xs.shape, F32),
        compiler_params=pltpu.CompilerParams(dimension_semantics=("arbitrary",),
                                             vmem_limit_bytes=_vmem_limit(vmem)),
        name="moe_ffn",
    )(tile_expert, n_active, xs, wg, wu, wd)


def _combine_ln_kernel(slot_ref, next_ref, x_ref, gate_ref, y_hbm, g_ref, b_ref, *rest, alpha, tm, rpt, offs, tiles):
    out_refs, (buf, sem) = rest[:len(tiles)], rest[len(tiles):]
    i = pl.program_id(0)
    n_rows = TOP_K * tm * rpt

    def issue(s_ref, slot):
        def body(r, carry):
            for k in range(TOP_K):
                row = k * tm + r
                src = y_hbm.at[pl.ds(pl.multiple_of(s_ref[0, 0, row] * rpt, rpt), rpt)]
                dst = buf.at[slot, pl.ds(pl.multiple_of(row * rpt, rpt), rpt)]
                pltpu.make_async_copy(src, dst, sem.at[slot]).start(priority=k % 2)
            return carry
        lax.fori_loop(0, tm, body, 0)

    @pl.when(i == 0)
    def _():
        issue(slot_ref, 0)

    @pl.when(i + 1 < pl.num_programs(0))
    def _():
        issue(next_ref, (i + 1) % 2)

    cur = i % 2
    pltpu.make_async_copy(y_hbm.at[pl.ds(0, n_rows)], buf.at[cur], sem.at[cur]).wait()
    g1 = gate_ref[:, 0:1]
    g2 = gate_ref[:, 1:2]
    cols = []
    for k in range(rpt):
        xk = x_ref[pl.ds(k, tm, stride=rpt), :]
        ya = buf[cur, pl.ds(k, tm, stride=rpt), :]
        yb = buf[cur, pl.ds(tm * rpt + k, tm, stride=rpt), :]
        cols.append(alpha * xk + (g1 * ya + g2 * yb))
    h = jnp.concatenate(cols, axis=1)
    _seg_write(out_refs, offs, tiles, _layer_norm(h, g_ref[...], b_ref[...]))


def _combine_ln(xt, gate, ys, slots, g, b, alpha, tm, rpt, out_rows):
    d = g.shape[1]
    tiles = [n // tm for n in out_rows]
    offs = _seg_offsets(tiles)
    steps = slots.shape[0]
    slot_spec = lambda f: pl.BlockSpec((1, 1, TOP_K * tm), f, memory_space=pltpu.SMEM)
    res = pl.pallas_call(
        functools.partial(_combine_ln_kernel, alpha=alpha, tm=tm, rpt=rpt, offs=offs, tiles=tiles),
        grid=(steps,),
        in_specs=[slot_spec(lambda i: (i, 0, 0)), slot_spec(lambda i: (jnp.minimum(i + 1, steps - 1), 0, 0)),
                  pl.BlockSpec((tm * rpt, LANES), lambda i: (i, 0)), pl.BlockSpec((tm, LANES), lambda i: (i, 0)),
                  pl.BlockSpec(memory_space=pl.ANY), _const_spec(g.shape), _const_spec(b.shape)],
        out_specs=[pl.BlockSpec((tm, d), lambda i, o=o, n=n: (jnp.clip(i - o, 0, n - 1), 0))
                   for o, n in zip(offs, tiles)],
        out_shape=[jax.ShapeDtypeStruct((n, d), F32) for n in out_rows],
        scratch_shapes=[pltpu.VMEM((2, TOP_K * tm * rpt, LANES), F32), pltpu.SemaphoreType.DMA((2,))],
        compiler_params=pltpu.CompilerParams(dimension_semantics=("arbitrary",),
                                             vmem_limit_bytes=_vmem_limit(32 * 1024 * 1024)),
        name="moe_combine_ln",
    )(slots, slots, xt, gate, ys, g, b)
    return list(res)


def _moe_ffn_ln(xt, gate, plan, counts, wg, wu, wd, g, b, alpha, tm, out_rows):
    n_experts = wg.shape[0]
    d = wg.shape[1]
    rpt = d // LANES
    t = gate.shape[0]
    tile = MOE_TILE
    cnt = counts[0, :n_experts].astype(jnp.int32)
    padded = ((cnt + tile - 1) // tile) * tile
    ends = jnp.cumsum(padded).astype(jnp.int32)
    starts = ends - padded
    plan = plan.astype(jnp.int32)
    slot = plan[TOP_K:2 * TOP_K]
    for e in range(n_experts):
        slot = slot + jnp.where(plan[:TOP_K] == e, starts[e], 0)
    n_slots = -(-(TOP_K * t + n_experts * (tile - 1)) // tile) * tile
    n_tiles = n_slots // tile
    n_active = ends[-1:] // tile
    tile_start = jnp.minimum(jnp.arange(n_tiles, dtype=jnp.int32), n_active[0] - 1) * tile
    tile_expert = jnp.sum((tile_start[:, None] >= ends[None, :]).astype(jnp.int32), axis=1)
    slots = jnp.swapaxes(slot.reshape(TOP_K, t // tm, tm), 0, 1).reshape(t // tm, 1, TOP_K * tm)
    xs = _dispatch(xt, slots, ends, n_slots, tm, rpt, n_experts)
    ys = _moe_ffn(xs, wg, wu, wd, tile_expert, n_active, rpt)
    return _combine_ln(xt, gate, ys, slots, g, b, alpha, tm, rpt, out_rows)


def kernel(x_prompt, x_sample, state_pool, state_conv, state_ssm, w_in, conv_w, conv_b, dt_bias, A_log, D_skip,
           ssm_norm_w, pool_w, pool_scale, w_out, ln1_g, ln1_b, ln2_g, ln2_b, ffn_w_gate, ffn_w_up, ffn_w_down,
           router_w, router_b, moe_w_gate, moe_w_up, moe_w_down):
    bp, seq, d = x_prompt.shape
    bs, n_new, _ = x_sample.shape
    depth = w_in.shape[0]
    d_pool = pool_scale.shape[1]
    d_ssm = ssm_norm_w.shape[1]
    d_xbc = conv_w.shape[2]
    n_heads = dt_bias.shape[1]
    tp, ts = bp * seq, bs * n_new
    alpha = (2.0 * depth) ** 0.25
    assert d_pool == POOL_GROUP * len(POOL_WINDOWS) and d_ssm == n_heads * HEAD_DIM
    assert d_xbc == d_ssm + 2 * N_GROUPS * D_STATE and seq % CHUNK == 0 and n_new <= CONV_W
    assert n_heads <= LANES and bs % (CHUNK // SAMPLE_ROWS) == 0 and d % LANES == 0

    consts_prompt = _ssd_consts(CHUNK, n_heads)
    consts_sample = _ssd_consts(SAMPLE_ROWS, n_heads)
    x_segs = [x_prompt.reshape(tp, d), x_sample.reshape(ts, d)]
    new_pool_p, new_conv_p, new_ssm_p, new_pool_s, new_conv_s = [], [], [], [], []
    ssm_s = None
    n_in = w_in.shape[2]
    w_in_pad_all = jnp.pad(w_in.astype(BF16), ((0, 0), (0, 0), (0, d_pool + d_ssm + d_xbc + LANES - n_in)))
    for l in range(depth):
        out_rows = [tp, ts]
        w_in_pad = w_in_pad_all[l]
        lp = dict(
            conv_w=conv_w[l], conv_b=conv_b[l][None, :],
            dt_bias=jnp.zeros((1, LANES), F32).at[0, :n_heads].set(dt_bias[l]),
            a_log=jnp.zeros((1, LANES), F32).at[0, :n_heads].set(A_log[l]),
            d_exp=jnp.repeat(D_skip[l], HEAD_DIM)[None, :], norm_w=ssm_norm_w[l][None, :],
            ssd_consts_prompt=consts_prompt, ssd_consts_sample=consts_sample)
        pw = pool_w[l].astype(BF16)
        ps = pool_scale[l][None, :]

        proj = (w_in_pad, lp["conv_w"], lp["conv_b"], lp["dt_bias"], d_pool, d_ssm, d_xbc)

        dense = l % 2 == 0
        side = ((ffn_w_gate, l // 2), (ffn_w_up, l // 2), (ffn_w_down, l // 2)) if dense else ()
        (u_p, sz_p, xact_p, dt_p, c_p), cast = _in_proj(x_segs[0], *proj, seq_len=seq, side_casts=side)
        ffn_bf16 = [c[None] for c in cast] if dense else None
        pool_p, buf_p = _pool_prompt(u_p, tp, seq, pw, ps)
        y_p, h_p = _ssd_prompt(sz_p, xact_p, dt_p, bp, seq, lp)
        new_pool_p.append(buf_p)
        new_conv_p.append(c_p)
        new_ssm_p.append(jnp.concatenate(h_p, axis=0).reshape(bp, n_heads, HEAD_DIM, D_STATE))

        u_s, sz_s, xbc_s, dt_s = _in_proj_sample(x_segs[1], w_in_pad, lp["dt_bias"], d_pool, d_ssm, d_xbc)
        u_s = u_s.reshape(bs, n_new, d_pool)
        xbc_s = xbc_s.reshape(bs, n_new, d_xbc)
        pool_ext = jnp.concatenate([state_pool[l], u_s], axis=1)
        pool_s = _pool_sample(jnp.swapaxes(pool_ext, 0, 1), pw, ps, n_new, PAST_LEN)
        pool_s = jnp.swapaxes(pool_s, 0, 1).reshape(ts, d_pool)
        conv_ext = jnp.concatenate([state_conv[l], xbc_s], axis=1)
        lead = CONV_LEAD - (CONV_W - 1)
        ext = jnp.pad(conv_ext, ((0, 0), (lead, SAMPLE_EXT_ROWS - lead - conv_ext.shape[1]), (0, 0)))
        pad_rows = lambda a: jnp.pad(a.reshape(bs, n_new, -1), ((0, 0), (0, SAMPLE_ROWS - n_new), (0, 0))
                                     ).reshape(bs * SAMPLE_ROWS, -1)
        y_s16, ssm_s = _ssd_sample(pad_rows(sz_s), ext.reshape(bs * SAMPLE_EXT_ROWS, d_xbc), pad_rows(dt_s),
                                   state_ssm.reshape(depth, bs, d_ssm, D_STATE), ssm_s, l, lp, n_new)
        y_s = y_s16.reshape(bs, SAMPLE_ROWS, d_ssm)[:, :n_new].reshape(ts, d_ssm)
        new_pool_s.append(pool_ext[:, n_new:])
        new_conv_s.append(conv_ext[:, n_new:])

        w_o = w_out[l].astype(BF16)
        ln1 = (ln1_g[l][None, :], ln1_b[l][None, :])
        g2, b2 = ln2_g[l][None, :], ln2_b[l][None, :]
        j = l // 2
        if l % 2 == 0:
            x1 = _outproj_ln(x_segs, [pool_p, pool_s], [*y_p, y_s], w_o[:d_pool], w_o[d_pool:], *ln1, alpha)
            side = ((moe_w_gate, j), (moe_w_up, j), (moe_w_down, j)) if l + 1 < depth else ()
            x_segs, moe_bf16 = _dense_ffn_ln(x1, *ffn_bf16, g2, b2, alpha, out_rows, side)
        else:
            n_experts = router_w.shape[2]
            rw_pad = jnp.zeros((d, LANES), F32).at[:, :n_experts].set(router_w[j])
            rb_pad = jnp.zeros((1, LANES), F32).at[0, :n_experts].set(router_b[j])
            xt, gate, plan, counts = _outproj_ln(x_segs, [pool_p, pool_s], [*y_p, y_s], w_o[:d_pool], w_o[d_pool:],
                                                 *ln1, alpha, router=(rw_pad, rb_pad, n_experts))
            tm = _common_tile([tp, ts], (512, 256, 128, 64))
            x_segs = _moe_ffn_ln(xt, gate, plan, counts, *moe_bf16, g2, b2, alpha, tm, out_rows)

    y_prompt, y_sample = x_segs
    return (y_prompt.reshape(bp, seq, d), y_sample.reshape(bs, n_new, d),
            jnp.stack(new_pool_p), jnp.stack(new_conv_p), jnp.stack(new_ssm_p),
            jnp.stack(new_pool_s), jnp.stack(new_conv_s),
            ssm_s.reshape(depth, bs, n_heads, HEAD_DIM, D_STATE))
```
